```python
import jax, jax.numpy as jnp
from jax import lax
import numpy as np

D_MODEL = 2048
BATCH = 4
SEQ = 8192
DEPTH = 1

HEAD_DIM = 128
SWA_GROUPS = ((128, 1), (512, 4), (2048, 16))
SWA_HEADS_PER_GROUP = 4
SWA_HEADS = SWA_HEADS_PER_GROUP * len(SWA_GROUPS)
SWA_WIDTH = SWA_HEADS * HEAD_DIM
SWA_OUT_WIDTH = SWA_HEADS_PER_GROUP * HEAD_DIM
SWA_BLOCK = 64
ROPE_THETA = 500000.0
ROPE_DIMS = HEAD_DIM // 4
GDN_HEADS = 12
GDN_WIDTH = GDN_HEADS * HEAD_DIM
GDN_CONV = 5
GDN_CHUNK = 64
MEM_TOKENS = 256
MEM_HEADS = 4
MEM_HEAD_DIM = 256
MEM_WIDTH = MEM_HEADS * MEM_HEAD_DIM
N_BRANCH = 3
IN_SPLITS = (SWA_WIDTH, SWA_WIDTH, SWA_WIDTH,
             GDN_WIDTH, GDN_WIDTH, GDN_WIDTH, GDN_WIDTH,
             2 * GDN_HEADS, 2 * GDN_HEADS,
             MEM_WIDTH, N_BRANCH * D_MODEL)
IN_WIDTH = sum(IN_SPLITS)
N_GROUPS = 4
EXPERTS_PER_GROUP = 8
N_EXPERTS = N_GROUPS * EXPERTS_PER_GROUP
EXPERT_TOPK = 2
D_EXPERT = 512
MOE_BLOCK = 256
EPS = 1e-6
NEG_INF = -1e30

kernel_name = "hybrid_gated_dilated_swa_gdn_memxattn_hiermoe"


def rmsnorm(x, g):
    xf = x.astype(jnp.float32)
    r = lax.rsqrt(jnp.mean(xf * xf, axis=-1, keepdims=True) + EPS)
    return (xf * r).astype(x.dtype) * g


def l2norm(t):
    tf = t.astype(jnp.float32)
    return (tf * lax.rsqrt(jnp.sum(tf * tf, axis=-1, keepdims=True) + EPS)).astype(t.dtype)


def partial_rope(x, pos):
    half = ROPE_DIMS // 2
    inv = ROPE_THETA ** (-jnp.arange(half, dtype=jnp.float32) / half)
    ang = pos.astype(jnp.float32)[:, None] * inv[None, :]
    cos = jnp.cos(ang)[None, :, None, :]
    sin = jnp.sin(ang)[None, :, None, :]
    x1 = x[..., :half].astype(jnp.float32)
    x2 = x[..., half:ROPE_DIMS].astype(jnp.float32)
    rot = jnp.concatenate([x1 * cos - x2 * sin, x2 * cos + x1 * sin], axis=-1).astype(x.dtype)
    return jnp.concatenate([rot, x[..., ROPE_DIMS:]], axis=-1)


def banded_attention(q, k, v, radius):
    N, L, H, hd = q.shape
    blk = SWA_BLOCK
    nb = -(-L // blk)
    Lp = nb * blk
    pad = Lp - L
    qb = jnp.pad(q, ((0, 0), (0, pad), (0, 0), (0, 0))).reshape(N, nb, blk, H, hd)
    kp = jnp.pad(k, ((0, 0), (blk, pad + blk), (0, 0), (0, 0))).reshape(N, nb + 2, blk, H, hd)
    vp = jnp.pad(v, ((0, 0), (blk, pad + blk), (0, 0), (0, 0))).reshape(N, nb + 2, blk, H, hd)
    kb = jnp.concatenate([kp[:, :-2], kp[:, 1:-1], kp[:, 2:]], axis=2)
    vb = jnp.concatenate([vp[:, :-2], vp[:, 1:-1], vp[:, 2:]], axis=2)
    s = jnp.einsum('ncqhd,nckhd->nchqk', qb, kb,
                   preferred_element_type=jnp.float32) * (hd ** -0.5)
    qpos = jnp.arange(nb)[:, None] * blk + jnp.arange(blk)[None, :]
    kpos = (jnp.arange(nb)[:, None] - 1) * blk + jnp.arange(3 * blk)[None, :]
    off = kpos[:, None, :] - qpos[:, :, None]
    valid = (jnp.abs(off) <= radius) & (kpos[:, None, :] >= 0) & (kpos[:, None, :] < L)
    s = jnp.where(valid[None, :, None], s, NEG_INF)
    m = jnp.max(s, axis=-1, keepdims=True)
    p = jnp.exp(s - m)
    l = jnp.sum(p, axis=-1, keepdims=True)
    o = jnp.einsum('nchqk,nckhd->ncqhd', (p / l).astype(v.dtype), vb)
    lse = (m + jnp.log(l))[..., 0].transpose(0, 1, 3, 2).reshape(N, Lp, H)
    return o.reshape(N, Lp, H, hd)[:, :L], lse[:, :L]


def dilated_swa(q, k, v):
    B, S, _, hd = q.shape
    H = SWA_HEADS_PER_GROUP
    outs, lses = [], []
    for g, (window, dil) in enumerate(SWA_GROUPS):
        radius = window // (2 * dil)
        Ld = S // dil

        def to_residues(t):
            t = t[:, :, g * H:(g + 1) * H]
            return t.reshape(B, Ld, dil, H, hd).transpose(0, 2, 1, 3, 4).reshape(B * dil, Ld, H, hd)

        o, lse = banded_attention(to_residues(q), to_residues(k), to_residues(v), radius)
        outs.append(o.reshape(B, dil, Ld, H, hd).transpose(0, 2, 1, 3, 4).reshape(B, S, H, hd))
        lses.append(lse.reshape(B, dil, Ld, H).transpose(0, 2, 1, 3).reshape(B, S, H))
    alpha = jax.nn.softmax(jnp.stack(lses), axis=0)
    return jnp.einsum('gbsh,gbshd->bshd', alpha.astype(q.dtype), jnp.stack(outs))


def centred_depthwise_conv(x, w):
    K, C = w.shape
    return lax.conv_general_dilated(
        x, w[:, None, :].astype(x.dtype), window_strides=(1,),
        padding=[(K // 2, K // 2)], dimension_numbers=('NWC', 'WIO', 'NWC'),
        feature_group_count=C)


def gdn_chunked(q, k, v, g, beta):
    f32 = jnp.float32
    lead = q.shape[:-2]
    S, dk = q.shape[-2:]
    dv = v.shape[-1]
    C = GDN_CHUNK
    nc = S // C
    q = q.astype(f32).reshape(*lead, nc, C, dk)
    k = k.astype(f32).reshape(*lead, nc, C, dk)
    v = v.astype(f32).reshape(*lead, nc, C, dv)
    b = beta.astype(f32).reshape(*lead, nc, C, 1)
    gc = jnp.cumsum(g.astype(f32).reshape(*lead, nc, C), axis=-1)
    i = jnp.arange(C)
    strict = i[:, None] > i[None, :]
    incl = i[:, None] >= i[None, :]
    decay = jnp.exp(jnp.where(incl, gc[..., :, None] - gc[..., None, :], -jnp.inf))
    kb = k * b
    lmat = jnp.where(strict, jnp.einsum('...id,...jd->...ij', kb, k) * decay, 0.0)
    egc = jnp.exp(gc)[..., None]
    u = lax.linalg.triangular_solve(lmat, jnp.concatenate([v * b, kb * egc], axis=-1),
                                    left_side=True, lower=True, unit_diagonal=True)
    value, kcd = u[..., :dv], u[..., dv:]
    attn = jnp.einsum('...id,...jd->...ij', q, k) * decay
    q_e = q * egc
    k_e = k * jnp.exp(gc[..., -1:] - gc)[..., None]
    g_e = jnp.exp(gc[..., -1])[..., None, None]
    xs = tuple(jnp.moveaxis(t, -3, 0) for t in (value, kcd, attn, q_e, k_e, g_e))

    def step(state, c):
        value_c, kcd_c, attn_c, qe_c, ke_c, ge_c = c
        v_new = value_c - jnp.einsum('...ck,...kv->...cv', kcd_c, state)
        o = (jnp.einsum('...ck,...kv->...cv', qe_c, state)
             + jnp.einsum('...ij,...jv->...iv', attn_c, v_new))
        state = state * ge_c + jnp.einsum('...ck,...cv->...kv', ke_c, v_new)
        return state, o

    state0 = jnp.zeros((*lead, dk, dv), f32)
    _, o = lax.scan(step, state0, xs)
    return jnp.moveaxis(o, 0, -3).reshape(*lead, S, dv)


def bidirectional_gdn(q, k, v, g, beta):
    def dirs(t):
        return jnp.stack([t, t[:, ::-1]]).transpose(0, 1, 3, 2, 4)

    gd = jnp.stack([g[:, :, 0], g[:, ::-1, 1]]).transpose(0, 1, 3, 2)
    bd = jnp.stack([beta[:, :, 0], beta[:, ::-1, 1]]).transpose(0, 1, 3, 2)
    o = gdn_chunked(dirs(q), dirs(k), dirs(v), gd, bd)
    o = o[0] + o[1][:, :, ::-1]
    return o.transpose(0, 2, 1, 3).astype(q.dtype)


def hier_moe(x, w_route_group, b_route_group, w_route_expert, w_expert_gate, w_expert_up, w_expert_down):
    B, S, D = x.shape
    T = B * S
    K = EXPERT_TOPK
    xt = x.reshape(T, D)
    grp_logits = jnp.dot(xt, w_route_group, preferred_element_type=jnp.float32) + b_route_group.astype(jnp.float32)
    grp = jnp.argmax(grp_logits, axis=-1)
    p_grp = jnp.take_along_axis(jax.nn.softmax(grp_logits, axis=-1), grp[:, None], axis=-1)
    exp_logits = jnp.dot(xt, w_route_expert, preferred_element_type=jnp.float32).reshape(T, N_GROUPS, EXPERTS_PER_GROUP)
    sel = jnp.take_along_axis(exp_logits, grp[:, None, None], axis=1)[:, 0]
    top_val, top_idx = lax.top_k(sel, K)
    wts = jax.nn.softmax(top_val, axis=-1) * p_grp
    eid = (grp[:, None] * EXPERTS_PER_GROUP + top_idx).astype(jnp.int32)
    N = T * K
    e_flat = eid.reshape(N)
    t_flat = jnp.repeat(jnp.arange(T, dtype=jnp.int32), K)
    w_flat = wts.reshape(N)
    order = jnp.argsort(e_flat)
    e_s, t_s, w_s = e_flat[order], t_flat[order], w_flat[order]
    counts = jnp.bincount(e_flat, length=N_EXPERTS)
    padded = (counts + MOE_BLOCK - 1) // MOE_BLOCK * MOE_BLOCK
    start = jnp.cumsum(counts) - counts
    pad_end = jnp.cumsum(padded)
    pad_start = pad_end - padded
    dest = pad_start[e_s] + (jnp.arange(N, dtype=jnp.int32) - start[e_s])
    nblk = -(-N // MOE_BLOCK) + N_EXPERTS
    P = nblk * MOE_BLOCK
    slot_tok = jnp.zeros((P,), jnp.int32).at[dest].set(t_s)
    slot_w = jnp.zeros((P,), jnp.float32).at[dest].set(w_s)
    blk_expert = jnp.minimum(jnp.searchsorted(pad_end, jnp.arange(nblk) * MOE_BLOCK, side='right'),
                             N_EXPERTS - 1).astype(jnp.int32)

    def expert_block(args):
        tok, e = args
        xb = xt[tok]
        hmid = jax.nn.silu(xb @ w_expert_gate[e]) * (xb @ w_expert_up[e])
        return hmid @ w_expert_down[e]

    yb = lax.map(expert_block, (slot_tok.reshape(nblk, MOE_BLOCK), blk_expert))
    y = yb.reshape(P, D) * slot_w[:, None].astype(x.dtype)
    return jnp.zeros((T, D), x.dtype).at[slot_tok].add(y).reshape(B, S, D)


def hybrid_layer(x, mem, g_mix, w_in, b_gate, gdn_conv, gdn_a_log, gdn_dt_bias, gdn_norm_g,
                 g_mem, w_mem_kv, w_o_swa, w_o_gdn, w_o_mem, w_out, g_ffn,
                 w_route_group, b_route_group, w_route_expert, w_expert_gate, w_expert_up, w_expert_down):
    B, S, D = x.shape
    hd = HEAD_DIM
    a = rmsnorm(x, g_mix)
    proj = a @ w_in
    (aq, ak, av, bq, bk, bv, bz, b_beta, b_alpha, mq, gate_logits) = jnp.split(
        proj, [int(i) for i in np.cumsum(IN_SPLITS)[:-1]], axis=-1)
    pos = jnp.arange(S)

    aq = partial_rope(aq.reshape(B, S, SWA_HEADS, hd), pos)
    ak = partial_rope(ak.reshape(B, S, SWA_HEADS, hd), pos)
    o_a = dilated_swa(aq, ak, av.reshape(B, S, SWA_HEADS, hd))
    y_a = o_a.reshape(B, S, SWA_OUT_WIDTH) @ w_o_swa

    qkv = jax.nn.silu(centred_depthwise_conv(jnp.concatenate([bq, bk, bv], axis=-1), gdn_conv))
    q_b, k_b, v_b = jnp.split(qkv, [GDN_WIDTH, 2 * GDN_WIDTH], axis=-1)
    q_b = l2norm(q_b.reshape(B, S, GDN_HEADS, hd)) * (hd ** -0.5)
    k_b = l2norm(k_b.reshape(B, S, GDN_HEADS, hd))
    v_b = v_b.reshape(B, S, GDN_HEADS, hd)
    beta = jax.nn.sigmoid(b_beta.reshape(B, S, 2, GDN_HEADS).astype(jnp.float32))
    g_log = -jnp.exp(gdn_a_log.astype(jnp.float32)) * jax.nn.softplus(
        b_alpha.reshape(B, S, 2, GDN_HEADS).astype(jnp.float32) + gdn_dt_bias.astype(jnp.float32))
    o_b = bidirectional_gdn(q_b, k_b, v_b, g_log, beta)
    o_b = rmsnorm(o_b, gdn_norm_g) * jax.nn.silu(bz.reshape(B, S, GDN_HEADS, hd))
    y_b = o_b.reshape(B, S, GDN_WIDTH) @ w_o_gdn

    kv = rmsnorm(mem, g_mem) @ w_mem_kv
    mk, mv = jnp.split(kv, 2, axis=-1)
    mk = mk.reshape(B, MEM_TOKENS, MEM_HEADS, MEM_HEAD_DIM)
    mv = mv.reshape(B, MEM_TOKENS, MEM_HEADS, MEM_HEAD_DIM)
    s_m = jnp.einsum('bshd,bmhd->bhsm', mq.reshape(B, S, MEM_HEADS, MEM_HEAD_DIM), mk,
                     preferred_element_type=jnp.float32) * (MEM_HEAD_DIM ** -0.5)
    p_m = jax.nn.softmax(s_m, axis=-1).astype(mv.dtype)
    o_m = jnp.einsum('bhsm,bmhd->bshd', p_m, mv)
    y_m = o_m.reshape(B, S, MEM_WIDTH) @ w_o_mem

    gates = jax.nn.sigmoid(gate_logits.reshape(B, S, N_BRANCH, D) + b_gate)
    mix = gates[:, :, 0] * y_a + gates[:, :, 1] * y_b + gates[:, :, 2] * y_m
    x = x + mix @ w_out

    x = x + hier_moe(rmsnorm(x, g_ffn), w_route_group, b_route_group, w_route_expert,
                     w_expert_gate, w_expert_up, w_expert_down)
    return x


def setup_inputs(seed: int = 0) -> dict:
    key = jax.random.key(seed)
    ks = jax.random.split(key, 24)
    f32 = jnp.float32
    L = DEPTH

    def nrm(k, shape, fan):
        return jax.random.normal(k, shape, f32) * (fan ** -0.5)

    def gain(k, shape):
        return 1.0 + 0.02 * jax.random.normal(k, shape, f32)

    dt = jax.random.uniform(ks[7], (L, 2, GDN_HEADS), f32, 1e-3, 1e-1)
    return {
        "x": jax.random.normal(ks[0], (BATCH, SEQ, D_MODEL), f32),
        "mem": jax.random.normal(ks[1], (BATCH, MEM_TOKENS, D_MODEL), f32),
        "g_mix": gain(ks[2], (L, D_MODEL)),
        "w_in": nrm(ks[3], (L, D_MODEL, IN_WIDTH), D_MODEL),
        "b_gate": 0.02 * jax.random.normal(ks[4], (L, N_BRANCH, D_MODEL), f32),
        "gdn_conv": nrm(ks[5], (L, GDN_CONV, 3 * GDN_WIDTH), GDN_CONV),
        "gdn_a_log": jnp.log(jax.random.uniform(ks[6], (L, 2, GDN_HEADS), f32, 1.0, 16.0)),
        "gdn_dt_bias": dt + jnp.log(-jnp.expm1(-dt)),
        "gdn_norm_g": gain(ks[8], (L, HEAD_DIM)),
        "g_mem": gain(ks[9], (L, D_MODEL)),
        "w_mem_kv": nrm(ks[10], (L, D_MODEL, 2 * MEM_WIDTH), D_MODEL),
        "w_o_swa": nrm(ks[11], (L, SWA_OUT_WIDTH, D_MODEL), SWA_OUT_WIDTH),
        "w_o_gdn": nrm(ks[12], (L, GDN_WIDTH, D_MODEL), GDN_WIDTH),
        "w_o_mem": nrm(ks[13], (L, MEM_WIDTH, D_MODEL), MEM_WIDTH),
        "w_out": nrm(ks[14], (L, D_MODEL, D_MODEL), D_MODEL),
        "g_ffn": gain(ks[15], (L, D_MODEL)),
        "w_route_group": nrm(ks[16], (L, D_MODEL, N_GROUPS), D_MODEL),
        "b_route_group": 0.01 * jax.random.normal(ks[17], (L, N_GROUPS), f32),
        "w_route_expert": nrm(ks[18], (L, D_MODEL, N_EXPERTS), D_MODEL),
        "w_expert_gate": nrm(ks[19], (L, N_EXPERTS, D_MODEL, D_EXPERT), D_MODEL),
        "w_expert_up": nrm(ks[20], (L, N_EXPERTS, D_MODEL, D_EXPERT), D_MODEL),
        "w_expert_down": nrm(ks[21], (L, N_EXPERTS, D_EXPERT, D_MODEL), D_EXPERT),
        "g_final": gain(ks[22], (D_MODEL,)),
    }


def reference(x, mem, g_mix, w_in, b_gate, gdn_conv, gdn_a_log, gdn_dt_bias, gdn_norm_g,
              g_mem, w_mem_kv, w_o_swa, w_o_gdn, w_o_mem, w_out, g_ffn,
              w_route_group, b_route_group, w_route_expert, w_expert_gate, w_expert_up,
              w_expert_down, g_final):
    h = x
    for layer in range(DEPTH):
        h = hybrid_layer(h, mem, g_mix[layer], w_in[layer], b_gate[layer], gdn_conv[layer],
                         gdn_a_log[layer], gdn_dt_bias[layer], gdn_norm_g[layer], g_mem[layer],
                         w_mem_kv[layer], w_o_swa[layer], w_o_gdn[layer], w_o_mem[layer], w_out[layer],
                         g_ffn[layer], w_route_group[layer], b_route_group[layer], w_route_expert[layer],
                         w_expert_gate[layer], w_expert_up[layer], w_expert_down[layer])
    return rmsnorm(h, g_final)
```

```python
import functools

import jax
import jax.numpy as jnp
from jax import lax
from jax.experimental import pallas as pl
from jax.experimental.pallas import tpu as pltpu

F32 = jnp.float32
BF16 = jnp.bfloat16

HEAD_DIM = 128
SWA_GROUPS = ((128, 1), (512, 4), (2048, 16))
SWA_HEADS_PER_GROUP = 4
SWA_HEADS = SWA_HEADS_PER_GROUP * len(SWA_GROUPS)
SWA_WIDTH = SWA_HEADS * HEAD_DIM
SWA_BLOCK = 64
ROPE_THETA = 500000.0
ROPE_DIMS = HEAD_DIM // 4
GDN_HEADS = 12
GDN_WIDTH = GDN_HEADS * HEAD_DIM
GDN_CONV = 5
GDN_CHUNK = 64
MEM_HEADS = 4
MEM_HEAD_DIM = 256
MEM_WIDTH = MEM_HEADS * MEM_HEAD_DIM
N_BRANCH = 3
N_GROUPS = 4
EXPERTS_PER_GROUP = 8
N_EXPERTS = N_GROUPS * EXPERTS_PER_GROUP
EXPERT_TOPK = 2
MOE_BLOCK = 256
EPS = 1e-6
NEG_INF = -1e30

LANES = 128
SUBLANES = 8
VMEM_LIMIT = 48 * 1024 * 1024


def _cp(*sem, vmem=VMEM_LIMIT):
    return pltpu.CompilerParams(dimension_semantics=sem, vmem_limit_bytes=vmem)


def _tile(n, pref):
    t = min(n, pref)
    while n % t:
        t //= 2
    return t


def _rmsnorm_kernel(x_ref, g_ref, o_ref):
    x = x_ref[...].astype(F32)
    r = lax.rsqrt(jnp.mean(x * x, axis=-1, keepdims=True) + EPS)
    o_ref[...] = (x * r * g_ref[...]).astype(o_ref.dtype)


def rmsnorm_rows(x, g, out_dtype):
    m, d = x.shape
    tm = _tile(m, 512)
    return pl.pallas_call(
        _rmsnorm_kernel,
        out_shape=jax.ShapeDtypeStruct((m, d), out_dtype),
        grid=(m // tm,),
        in_specs=[pl.BlockSpec((tm, d), lambda i: (i, 0)), pl.BlockSpec((1, d), lambda i: (0, 0))],
        out_specs=pl.BlockSpec((tm, d), lambda i: (i, 0)),
        compiler_params=_cp("parallel"),
        name="rmsnorm_rows",
    )(x, g.reshape(1, d))


def _mm_kernel(a_ref, w_ref, o_ref):
    o_ref[...] = jnp.dot(a_ref[...], w_ref[...], preferred_element_type=F32).astype(o_ref.dtype)


def _mm_sigmoid_kernel(a_ref, w_ref, b_ref, o_ref):
    z = jnp.dot(a_ref[...], w_ref[...], preferred_element_type=F32) + b_ref[...]
    o_ref[...] = jax.nn.sigmoid(z).astype(o_ref.dtype)


def _mm_residual_kernel(a_ref, w_ref, r_ref, o_ref):
    o_ref[...] = r_ref[...] + jnp.dot(a_ref[...], w_ref[...], preferred_element_type=F32)


def _mm_rope_kernel(a_ref, w_ref, c_ref, s1_ref, s2_ref, o_ref, *, n_rope_tiles):
    acc = jnp.dot(a_ref[...], w_ref[...], preferred_element_type=F32)
    j = pl.program_id(1)

    @pl.when(j < n_rope_tiles)
    def _():
        c, s1, s2 = c_ref[...], s1_ref[...], s2_ref[...]
        half = ROPE_DIMS // 2
        for h in range(acc.shape[1] // HEAD_DIM):
            xh = acc[:, h * HEAD_DIM:(h + 1) * HEAD_DIM]
            o_ref[:, h * HEAD_DIM:(h + 1) * HEAD_DIM] = (
                xh * c + pltpu.roll(xh, HEAD_DIM - half, 1) * s1 + pltpu.roll(xh, half, 1) * s2)

    @pl.when(j >= n_rope_tiles)
    def _():
        o_ref[...] = acc


def matmul(a, w, out_dtype, *, bias=None, residual=None, rope=None, tm=1024, tn=512):
    m, k = a.shape
    n = w.shape[1]
    tm, tn = _tile(m, tm), _tile(n, tn)
    a_spec = pl.BlockSpec((tm, k), lambda i, j: (i, 0))
    w_spec = pl.BlockSpec((k, tn), lambda i, j: (0, j))
    o_spec = pl.BlockSpec((tm, tn), lambda i, j: (i, j))
    if bias is not None:
        kern, extra, extra_specs = _mm_sigmoid_kernel, (bias.reshape(1, n),), [pl.BlockSpec((1, tn), lambda i, j: (0, j))]
    elif residual is not None:
        kern, extra, extra_specs = _mm_residual_kernel, (residual,), [o_spec]
    elif rope is not None:
        tables, seq, n_rope_cols = rope
        per = seq // tm
        t_spec = pl.BlockSpec((tm, HEAD_DIM), lambda i, j: (i % per, 0))
        kern = functools.partial(_mm_rope_kernel, n_rope_tiles=n_rope_cols // tn)
        extra, extra_specs = tuple(tables), [t_spec] * 3
    else:
        kern, extra, extra_specs = _mm_kernel, (), []
    return pl.pallas_call(
        kern,
        out_shape=jax.ShapeDtypeStruct((m, n), out_dtype),
        grid=(m // tm, n // tn),
        in_specs=[a_spec, w_spec] + extra_specs,
        out_specs=o_spec,
        compiler_params=_cp("parallel", "arbitrary"),
        name="matmul",
    )(a, w, *extra)


def rope_tables(seq):
    half = ROPE_DIMS // 2
    inv = ROPE_THETA ** (-jnp.arange(half, dtype=F32) / half)
    ang = jnp.arange(seq, dtype=F32)[:, None] * inv[None, :]
    cos, sin = jnp.cos(ang), jnp.sin(ang)
    zeros = jnp.zeros((seq, HEAD_DIM - ROPE_DIMS), F32)
    zh = jnp.zeros((seq, half), F32)
    c = jnp.concatenate([cos, cos, zeros + 1.0], axis=1)
    s1 = jnp.concatenate([-sin, zh, zeros], axis=1)
    s2 = jnp.concatenate([zh, sin, zeros], axis=1)
    return c, s1, s2


def _swa_kernel(q_ref, k_ref, v_ref, o_ref, acc_ref, m_ref, l_ref, *, seq):
    g = pl.program_id(2)

    @pl.when(g == 0)
    def _():
        acc_ref[...] = jnp.zeros_like(acc_ref)
        m_ref[...] = jnp.full_like(m_ref, NEG_INF)
        l_ref[...] = jnp.zeros_like(l_ref)

    scale = HEAD_DIM ** -0.5
    for gi, (window, dil) in enumerate(SWA_GROUPS):
        radius = window // (2 * dil)
        assert radius <= SWA_BLOCK
        sub = seq // dil
        qb = min(128, sub)
        win = min(sub, qb + 2 * SWA_BLOCK)
        nqb = sub // qb

        @pl.when(g == gi)
        def _(dil=dil, radius=radius, sub=sub, qb=qb, win=win, nqb=nqb):
            def rows(start, size):
                return pl.ds(start, size) if dil == 1 else pl.ds(start, size, stride=dil)

            def body(it, carry):
                res = it // nqb
                qs = (it % nqb) * qb
                ws = jnp.clip(qs - SWA_BLOCK, 0, sub - win)
                q_rows = rows(res + dil * qs, qb)
                k_rows = rows(res + dil * ws, win)
                q = q_ref[q_rows, :].astype(BF16)
                k = k_ref[k_rows, :].astype(BF16)
                v = v_ref[k_rows, :].astype(BF16)
                s = lax.dot_general(q, k, (((1,), (1,)), ((), ())), preferred_element_type=F32) * scale
                qpos = qs + lax.broadcasted_iota(jnp.int32, (qb, win), 0)
                kpos = ws + lax.broadcasted_iota(jnp.int32, (qb, win), 1)
                s = jnp.where(jnp.abs(qpos - kpos) <= radius, s, NEG_INF)
                m_old = m_ref[q_rows, :]
                m_new = jnp.maximum(m_old, jnp.max(s, axis=1, keepdims=True))
                p = jnp.exp(s - m_new)
                corr = jnp.exp(m_old - m_new)
                l_ref[q_rows, :] = l_ref[q_rows, :] * corr + jnp.sum(p, axis=1, keepdims=True)
                acc_ref[q_rows, :] = acc_ref[q_rows, :] * corr + jnp.dot(
                    p.astype(BF16), v, preferred_element_type=F32)
                m_ref[q_rows, :] = m_new
                return carry

            lax.fori_loop(0, dil * nqb, body, 0)

    @pl.when(g == len(SWA_GROUPS) - 1)
    def _():
        o_ref[...] = (acc_ref[...] / l_ref[...]).astype(o_ref.dtype)


def dilated_swa(qkv, batch, seq):
    nh = SWA_HEADS_PER_GROUP
    qkv3 = qkv.reshape(batch, seq, 3 * SWA_WIDTH)

    def spec(off):
        return pl.BlockSpec((None, seq, HEAD_DIM), lambda b, h, g: (b, 0, off + g * nh + h))

    out = pl.pallas_call(
        functools.partial(_swa_kernel, seq=seq),
        out_shape=jax.ShapeDtypeStruct((batch, seq, nh * HEAD_DIM), BF16),
        grid=(batch, nh, len(SWA_GROUPS)),
        in_specs=[spec(0), spec(SWA_HEADS), spec(2 * SWA_HEADS)],
        out_specs=pl.BlockSpec((None, seq, HEAD_DIM), lambda b, h, g: (b, 0, h)),
        scratch_shapes=[pltpu.VMEM((seq, HEAD_DIM), F32), pltpu.VMEM((seq, 1), F32), pltpu.VMEM((seq, 1), F32)],
        compiler_params=_cp("parallel", "parallel", "arbitrary"),
        name="dilated_swa",
    )(qkv3, qkv3, qkv3)
    return out.reshape(batch * seq, nh * HEAD_DIM)


def _gdn_conv_kernel(x_ref, prev_ref, next_ref, w_ref, o_ref, buf_ref, *, ts, tiles_per_seq):
    i, c = pl.program_id(0), pl.program_id(1)
    pad = SUBLANES
    first = (i % tiles_per_seq) == 0
    last = (i % tiles_per_seq) == tiles_per_seq - 1
    buf_ref[0:pad, :] = jnp.where(first, 0.0, prev_ref[...])
    buf_ref[pad:pad + ts, :] = x_ref[...]
    buf_ref[pad + ts:pad + ts + pad, :] = jnp.where(last, 0.0, next_ref[...])
    acc = jnp.zeros((ts, HEAD_DIM), F32)
    for t in range(GDN_CONV):
        acc = acc + buf_ref[pl.ds(pad - GDN_CONV // 2 + t, ts), :] * w_ref[t:t + 1, :]
    y = acc * jax.nn.sigmoid(acc)
    inv = lax.rsqrt(jnp.sum(y * y, axis=-1, keepdims=True) + EPS)
    is_q, is_k = c < GDN_HEADS, jnp.logical_and(c >= GDN_HEADS, c < 2 * GDN_HEADS)
    mul = jnp.where(is_q, inv * (HEAD_DIM ** -0.5), jnp.where(is_k, inv, 1.0))
    o_ref[...] = (y * mul).astype(o_ref.dtype)


def gdn_conv_norm(x, conv_w, seq):
    t, ctot = x.shape
    ts = _tile(seq, 1024)
    nb8 = t // SUBLANES
    per8 = ts // SUBLANES
    return pl.pallas_call(
        functools.partial(_gdn_conv_kernel, ts=ts, tiles_per_seq=seq // ts),
        out_shape=jax.ShapeDtypeStruct((ctot // HEAD_DIM, t, HEAD_DIM), BF16),
        grid=(t // ts, ctot // HEAD_DIM),
        in_specs=[
            pl.BlockSpec((ts, HEAD_DIM), lambda i, c: (i, c)),
            pl.BlockSpec((SUBLANES, HEAD_DIM), lambda i, c: (jnp.maximum(i * per8 - 1, 0), c)),
            pl.BlockSpec((SUBLANES, HEAD_DIM), lambda i, c: (jnp.minimum((i + 1) * per8, nb8 - 1), c)),
            pl.BlockSpec((GDN_CONV, HEAD_DIM), lambda i, c: (0, c)),
        ],
        out_specs=pl.BlockSpec((None, ts, HEAD_DIM), lambda i, c: (c, i, 0)),
        scratch_shapes=[pltpu.VMEM((ts + 2 * SUBLANES, HEAD_DIM), F32)],
        compiler_params=_cp("parallel", "arbitrary"),
        name="gdn_conv_norm",
    )(x, x, x, conv_w)


def _gdn_gate_kernel(ba_ref, alog_ref, dtb_ref, beta_ref, gc_ref, *, ts, heads_per_step):
    x = ba_ref[...]
    nhd = 2 * GDN_HEADS
    lane = lax.broadcasted_iota(jnp.int32, (ts, LANES), 1)
    row = lax.broadcasted_iota(jnp.int32, (ts, LANES), 0) % GDN_CHUNK
    beta = jax.nn.sigmoid(x)
    z = x + dtb_ref[...]
    softplus = jnp.maximum(z, 0.0) + jnp.log(1.0 + jnp.exp(-jnp.abs(z)))
    g = jnp.where(jnp.logical_and(lane >= nhd, lane < 2 * nhd), -jnp.exp(alog_ref[...]) * softplus, 0.0)
    pre, suf = g, g
    s = 1
    while s < GDN_CHUNK:
        pre = pre + jnp.where(row >= s, pltpu.roll(pre, s, 0), 0.0)
        suf = suf + jnp.where(row < GDN_CHUNK - s, pltpu.roll(suf, ts - s, 0), 0.0)
        s *= 2
    gc = jnp.where(lane < nhd + GDN_HEADS, pre, suf)
    for hg in range(GDN_HEADS // heads_per_step):
        sh = (LANES - hg * heads_per_step) % LANES
        beta_ref[hg] = beta if sh == 0 else pltpu.roll(beta, sh, 1)
        gc_ref[hg] = gc if sh == 0 else pltpu.roll(gc, sh, 1)


def gdn_gates(ba, a_log, dt_bias, heads_per_step):
    t = ba.shape[0]
    ts = _tile(t, 1024)
    nhd = 2 * GDN_HEADS
    pad = lambda v: jnp.zeros((1, LANES), F32).at[0, nhd:2 * nhd].set(v.reshape(-1).astype(F32))
    ng = GDN_HEADS // heads_per_step
    shp = jax.ShapeDtypeStruct((ng, t, LANES), F32)
    vec = pl.BlockSpec((1, LANES), lambda i: (0, 0))
    return pl.pallas_call(
        functools.partial(_gdn_gate_kernel, ts=ts, heads_per_step=heads_per_step),
        out_shape=(shp, shp),
        grid=(t // ts,),
        in_specs=[pl.BlockSpec((ts, LANES), lambda i: (i, 0)), vec, vec],
        out_specs=(pl.BlockSpec((ng, ts, LANES), lambda i: (0, i, 0)),) * 2,
        compiler_params=_cp("parallel"),
        name="gdn_gates",
    )(ba, pad(a_log), pad(dt_bias))


def _nt(a, b):
    return lax.dot_general(a, b, (((1,), (1,)), ((), ())), preferred_element_type=F32)


def _tn(a, b):
    return lax.dot_general(a, b, (((0,), (0,)), ((), ())), preferred_element_type=F32)


def _mmb(a, b):
    return jnp.dot(a.astype(BF16), b.astype(BF16), preferred_element_type=F32)


def _gdn_chunk(q, k, v, bcol, gcol, state, *, reverse):
    c = GDN_CHUNK
    ri = lax.broadcasted_iota(jnp.int32, (c, c), 0)
    ci = lax.broadcasted_iota(jnp.int32, (c, c), 1)
    strict = (ri < ci) if reverse else (ri > ci)
    incl = (ri <= ci) if reverse else (ri >= ci)
    g_last = gcol[0:1, :] if reverse else gcol[c - 1:c, :]
    egc = jnp.exp(gcol)
    g1 = gcol.astype(BF16).astype(F32)
    g2 = (gcol - g1).astype(BF16).astype(F32)
    g3 = (gcol - g1 - g2).astype(BF16).astype(F32)
    lane = lax.broadcasted_iota(jnp.int32, (c, LANES), 1)
    pieces = jnp.where(lane == 0, g1, jnp.where(lane == 1, g2, jnp.where(lane == 2, g3, 0.0)))
    ones3 = jnp.where(lane < 3, 1.0, 0.0)
    xm = pieces + pltpu.roll(ones3, 3, 1)
    ym = ones3 - pltpu.roll(pieces, 3, 1)
    diff = _nt(xm.astype(BF16), ym.astype(BF16))
    decay = jnp.where(incl, jnp.exp(jnp.where(incl, diff, 0.0)), 0.0)
    kb = k * bcol
    k16 = k.astype(BF16)
    lm = jnp.where(strict, _nt(kb.astype(BF16), k16) * decay, 0.0)
    tinv = jnp.where(ri == ci, 1.0, 0.0) - lm
    pw = lm
    n = 2
    while n < c:
        pw = _mmb(pw, pw)
        tinv = tinv + _mmb(tinv, pw)
        n *= 2
    u = _mmb(tinv, jnp.concatenate([v * bcol, kb * egc], axis=1))
    value, kcd = u[:, :HEAD_DIM], u[:, HEAD_DIM:]
    attn = _nt(q.astype(BF16), k16) * decay
    v_new = value - _mmb(kcd, state)
    o = _mmb(q * egc, state) + _mmb(attn, v_new)
    k_e = k * jnp.exp(g_last - gcol)
    state = state * jnp.exp(g_last) + _tn(k_e.astype(BF16), v_new.astype(BF16))
    return o, state


def _gdn_kernel(q_ref, k_ref, v_ref, beta_ref, gc_ref, o_ref, state_ref, *, reverse, nh, chunks):
    @pl.when(pl.program_id(2) == 0)
    def _():
        state_ref[...] = jnp.zeros_like(state_ref)

    lane0 = GDN_HEADS if reverse else 0

    def body(it, carry):
        ci = (chunks - 1 - it) if reverse else it
        r0 = pl.multiple_of(ci * GDN_CHUNK, GDN_CHUNK)
        rows = pl.ds(r0, GDN_CHUNK)
        bt = beta_ref[rows, :]
        gt = gc_ref[rows, :]
        for h in range(nh):
            bcol = bt[:, lane0 + h:lane0 + h + 1]
            gcol = gt[:, 2 * GDN_HEADS + lane0 + h:2 * GDN_HEADS + lane0 + h + 1]
            o, st = _gdn_chunk(q_ref[h, rows, :].astype(F32), k_ref[h, rows, :].astype(F32),
                               v_ref[h, rows, :].astype(F32), bcol, gcol, state_ref[h], reverse=reverse)
            state_ref[h] = st
            o_ref[h, rows, :] = o.astype(o_ref.dtype)
        return carry

    lax.fori_loop(0, chunks, body, 0)


def gdn_scan(qkv_h, beta_r, gc_r, batch, seq, *, reverse, nh):
    t = batch * seq
    rows = _tile(seq, 512)
    nblk = seq // rows
    ng = GDN_HEADS // nh

    def blk(b, ib):
        return b * nblk + ((nblk - 1 - ib) if reverse else ib)

    def qspec(off):
        return pl.BlockSpec((nh, rows, HEAD_DIM), lambda b, hg, ib: (off * ng + hg, blk(b, ib), 0))

    gspec = pl.BlockSpec((None, rows, LANES), lambda b, hg, ib: (hg, blk(b, ib), 0))
    return pl.pallas_call(
        functools.partial(_gdn_kernel, reverse=reverse, nh=nh, chunks=rows // GDN_CHUNK),
        out_shape=jax.ShapeDtypeStruct((GDN_HEADS, t, HEAD_DIM), BF16),
        grid=(batch, ng, nblk),
        in_specs=[qspec(0), qspec(1), qspec(2), gspec, gspec],
        out_specs=pl.BlockSpec((nh, rows, HEAD_DIM), lambda b, hg, ib: (hg, blk(b, ib), 0)),
        scratch_shapes=[pltpu.VMEM((nh, HEAD_DIM, HEAD_DIM), F32)],
        compiler_params=_cp("parallel", "parallel", "arbitrary"),
        name="gdn_scan_bwd" if reverse else "gdn_scan_fwd",
    )(qkv_h, qkv_h, qkv_h, beta_r, gc_r)


def _gdn_post_kernel(of_ref, ob_ref, z_ref, g_ref, o_ref):
    o = of_ref[...].astype(F32) + ob_ref[...].astype(F32)
    r = lax.rsqrt(jnp.mean(o * o, axis=-1, keepdims=True) + EPS)
    z = z_ref[...].astype(F32)
    o_ref[...] = (o * r * g_ref[...] * (z * jax.nn.sigmoid(z))).astype(o_ref.dtype)


def gdn_post(o_f, o_b, z, norm_g):
    nhd, t, _ = o_f.shape
    ts = _tile(t, 1024)
    hspec = pl.BlockSpec((None, ts, HEAD_DIM), lambda i, h: (h, i, 0))
    tspec = pl.BlockSpec((ts, HEAD_DIM), lambda i, h: (i, h))
    return pl.pallas_call(
        _gdn_post_kernel,
        out_shape=jax.ShapeDtypeStruct((t, nhd * HEAD_DIM), BF16),
        grid=(t // ts, nhd),
        in_specs=[hspec, hspec, tspec, pl.BlockSpec((1, HEAD_DIM), lambda i, h: (0, 0))],
        out_specs=tspec,
        compiler_params=_cp("parallel", "arbitrary"),
        name="gdn_post",
    )(o_f, o_b, z, norm_g.reshape(1, HEAD_DIM).astype(F32))


def _mem_attn_kernel(q_ref, k_ref, v_ref, o_ref):
    scale = MEM_HEAD_DIM ** -0.5
    for h in range(MEM_HEADS):
        cols = slice(h * MEM_HEAD_DIM, (h + 1) * MEM_HEAD_DIM)
        s = _nt(q_ref[:, cols], k_ref[:, cols]) * scale
        p = jnp.exp(s - jnp.max(s, axis=-1, keepdims=True))
        p = p / jnp.sum(p, axis=-1, keepdims=True)
        o_ref[:, cols] = jnp.dot(p.astype(BF16), v_ref[:, cols], preferred_element_type=F32).astype(o_ref.dtype)


def mem_attention(mq, kv, batch, seq, mem_tokens):
    tq = _tile(seq, 512)
    per = seq // tq
    return pl.pallas_call(
        _mem_attn_kernel,
        out_shape=jax.ShapeDtypeStruct(mq.shape, BF16),
        grid=(batch * per,),
        in_specs=[
            pl.BlockSpec((tq, MEM_WIDTH), lambda i: (i, 0)),
            pl.BlockSpec((mem_tokens, MEM_WIDTH), lambda i: (i // per, 0)),
            pl.BlockSpec((mem_tokens, MEM_WIDTH), lambda i: (i // per, 1)),
        ],
        out_specs=pl.BlockSpec((tq, MEM_WIDTH), lambda i: (i, 0)),
        compiler_params=_cp("parallel"),
        name="mem_attention",
    )(mq, kv, kv)


def _merge_kernel(oa_ref, ob_ref, om_ref, wa_ref, wb_ref, wm_ref, g0_ref, g1_ref, g2_ref, o_ref):
    ya = jnp.dot(oa_ref[...], wa_ref[...], preferred_element_type=F32)
    yb = jnp.dot(ob_ref[...], wb_ref[...], preferred_element_type=F32)
    ym = jnp.dot(om_ref[...], wm_ref[...], preferred_element_type=F32)
    mix = g0_ref[...].astype(F32) * ya + g1_ref[...].astype(F32) * yb + g2_ref[...].astype(F32) * ym
    o_ref[...] = mix.astype(o_ref.dtype)


def gated_merge(o_a, o_b, o_m, w_a, w_b, w_m, gates):
    t = o_a.shape[0]
    d = w_a.shape[1]
    tm, tn = _tile(t, 1024), _tile(d, 512)
    nj = d // tn
    act = lambda o: pl.BlockSpec((tm, o.shape[1]), lambda i, j: (i, 0))
    wsp = lambda w: pl.BlockSpec((w.shape[0], tn), lambda i, j: (0, j))
    gsp = lambda br: pl.BlockSpec((tm, tn), lambda i, j: (i, br * nj + j))
    return pl.pallas_call(
        _merge_kernel,
        out_shape=jax.ShapeDtypeStruct((t, d), BF16),
        grid=(t // tm, nj),
        in_specs=[act(o_a), act(o_b), act(o_m), wsp(w_a), wsp(w_b), wsp(w_m), gsp(0), gsp(1), gsp(2)],
        out_specs=pl.BlockSpec((tm, tn), lambda i, j: (i, j)),
        compiler_params=_cp("parallel", "arbitrary"),
        name="gated_merge",
    )(o_a, o_b, o_m, w_a, w_b, w_m, gates, gates, gates)


ROUTE_ROWS = SUBLANES + N_EXPERTS


def _route_kernel(x_ref, g_ref, w_ref, b_ref, xn_ref, eid_ref, wt_ref):
    x = x_ref[...]
    r = lax.rsqrt(jnp.mean(x * x, axis=-1, keepdims=True) + EPS)
    xn = x * r * g_ref[...]
    xn_ref[...] = xn
    tm = x.shape[0]
    lg = lax.dot_general(w_ref[...], xn, (((1,), (1,)), ((), ())), preferred_element_type=F32,
                         precision=lax.Precision.HIGHEST) + b_ref[...]
    gl = [lg[i:i + 1, :] for i in range(N_GROUPS)]
    gmax = functools.reduce(jnp.maximum, gl)
    grp = jnp.full((1, tm), N_GROUPS - 1, jnp.int32)
    for i in range(N_GROUPS - 2, -1, -1):
        grp = jnp.where(gl[i] == gmax, i, grp)
    p_grp = 1.0 / functools.reduce(jnp.add, [jnp.exp(v - gmax) for v in gl])
    sel = lg[SUBLANES + (N_GROUPS - 1) * EXPERTS_PER_GROUP:SUBLANES + N_GROUPS * EXPERTS_PER_GROUP, :]
    for i in range(N_GROUPS - 2, -1, -1):
        lo = SUBLANES + i * EXPERTS_PER_GROUP
        sel = jnp.where(grp == i, lg[lo:lo + EXPERTS_PER_GROUP, :], sel)
    rowi = lax.broadcasted_iota(jnp.int32, (EXPERTS_PER_GROUP, tm), 0)
    v1 = jnp.max(sel, axis=0, keepdims=True)
    i1 = jnp.min(jnp.where(sel == v1, rowi, EXPERTS_PER_GROUP), axis=0, keepdims=True)
    sel2 = jnp.where(rowi == i1, -jnp.inf, sel)
    v2 = jnp.max(sel2, axis=0, keepdims=True)
    i2 = jnp.min(jnp.where(sel2 == v2, rowi, EXPERTS_PER_GROUP), axis=0, keepdims=True)
    e2 = jnp.exp(v2 - v1)
    w1 = p_grp / (1.0 + e2)
    eid_ref[0:1, :] = grp * EXPERTS_PER_GROUP + i1
    eid_ref[1:2, :] = grp * EXPERTS_PER_GROUP + i2
    wt_ref[0:1, :] = w1
    wt_ref[1:2, :] = w1 * e2


def moe_route(x, g, w_grp, b_grp, w_exp):
    t, d = x.shape
    tm = _tile(t, 512)
    w = jnp.zeros((ROUTE_ROWS, d), F32).at[:N_GROUPS].set(w_grp.T).at[SUBLANES:].set(w_exp.T)
    b = jnp.zeros((ROUTE_ROWS, 1), F32).at[:N_GROUPS, 0].set(b_grp.astype(F32))
    return pl.pallas_call(
        _route_kernel,
        out_shape=(jax.ShapeDtypeStruct((t, d), F32), jax.ShapeDtypeStruct((EXPERT_TOPK, t), jnp.int32),
                   jax.ShapeDtypeStruct((EXPERT_TOPK, t), F32)),
        grid=(t // tm,),
        in_specs=[pl.BlockSpec((tm, d), lambda i: (i, 0)), pl.BlockSpec((1, d), lambda i: (0, 0)),
                  pl.BlockSpec((ROUTE_ROWS, d), lambda i: (0, 0)), pl.BlockSpec((ROUTE_ROWS, 1), lambda i: (0, 0))],
        out_specs=(pl.BlockSpec((tm, d), lambda i: (i, 0)), pl.BlockSpec((EXPERT_TOPK, tm), lambda i: (0, i)),
                   pl.BlockSpec((EXPERT_TOPK, tm), lambda i: (0, i))),
        compiler_params=_cp("parallel"),
        name="moe_route",
    )(x, g.reshape(1, d), w, b)


def _rank_kernel(e_ref, rank_ref, cnt_ref, *, tb):
    @pl.when(pl.program_id(0) == 0)
    def _():
        cnt_ref[...] = jnp.zeros_like(cnt_ref)

    e = e_ref[...]
    onehot = jnp.where(lax.broadcasted_iota(jnp.int32, (N_EXPERTS, tb), 0) == e, 1.0, 0.0)
    earlier = jnp.where(lax.broadcasted_iota(jnp.int32, (tb, tb), 0) < lax.broadcasted_iota(jnp.int32, (tb, tb), 1),
                        1.0, 0.0)
    before = _mmb(onehot, earlier) + cnt_ref[...]
    rank_ref[...] = jnp.sum(onehot * before, axis=0, keepdims=True).astype(jnp.int32)
    cnt_ref[...] = cnt_ref[...] + jnp.sum(onehot, axis=1, keepdims=True)


def moe_rank(e_flat):
    n = e_flat.shape[1]
    tb = _tile(n, 512)
    return pl.pallas_call(
        functools.partial(_rank_kernel, tb=tb),
        out_shape=(jax.ShapeDtypeStruct((1, n), jnp.int32), jax.ShapeDtypeStruct((N_EXPERTS, 1), F32)),
        grid=(n // tb,),
        in_specs=[pl.BlockSpec((1, tb), lambda i: (0, i))],
        out_specs=(pl.BlockSpec((1, tb), lambda i: (0, i)), pl.BlockSpec((N_EXPERTS, 1), lambda i: (0, 0))),
        compiler_params=_cp("arbitrary"),
        name="moe_rank",
    )(e_flat)


def _dispatch_kernel(dest_ref, cnt_ref, pstart_ref, xn_ref, zero_ref, xs_ref, sem, *, tb, tokens, nblk):
    i = pl.program_id(0)

    def row_copy(src_ref, src_row, dst_row):
        return pltpu.make_async_copy(src_ref.at[pl.ds(src_row, 1)], xs_ref.at[pl.ds(dst_row, 1)], sem)

    def block_copy(blk):
        return pltpu.make_async_copy(zero_ref, xs_ref.at[pl.ds(blk * MOE_BLOCK, MOE_BLOCK)], sem)

    @pl.when(i == 0)
    def _():
        last = N_EXPERTS - 1
        n_used = (pstart_ref[last] + cnt_ref[last] + MOE_BLOCK - 1) // MOE_BLOCK

        def fill_block(blk, carry):
            block_copy(blk).start()
            block_copy(blk).wait()
            return carry

        lax.fori_loop(n_used, nblk, fill_block, 0)

        def per_expert(e, carry):
            npad = (-cnt_ref[e]) % MOE_BLOCK
            base = pstart_ref[e] + cnt_ref[e]

            def fill(r, c2):
                row_copy(zero_ref, 0, base + r).start()
                return c2

            lax.fori_loop(0, npad, fill, 0)

            def drain(r, c2):
                row_copy(zero_ref, 0, base + r).wait()
                return c2

            lax.fori_loop(0, npad, drain, 0)
            return carry

        lax.fori_loop(0, N_EXPERTS, per_expert, 0)

    def issue(j, carry):
        tok = i * tb + j
        for kk in range(EXPERT_TOPK):
            row_copy(xn_ref, tok, dest_ref[kk * tokens + tok]).start()
        return carry

    lax.fori_loop(0, tb, issue, 0)

    def drain(j, carry):
        tok = i * tb + j
        for kk in range(EXPERT_TOPK):
            row_copy(xn_ref, tok, dest_ref[kk * tokens + tok]).wait()
        return carry

    lax.fori_loop(0, tb, drain, 0)


def moe_dispatch(xn, dest_flat, counts, pad_start, n_slots):
    t, d = xn.shape
    tb = _tile(t, 1024)
    return pl.pallas_call(
        functools.partial(_dispatch_kernel, tb=tb, tokens=t, nblk=n_slots // MOE_BLOCK),
        out_shape=jax.ShapeDtypeStruct((n_slots, d), xn.dtype),
        grid_spec=pltpu.PrefetchScalarGridSpec(
            num_scalar_prefetch=3,
            grid=(t // tb,),
            in_specs=[pl.BlockSpec(memory_space=pl.ANY), pl.BlockSpec(memory_space=pl.ANY)],
            out_specs=pl.BlockSpec(memory_space=pl.ANY),
            scratch_shapes=[pltpu.SemaphoreType.DMA],
        ),
        compiler_params=pltpu.CompilerParams(dimension_semantics=("arbitrary",), has_side_effects=True),
        name="moe_dispatch",
    )(dest_flat, counts, pad_start, xn, jnp.zeros((MOE_BLOCK, d), xn.dtype))


def _expert_kernel(be_ref, nused_ref, x_ref, wg_ref, wu_ref, wd_ref, y_ref):
    @pl.when(pl.program_id(0) < nused_ref[0])
    def _():
        x = x_ref[...].astype(BF16)
        hg = jnp.dot(x, wg_ref[...], preferred_element_type=F32)
        hu = jnp.dot(x, wu_ref[...], preferred_element_type=F32)
        hmid = (hg * jax.nn.sigmoid(hg) * hu).astype(BF16)
        y_ref[...] = jnp.dot(hmid, wd_ref[...], preferred_element_type=F32).astype(y_ref.dtype)

    @pl.when(pl.program_id(0) >= nused_ref[0])
    def _():
        y_ref[...] = jnp.zeros_like(y_ref)


def moe_experts(xs, blk_expert, n_used, w_gate, w_up, w_down):
    p, d = xs.shape
    de = w_gate.shape[2]
    nblk = p // MOE_BLOCK
    row = lambda i, be, nu: (jnp.minimum(i, nu[0] - 1), 0)
    wsel = lambda i, be, nu: (be[jnp.minimum(i, nu[0] - 1)], 0, 0)
    return pl.pallas_call(
        _expert_kernel,
        out_shape=jax.ShapeDtypeStruct((p, d), F32),
        grid_spec=pltpu.PrefetchScalarGridSpec(
            num_scalar_prefetch=2,
            grid=(nblk,),
            in_specs=[pl.BlockSpec((MOE_BLOCK, d), row), pl.BlockSpec((None, d, de), wsel),
                      pl.BlockSpec((None, d, de), wsel), pl.BlockSpec((None, de, d), wsel)],
            out_specs=pl.BlockSpec((MOE_BLOCK, d), lambda i, be, nu: (i, 0)),
        ),
        compiler_params=_cp("arbitrary"),
        name="moe_experts",
    )(blk_expert, n_used, xs, w_gate, w_up, w_down)


def _combine_kernel(dest_ref, x_ref, wt_ref, g_ref, ys_ref, o_ref, y0_ref, y1_ref, sem, *, tb, tokens):
    i = pl.program_id(0)
    bufs = (y0_ref, y1_ref)

    def row_copy(kk, j):
        return pltpu.make_async_copy(ys_ref.at[pl.ds(dest_ref[kk * tokens + i * tb + j], 1)],
                                     bufs[kk].at[pl.ds(j, 1)], sem)

    def issue(j, carry):
        for kk in range(EXPERT_TOPK):
            row_copy(kk, j).start()
        return carry

    lax.fori_loop(0, tb, issue, 0)

    def drain(j, carry):
        for kk in range(EXPERT_TOPK):
            row_copy(kk, j).wait()
        return carry

    lax.fori_loop(0, tb, drain, 0)
    h = x_ref[...] + wt_ref[:, 0:1] * y0_ref[...] + wt_ref[:, 1:2] * y1_ref[...]
    r = lax.rsqrt(jnp.mean(h * h, axis=-1, keepdims=True) + EPS)
    o_ref[...] = h * r * g_ref[...]


def moe_combine(x, ys, dest_flat, wts_tok, g_final):
    t, d = x.shape
    tb = _tile(t, 256)
    return pl.pallas_call(
        functools.partial(_combine_kernel, tb=tb, tokens=t),
        out_shape=jax.ShapeDtypeStruct((t, d), F32),
        grid_spec=pltpu.PrefetchScalarGridSpec(
            num_scalar_prefetch=1,
            grid=(t // tb,),
            in_specs=[pl.BlockSpec((tb, d), lambda i, dr: (i, 0)),
                      pl.BlockSpec((tb, EXPERT_TOPK), lambda i, dr: (i, 0)),
                      pl.BlockSpec((1, d), lambda i, dr: (0, 0)),
                      pl.BlockSpec(memory_space=pl.ANY)],
            out_specs=pl.BlockSpec((tb, d), lambda i, dr: (i, 0)),
            scratch_shapes=[pltpu.VMEM((tb, d), F32), pltpu.VMEM((tb, d), F32), pltpu.SemaphoreType.DMA],
        ),
        compiler_params=_cp("arbitrary"),
        name="moe_combine",
    )(dest_flat, x, wts_tok, g_final.reshape(1, d), ys)


def hier_moe_final(x2, g_ffn, w_rg, b_rg, w_re, w_gate, w_up, w_down, g_final):
    t, d = x2.shape
    xn, eid, wts = moe_route(x2, g_ffn, w_rg, b_rg, w_re)
    n = EXPERT_TOPK * t
    e_flat = eid.reshape(1, n)
    rank, counts_f = moe_rank(e_flat)
    counts = counts_f.reshape(N_EXPERTS).astype(jnp.int32)
    padded = (counts + MOE_BLOCK - 1) // MOE_BLOCK * MOE_BLOCK
    pad_end = jnp.cumsum(padded)
    pad_start = pad_end - padded
    dest = (jnp.take(pad_start, e_flat[0]) + rank[0]).astype(jnp.int32)
    nblk = -(-n // MOE_BLOCK) + N_EXPERTS
    blk_expert = jnp.minimum(jnp.searchsorted(pad_end, jnp.arange(nblk) * MOE_BLOCK, side="right"),
                             N_EXPERTS - 1).astype(jnp.int32)
    n_used = (pad_end[-1:] // MOE_BLOCK).astype(jnp.int32)
    xs = moe_dispatch(xn, dest, counts, pad_start.astype(jnp.int32), nblk * MOE_BLOCK)
    ys = moe_experts(xs, blk_expert, n_used, w_gate.astype(BF16), w_up.astype(BF16), w_down.astype(BF16))
    return moe_combine(x2, ys, dest, wts.T, g_final)


GDN_HEADS_PER_STEP = 4


def kernel(x, mem, g_mix, w_in, b_gate, gdn_conv, gdn_a_log, gdn_dt_bias, gdn_norm_g, g_mem, w_mem_kv, w_o_swa, w_o_gdn, w_o_mem, w_out, g_ffn, w_route_group, b_route_group, w_route_expert, w_expert_gate, w_expert_up, w_expert_down, g_final):
    batch, seq, d = x.shape
    depth = w_in.shape[0]
    assert depth == 1, "the final RMSNorm is fused into the layer's MoE combine"
    t = batch * seq
    mem_tokens = mem.shape[1]
    h = x.reshape(t, d)
    o0 = 3 * SWA_WIDTH
    o1 = o0 + 3 * GDN_WIDTH
    o2 = o1 + GDN_WIDTH
    o3 = o2 + 4 * GDN_HEADS
    o4 = o3 + MEM_WIDTH
    tables = rope_tables(seq)
    for l in range(depth):
        w = w_in[l].astype(BF16)
        a = rmsnorm_rows(h, g_mix[l], BF16)
        qkv_a = matmul(a, w[:, :o0], F32, rope=(tables, seq, 2 * SWA_WIDTH))
        o_a = dilated_swa(qkv_a, batch, seq)
        qkv_b = gdn_conv_norm(matmul(a, w[:, o0:o1], F32), gdn_conv[l], seq)
        z_b = matmul(a, w[:, o1:o2], BF16)
        w_ba = jnp.zeros((d, LANES), BF16).at[:, :o3 - o2].set(w[:, o2:o3])
        beta_r, gc_r = gdn_gates(matmul(a, w_ba, F32), gdn_a_log[l], gdn_dt_bias[l], GDN_HEADS_PER_STEP)
        o_f = gdn_scan(qkv_b, beta_r, gc_r, batch, seq, reverse=False, nh=GDN_HEADS_PER_STEP)
        o_r = gdn_scan(qkv_b, beta_r, gc_r, batch, seq, reverse=True, nh=GDN_HEADS_PER_STEP)
        o_b = gdn_post(o_f, o_r, z_b, gdn_norm_g[l])
        kv = matmul(rmsnorm_rows(mem.reshape(batch * mem_tokens, d), g_mem[l], BF16), w_mem_kv[l].astype(BF16), BF16)
        o_m = mem_attention(matmul(a, w[:, o3:o4], BF16), kv, batch, seq, mem_tokens)
        gates = matmul(a, w[:, o4:], BF16, bias=b_gate[l].reshape(-1))
        mix = gated_merge(o_a, o_b, o_m, w_o_swa[l].astype(BF16), w_o_gdn[l].astype(BF16),
                          w_o_mem[l].astype(BF16), gates)
        x2 = matmul(mix, w_out[l].astype(BF16), F32, residual=h)
        h = hier_moe_final(x2, g_ffn[l], w_route_group[l], b_route_group[l], w_route_expert[l],
                           w_expert_gate[l], w_expert_up[l], w_expert_down[l], g_final)
    return h.reshape(batch, seq, d)
```

```python
import functools

import jax
import jax.numpy as jnp
from jax import lax
from jax.experimental import pallas as pl
from jax.experimental.pallas import tpu as pltpu

F32 = jnp.float32
BF16 = jnp.bfloat16

HEAD_DIM = 128
SWA_GROUPS = ((128, 1), (512, 4), (2048, 16))
SWA_HEADS_PER_GROUP = 4
SWA_HEADS = SWA_HEADS_PER_GROUP * len(SWA_GROUPS)
SWA_WIDTH = SWA_HEADS * HEAD_DIM
SWA_BLOCK = 64
SWA_UNROLL = 4
ROPE_THETA = 500000.0
ROPE_DIMS = HEAD_DIM // 4
GDN_HEADS = 12
GDN_WIDTH = GDN_HEADS * HEAD_DIM
GDN_CONV = 5
GDN_CHUNK = 64
MEM_HEADS = 4
MEM_HEAD_DIM = 256
MEM_WIDTH = MEM_HEADS * MEM_HEAD_DIM
N_BRANCH = 3
N_GROUPS = 4
EXPERTS_PER_GROUP = 8
N_EXPERTS = N_GROUPS * EXPERTS_PER_GROUP
EXPERT_TOPK = 2
MOE_BLOCK = 256
EPS = 1e-6
NEG_INF = -1e30

LANES = 128
SUBLANES = 8
VMEM_LIMIT = 48 * 1024 * 1024


def _cp(*sem, vmem=VMEM_LIMIT):
    return pltpu.CompilerParams(dimension_semantics=sem, vmem_limit_bytes=vmem)


def _tile(n, pref):
    t = min(n, pref)
    while n % t:
        t //= 2
    return t


def _rmsnorm_kernel(x_ref, g_ref, o_ref):
    x = x_ref[...].astype(F32)
    r = lax.rsqrt(jnp.mean(x * x, axis=-1, keepdims=True) + EPS)
    o_ref[...] = (x * r * g_ref[...]).astype(o_ref.dtype)


def rmsnorm_rows(x, g, out_dtype):
    m, d = x.shape
    tm = _tile(m, 512)
    return pl.pallas_call(
        _rmsnorm_kernel,
        out_shape=jax.ShapeDtypeStruct((m, d), out_dtype),
        grid=(m // tm,),
        in_specs=[pl.BlockSpec((tm, d), lambda i: (i, 0)), pl.BlockSpec((1, d), lambda i: (0, 0))],
        out_specs=pl.BlockSpec((tm, d), lambda i: (i, 0)),
        compiler_params=_cp("parallel"),
        name="rmsnorm_rows",
    )(x, g.reshape(1, d))


def _mm_kernel(a_ref, w_ref, o_ref):
    o_ref[...] = jnp.dot(a_ref[...], w_ref[...], preferred_element_type=F32).astype(o_ref.dtype)


def _mm_sigmoid_kernel(a_ref, w_ref, b_ref, o_ref):
    z = jnp.dot(a_ref[...], w_ref[...], preferred_element_type=F32) + b_ref[...]
    o_ref[...] = jax.nn.sigmoid(z).astype(o_ref.dtype)


def _mm_residual_kernel(a_ref, w_ref, r_ref, o_ref):
    o_ref[...] = r_ref[...] + jnp.dot(a_ref[...], w_ref[...], preferred_element_type=F32)


def _mm_rope_kernel(a_ref, w_ref, c_ref, s1_ref, s2_ref, o_ref, *, n_rope_tiles):
    acc = jnp.dot(a_ref[...], w_ref[...], preferred_element_type=F32)
    j = pl.program_id(1)

    @pl.when(j < n_rope_tiles)
    def _():
        c, s1, s2 = c_ref[...], s1_ref[...], s2_ref[...]
        half = ROPE_DIMS // 2
        for h in range(acc.shape[1] // HEAD_DIM):
            xh = acc[:, h * HEAD_DIM:(h + 1) * HEAD_DIM]
            o_ref[:, h * HEAD_DIM:(h + 1) * HEAD_DIM] = (
                xh * c + pltpu.roll(xh, HEAD_DIM - half, 1) * s1 + pltpu.roll(xh, half, 1) * s2)

    @pl.when(j >= n_rope_tiles)
    def _():
        o_ref[...] = acc


def matmul(a, w, out_dtype, *, bias=None, residual=None, rope=None, tm=1024, tn=512):
    m, k = a.shape
    n = w.shape[1]
    tm, tn = _tile(m, tm), _tile(n, tn)
    a_spec = pl.BlockSpec((tm, k), lambda i, j: (i, 0))
    w_spec = pl.BlockSpec((k, tn), lambda i, j: (0, j))
    o_spec = pl.BlockSpec((tm, tn), lambda i, j: (i, j))
    if bias is not None:
        kern, extra, extra_specs = _mm_sigmoid_kernel, (bias.reshape(1, n),), [pl.BlockSpec((1, tn), lambda i, j: (0, j))]
    elif residual is not None:
        kern, extra, extra_specs = _mm_residual_kernel, (residual,), [o_spec]
    elif rope is not None:
        tables, seq, n_rope_cols = rope
        per = seq // tm
        t_spec = pl.BlockSpec((tm, HEAD_DIM), lambda i, j: (i % per, 0))
        kern = functools.partial(_mm_rope_kernel, n_rope_tiles=n_rope_cols // tn)
        extra, extra_specs = tuple(tables), [t_spec] * 3
    else:
        kern, extra, extra_specs = _mm_kernel, (), []
    return pl.pallas_call(
        kern,
        out_shape=jax.ShapeDtypeStruct((m, n), out_dtype),
        grid=(m // tm, n // tn),
        in_specs=[a_spec, w_spec] + extra_specs,
        out_specs=o_spec,
        compiler_params=_cp("parallel", "arbitrary"),
        name="matmul",
    )(a, w, *extra)


def rope_tables(seq):
    half = ROPE_DIMS // 2
    inv = ROPE_THETA ** (-jnp.arange(half, dtype=F32) / half)
    ang = jnp.arange(seq, dtype=F32)[:, None] * inv[None, :]
    cos, sin = jnp.cos(ang), jnp.sin(ang)
    zeros = jnp.zeros((seq, HEAD_DIM - ROPE_DIMS), F32)
    zh = jnp.zeros((seq, half), F32)
    c = jnp.concatenate([cos, cos, zeros + 1.0], axis=1)
    s1 = jnp.concatenate([-sin, zh, zeros], axis=1)
    s2 = jnp.concatenate([zh, sin, zeros], axis=1)
    return c, s1, s2


def _swa_kernel(q_ref, k_ref, v_ref, o_ref, acc_ref, m_ref, l_ref, *, seq):
    g = pl.program_id(2)

    @pl.when(g == 0)
    def _():
        acc_ref[...] = jnp.zeros_like(acc_ref)
        m_ref[...] = jnp.full_like(m_ref, NEG_INF)
        l_ref[...] = jnp.zeros_like(l_ref)

    scale = HEAD_DIM ** -0.5
    for gi, (window, dil) in enumerate(SWA_GROUPS):
        radius = window // (2 * dil)
        assert radius <= SWA_BLOCK
        sub = seq // dil
        qb = min(128, sub)
        win = min(sub, qb + 2 * SWA_BLOCK)
        nqb = sub // qb

        @pl.when(g == gi)
        def _(dil=dil, radius=radius, sub=sub, qb=qb, win=win, nqb=nqb):
            def rows(start, size):
                return pl.ds(start, size) if dil == 1 else pl.ds(start, size, stride=dil)

            total = dil * nqb
            unroll = SWA_UNROLL if total % SWA_UNROLL == 0 else 1
            idx = range(unroll)

            def body(step, carry):
                its = [step * unroll + u for u in idx]
                qss = [(it % nqb) * qb for it in its]
                wss = [jnp.clip(qs - SWA_BLOCK, 0, sub - win) for qs in qss]
                q_rows = [rows(it // nqb + dil * qs, qb) for it, qs in zip(its, qss)]
                k_rows = [rows(it // nqb + dil * ws, win) for it, ws in zip(its, wss)]
                q = [q_ref[r, :].astype(BF16) for r in q_rows]
                k = [k_ref[r, :].astype(BF16) for r in k_rows]
                v = [v_ref[r, :].astype(BF16) for r in k_rows]
                s = [_nt(q[u], k[u]) * scale for u in idx]
                off = lax.broadcasted_iota(jnp.int32, (qb, win), 0) - lax.broadcasted_iota(jnp.int32, (qb, win), 1)
                s = [jnp.where(jnp.abs(off + (qss[u] - wss[u])) <= radius, s[u], NEG_INF) for u in idx]
                m_old = [m_ref[r, :] for r in q_rows]
                m_new = [jnp.maximum(m_old[u], jnp.max(s[u], axis=1, keepdims=True)) for u in idx]
                p = [jnp.exp(s[u] - m_new[u]) for u in idx]
                corr = [jnp.exp(m_old[u] - m_new[u]) for u in idx]
                pv = [jnp.dot(p[u].astype(BF16), v[u], preferred_element_type=F32) for u in idx]
                for u in idx:
                    l_ref[q_rows[u], :] = l_ref[q_rows[u], :] * corr[u] + jnp.sum(p[u], axis=1, keepdims=True)
                    acc_ref[q_rows[u], :] = acc_ref[q_rows[u], :] * corr[u] + pv[u]
                    m_ref[q_rows[u], :] = m_new[u]
                return carry

            lax.fori_loop(0, total // unroll, body, 0)

    @pl.when(g == len(SWA_GROUPS) - 1)
    def _():
        o_ref[...] = (acc_ref[...] / l_ref[...]).astype(o_ref.dtype)


def dilated_swa(qkv, batch, seq):
    nh = SWA_HEADS_PER_GROUP
    qkv3 = qkv.reshape(batch, seq, 3 * SWA_WIDTH)

    def spec(off):
        return pl.BlockSpec((None, seq, HEAD_DIM), lambda b, h, g: (b, 0, off + g * nh + h))

    out = pl.pallas_call(
        functools.partial(_swa_kernel, seq=seq),
        out_shape=jax.ShapeDtypeStruct((batch, seq, nh * HEAD_DIM), BF16),
        grid=(batch, nh, len(SWA_GROUPS)),
        in_specs=[spec(0), spec(SWA_HEADS), spec(2 * SWA_HEADS)],
        out_specs=pl.BlockSpec((None, seq, HEAD_DIM), lambda b, h, g: (b, 0, h)),
        scratch_shapes=[pltpu.VMEM((seq, HEAD_DIM), F32), pltpu.VMEM((seq, 1), F32), pltpu.VMEM((seq, 1), F32)],
        compiler_params=_cp("parallel", "parallel", "arbitrary"),
        name="dilated_swa",
    )(qkv3, qkv3, qkv3)
    return out.reshape(batch * seq, nh * HEAD_DIM)


def _gdn_conv_kernel(x_ref, prev_ref, next_ref, w_ref, o_ref, buf_ref, *, ts, tiles_per_seq):
    i, c = pl.program_id(0), pl.program_id(1)
    pad = SUBLANES
    first = (i % tiles_per_seq) == 0
    last = (i % tiles_per_seq) == tiles_per_seq - 1
    buf_ref[0:pad, :] = jnp.where(first, 0.0, prev_ref[...])
    buf_ref[pad:pad + ts, :] = x_ref[...]
    buf_ref[pad + ts:pad + ts + pad, :] = jnp.where(last, 0.0, next_ref[...])
    acc = jnp.zeros((ts, HEAD_DIM), F32)
    for t in range(GDN_CONV):
        acc = acc + buf_ref[pl.ds(pad - GDN_CONV // 2 + t, ts), :] * w_ref[t:t + 1, :]
    y = acc * jax.nn.sigmoid(acc)
    inv = lax.rsqrt(jnp.sum(y * y, axis=-1, keepdims=True) + EPS)
    is_q, is_k = c < GDN_HEADS, jnp.logical_and(c >= GDN_HEADS, c < 2 * GDN_HEADS)
    mul = jnp.where(is_q, inv * (HEAD_DIM ** -0.5), jnp.where(is_k, inv, 1.0))
    o_ref[...] = (y * mul).astype(o_ref.dtype)


def gdn_conv_norm(x, conv_w, seq):
    t, ctot = x.shape
    ts = _tile(seq, 1024)
    nb8 = t // SUBLANES
    per8 = ts // SUBLANES
    return pl.pallas_call(
        functools.partial(_gdn_conv_kernel, ts=ts, tiles_per_seq=seq // ts),
        out_shape=jax.ShapeDtypeStruct((ctot // HEAD_DIM, t, HEAD_DIM), BF16),
        grid=(t // ts, ctot // HEAD_DIM),
        in_specs=[
            pl.BlockSpec((ts, HEAD_DIM), lambda i, c: (i, c)),
            pl.BlockSpec((SUBLANES, HEAD_DIM), lambda i, c: (jnp.maximum(i * per8 - 1, 0), c)),
            pl.BlockSpec((SUBLANES, HEAD_DIM), lambda i, c: (jnp.minimum((i + 1) * per8, nb8 - 1), c)),
            pl.BlockSpec((GDN_CONV, HEAD_DIM), lambda i, c: (0, c)),
        ],
        out_specs=pl.BlockSpec((None, ts, HEAD_DIM), lambda i, c: (c, i, 0)),
        scratch_shapes=[pltpu.VMEM((ts + 2 * SUBLANES, HEAD_DIM), F32)],
        compiler_params=_cp("parallel", "arbitrary"),
        name="gdn_conv_norm",
    )(x, x, x, conv_w)


def _gdn_gate_kernel(ba_ref, alog_ref, dtb_ref, beta_ref, gc_ref, *, ts):
    x = ba_ref[...]
    nhd = 2 * GDN_HEADS
    lane = lax.broadcasted_iota(jnp.int32, (ts, LANES), 1)
    row = lax.broadcasted_iota(jnp.int32, (ts, LANES), 0) % GDN_CHUNK
    beta_ref[...] = jax.nn.sigmoid(x)
    z = x + dtb_ref[...]
    softplus = jnp.maximum(z, 0.0) + jnp.log(1.0 + jnp.exp(-jnp.abs(z)))
    g = jnp.where(jnp.logical_and(lane >= nhd, lane < 2 * nhd), -jnp.exp(alog_ref[...]) * softplus, 0.0)
    pre, suf = g, g
    s = 1
    while s < GDN_CHUNK:
        pre = pre + jnp.where(row >= s, pltpu.roll(pre, s, 0), 0.0)
        suf = suf + jnp.where(row < GDN_CHUNK - s, pltpu.roll(suf, ts - s, 0), 0.0)
        s *= 2
    gc_ref[...] = jnp.where(lane < nhd + GDN_HEADS, pre, suf)


def gdn_gates(ba, a_log, dt_bias):
    t = ba.shape[0]
    ts = _tile(t, 1024)
    nhd = 2 * GDN_HEADS
    pad = lambda v: jnp.zeros((1, LANES), F32).at[0, nhd:2 * nhd].set(v.reshape(-1).astype(F32))
    shp = jax.ShapeDtypeStruct((t, LANES), F32)
    vec = pl.BlockSpec((1, LANES), lambda i: (0, 0))
    blk = pl.BlockSpec((ts, LANES), lambda i: (i, 0))
    return pl.pallas_call(
        functools.partial(_gdn_gate_kernel, ts=ts),
        out_shape=(shp, shp),
        grid=(t // ts,),
        in_specs=[blk, vec, vec],
        out_specs=(blk, blk),
        compiler_params=_cp("parallel"),
        name="gdn_gates",
    )(ba, pad(a_log), pad(dt_bias))


def _nt(a, b):
    return lax.dot_general(a, b, (((1,), (1,)), ((), ())), preferred_element_type=F32)


def _tn(a, b):
    return lax.dot_general(a, b, (((0,), (0,)), ((), ())), preferred_element_type=F32)


def _mmb(a, b):
    return jnp.dot(a.astype(BF16), b.astype(BF16), preferred_element_type=F32)


def _gdn_chunks(qs, ks, vs, bcols, gcols, states, revs):
    c = GDN_CHUNK
    idx = range(len(qs))
    ri = lax.broadcasted_iota(jnp.int32, (c, c), 0)
    ci = lax.broadcasted_iota(jnp.int32, (c, c), 1)
    strict = {False: ri > ci, True: ri < ci}
    incl = {False: ri >= ci, True: ri <= ci}
    eye = jnp.where(ri == ci, 1.0, 0.0)
    lane = lax.broadcasted_iota(jnp.int32, (c, LANES), 1)
    ones3 = jnp.where(lane < 3, 1.0, 0.0)
    ones3_hi = jnp.where(jnp.logical_and(lane >= 3, lane < 6), 1.0, 0.0)
    g_last = [gcols[i][0:1, :] if revs[i] else gcols[i][c - 1:c, :] for i in idx]
    egc = [jnp.exp(gcols[i]) for i in idx]
    q_state = [_mmb(qs[i] * egc[i], states[i]) for i in idx]
    diff = []
    for i in idx:
        g1 = gcols[i].astype(BF16).astype(F32)
        g2 = (gcols[i] - g1).astype(BF16).astype(F32)
        g3 = (gcols[i] - g1 - g2).astype(BF16).astype(F32)
        pieces = jnp.where(lane == 0, g1, jnp.where(lane == 1, g2, jnp.where(lane == 2, g3, 0.0)))
        xm = pieces + ones3_hi
        ym = ones3 - pltpu.roll(pieces, 3, 1)
        diff.append(_nt(xm.astype(BF16), ym.astype(BF16)))
    decay = [jnp.where(incl[revs[i]], jnp.exp(jnp.where(incl[revs[i]], diff[i], 0.0)), 0.0) for i in idx]
    kb = [ks[i] * bcols[i] for i in idx]
    k16 = [ks[i].astype(BF16) for i in idx]
    lm = [jnp.where(strict[revs[i]], _nt(kb[i].astype(BF16), k16[i]) * decay[i], 0.0) for i in idx]
    attn = [_nt(qs[i].astype(BF16), k16[i]) * decay[i] for i in idx]
    tinv = [eye - lm[i] for i in idx]
    pw = lm
    n = 2
    while n < c:
        pw = [_mmb(pw[i], pw[i]) for i in idx]
        tinv = [tinv[i] + _mmb(tinv[i], pw[i]) for i in idx]
        n *= 2
    u = [_mmb(tinv[i], jnp.concatenate([vs[i] * bcols[i], kb[i] * egc[i]], axis=1)) for i in idx]
    v_new = [u[i][:, :HEAD_DIM] - _mmb(u[i][:, HEAD_DIM:], states[i]) for i in idx]
    outs = [q_state[i] + _mmb(attn[i], v_new[i]) for i in idx]
    k_e = [ks[i] * jnp.exp(g_last[i] - gcols[i]) for i in idx]
    new_states = [states[i] * jnp.exp(g_last[i]) + _tn(k_e[i].astype(BF16), v_new[i].astype(BF16)) for i in idx]
    return outs, new_states


def _gdn_kernel(qf_ref, kf_ref, vf_ref, bf_ref, gf_ref, qb_ref, kb_ref, vb_ref, bb_ref, gb_ref,
                of_ref, ob_ref, state_ref, *, chunks):
    @pl.when(pl.program_id(1) == 0)
    def _():
        state_ref[...] = jnp.zeros_like(state_ref)

    nh = GDN_HEADS

    def body(it, carry):
        rows_f = pl.ds(pl.multiple_of(it * GDN_CHUNK, GDN_CHUNK), GDN_CHUNK)
        rows_b = pl.ds(pl.multiple_of((chunks - 1 - it) * GDN_CHUNK, GDN_CHUNK), GDN_CHUNK)
        tabs = ((qf_ref, kf_ref, vf_ref, bf_ref[rows_f, :], gf_ref[rows_f, :], rows_f, 0),
                (qb_ref, kb_ref, vb_ref, bb_ref[rows_b, :], gb_ref[rows_b, :], rows_b, nh))
        qs, ks, vs, bcols, gcols, states, revs = [], [], [], [], [], [], []
        for d, (q_ref, k_ref, v_ref, bt, gt, rows, lane0) in enumerate(tabs):
            for h in range(nh):
                qs.append(q_ref[h, rows, :].astype(F32))
                ks.append(k_ref[h, rows, :].astype(F32))
                vs.append(v_ref[h, rows, :].astype(F32))
                bcols.append(bt[:, lane0 + h:lane0 + h + 1])
                gcols.append(gt[:, 2 * nh + lane0 + h:2 * nh + lane0 + h + 1])
                states.append(state_ref[d * nh + h])
                revs.append(d == 1)
        outs, new_states = _gdn_chunks(qs, ks, vs, bcols, gcols, states, revs)
        for d, o_ref in enumerate((of_ref, ob_ref)):
            for h in range(nh):
                state_ref[d * nh + h] = new_states[d * nh + h]
                o_ref[h, tabs[d][5], :] = outs[d * nh + h].astype(o_ref.dtype)
        return carry

    lax.fori_loop(0, chunks, body, 0)


def gdn_scan(qkv_h, beta, gc, batch, seq):
    t = batch * seq
    rows = _tile(seq, 512)
    nblk = seq // rows
    fwd = lambda b, ib: b * nblk + ib
    bwd = lambda b, ib: b * nblk + nblk - 1 - ib

    def specs(blk):
        qspec = lambda part: pl.BlockSpec((GDN_HEADS, rows, HEAD_DIM), lambda b, ib: (part, blk(b, ib), 0))
        gspec = pl.BlockSpec((rows, LANES), lambda b, ib: (blk(b, ib), 0))
        return [qspec(0), qspec(1), qspec(2), gspec, gspec]

    ospec = lambda blk: pl.BlockSpec((GDN_HEADS, rows, HEAD_DIM), lambda b, ib: (0, blk(b, ib), 0))
    out = jax.ShapeDtypeStruct((GDN_HEADS, t, HEAD_DIM), BF16)
    return pl.pallas_call(
        functools.partial(_gdn_kernel, chunks=rows // GDN_CHUNK),
        out_shape=(out, out),
        grid=(batch, nblk),
        in_specs=specs(fwd) + specs(bwd),
        out_specs=(ospec(fwd), ospec(bwd)),
        scratch_shapes=[pltpu.VMEM((2 * GDN_HEADS, HEAD_DIM, HEAD_DIM), F32)],
        compiler_params=_cp("parallel", "arbitrary"),
        name="gdn_scan",
    )(qkv_h, qkv_h, qkv_h, beta, gc, qkv_h, qkv_h, qkv_h, beta, gc)


def _gdn_post_kernel(of_ref, ob_ref, z_ref, g_ref, o_ref):
    o = of_ref[...].astype(F32) + ob_ref[...].astype(F32)
    r = lax.rsqrt(jnp.mean(o * o, axis=-1, keepdims=True) + EPS)
    z = z_ref[...].astype(F32)
    o_ref[...] = (o * r * g_ref[...] * (z * jax.nn.sigmoid(z))).astype(o_ref.dtype)


def gdn_post(o_f, o_b, z, norm_g):
    nhd, t, _ = o_f.shape
    ts = _tile(t, 1024)
    hspec = pl.BlockSpec((None, ts, HEAD_DIM), lambda i, h: (h, i, 0))
    tspec = pl.BlockSpec((ts, HEAD_DIM), lambda i, h: (i, h))
    return pl.pallas_call(
        _gdn_post_kernel,
        out_shape=jax.ShapeDtypeStruct((t, nhd * HEAD_DIM), BF16),
        grid=(t // ts, nhd),
        in_specs=[hspec, hspec, tspec, pl.BlockSpec((1, HEAD_DIM), lambda i, h: (0, 0))],
        out_specs=tspec,
        compiler_params=_cp("parallel", "arbitrary"),
        name="gdn_post",
    )(o_f, o_b, z, norm_g.reshape(1, HEAD_DIM).astype(F32))


def _mem_attn_kernel(q_ref, k_ref, v_ref, o_ref):
    scale = MEM_HEAD_DIM ** -0.5
    for h in range(MEM_HEADS):
        cols = slice(h * MEM_HEAD_DIM, (h + 1) * MEM_HEAD_DIM)
        s = _nt(q_ref[:, cols], k_ref[:, cols]) * scale
        p = jnp.exp(s - jnp.max(s, axis=-1, keepdims=True))
        p = p / jnp.sum(p, axis=-1, keepdims=True)
        o_ref[:, cols] = jnp.dot(p.astype(BF16), v_ref[:, cols], preferred_element_type=F32).astype(o_ref.dtype)


def mem_attention(mq, kv, batch, seq, mem_tokens):
    tq = _tile(seq, 512)
    per = seq // tq
    return pl.pallas_call(
        _mem_attn_kernel,
        out_shape=jax.ShapeDtypeStruct(mq.shape, BF16),
        grid=(batch * per,),
        in_specs=[
            pl.BlockSpec((tq, MEM_WIDTH), lambda i: (i, 0)),
            pl.BlockSpec((mem_tokens, MEM_WIDTH), lambda i: (i // per, 0)),
            pl.BlockSpec((mem_tokens, MEM_WIDTH), lambda i: (i // per, 1)),
        ],
        out_specs=pl.BlockSpec((tq, MEM_WIDTH), lambda i: (i, 0)),
        compiler_params=_cp("parallel"),
        name="mem_attention",
    )(mq, kv, kv)


def _merge_kernel(oa_ref, ob_ref, om_ref, wa_ref, wb_ref, wm_ref, g0_ref, g1_ref, g2_ref, o_ref):
    ya = jnp.dot(oa_ref[...], wa_ref[...], preferred_element_type=F32)
    yb = jnp.dot(ob_ref[...], wb_ref[...], preferred_element_type=F32)
    ym = jnp.dot(om_ref[...], wm_ref[...], preferred_element_type=F32)
    mix = g0_ref[...].astype(F32) * ya + g1_ref[...].astype(F32) * yb + g2_ref[...].astype(F32) * ym
    o_ref[...] = mix.astype(o_ref.dtype)


def gated_merge(o_a, o_b, o_m, w_a, w_b, w_m, gates):
    t = o_a.shape[0]
    d = w_a.shape[1]
    tm, tn = _tile(t, 1024), _tile(d, 512)
    nj = d // tn
    act = lambda o: pl.BlockSpec((tm, o.shape[1]), lambda i, j: (i, 0))
    wsp = lambda w: pl.BlockSpec((w.shape[0], tn), lambda i, j: (0, j))
    gsp = lambda br: pl.BlockSpec((tm, tn), lambda i, j: (i, br * nj + j))
    return pl.pallas_call(
        _merge_kernel,
        out_shape=jax.ShapeDtypeStruct((t, d), BF16),
        grid=(t // tm, nj),
        in_specs=[act(o_a), act(o_b), act(o_m), wsp(w_a), wsp(w_b), wsp(w_m), gsp(0), gsp(1), gsp(2)],
        out_specs=pl.BlockSpec((tm, tn), lambda i, j: (i, j)),
        compiler_params=_cp("parallel", "arbitrary"),
        name="gated_merge",
    )(o_a, o_b, o_m, w_a, w_b, w_m, gates, gates, gates)


ROUTE_ROWS = SUBLANES + N_EXPERTS


def _route_kernel(x_ref, g_ref, w_ref, b_ref, xn_ref, eid_ref, wt_ref):
    x = x_ref[...]
    r = lax.rsqrt(jnp.mean(x * x, axis=-1, keepdims=True) + EPS)
    xn = x * r * g_ref[...]
    xn_ref[...] = xn
    tm = x.shape[0]
    lg = lax.dot_general(w_ref[...], xn, (((1,), (1,)), ((), ())), preferred_element_type=F32,
                         precision=lax.Precision.HIGHEST) + b_ref[...]
    gl = [lg[i:i + 1, :] for i in range(N_GROUPS)]
    gmax = functools.reduce(jnp.maximum, gl)
    grp = jnp.full((1, tm), N_GROUPS - 1, jnp.int32)
    for i in range(N_GROUPS - 2, -1, -1):
        grp = jnp.where(gl[i] == gmax, i, grp)
    p_grp = 1.0 / functools.reduce(jnp.add, [jnp.exp(v - gmax) for v in gl])
    sel = lg[SUBLANES + (N_GROUPS - 1) * EXPERTS_PER_GROUP:SUBLANES + N_GROUPS * EXPERTS_PER_GROUP, :]
    for i in range(N_GROUPS - 2, -1, -1):
        lo = SUBLANES + i * EXPERTS_PER_GROUP
        sel = jnp.where(grp == i, lg[lo:lo + EXPERTS_PER_GROUP, :], sel)
    rowi = lax.broadcasted_iota(jnp.int32, (EXPERTS_PER_GROUP, tm), 0)
    v1 = jnp.max(sel, axis=0, keepdims=True)
    i1 = jnp.min(jnp.where(sel == v1, rowi, EXPERTS_PER_GROUP), axis=0, keepdims=True)
    sel2 = jnp.where(rowi == i1, -jnp.inf, sel)
    v2 = jnp.max(sel2, axis=0, keepdims=True)
    i2 = jnp.min(jnp.where(sel2 == v2, rowi, EXPERTS_PER_GROUP), axis=0, keepdims=True)
    e2 = jnp.exp(v2 - v1)
    w1 = p_grp / (1.0 + e2)
    eid_ref[0:1, :] = grp * EXPERTS_PER_GROUP + i1
    eid_ref[1:2, :] = grp * EXPERTS_PER_GROUP + i2
    wt_ref[0:1, :] = w1
    wt_ref[1:2, :] = w1 * e2


def moe_route(x, g, w_grp, b_grp, w_exp):
    t, d = x.shape
    tm = _tile(t, 512)
    w = jnp.zeros((ROUTE_ROWS, d), F32).at[:N_GROUPS].set(w_grp.T).at[SUBLANES:].set(w_exp.T)
    b = jnp.zeros((ROUTE_ROWS, 1), F32).at[:N_GROUPS, 0].set(b_grp.astype(F32))
    return pl.pallas_call(
        _route_kernel,
        out_shape=(jax.ShapeDtypeStruct((t, d), F32), jax.ShapeDtypeStruct((EXPERT_TOPK, t), jnp.int32),
                   jax.ShapeDtypeStruct((EXPERT_TOPK, t), F32)),
        grid=(t // tm,),
        in_specs=[pl.BlockSpec((tm, d), lambda i: (i, 0)), pl.BlockSpec((1, d), lambda i: (0, 0)),
                  pl.BlockSpec((ROUTE_ROWS, d), lambda i: (0, 0)), pl.BlockSpec((ROUTE_ROWS, 1), lambda i: (0, 0))],
        out_specs=(pl.BlockSpec((tm, d), lambda i: (i, 0)), pl.BlockSpec((EXPERT_TOPK, tm), lambda i: (0, i)),
                   pl.BlockSpec((EXPERT_TOPK, tm), lambda i: (0, i))),
        compiler_params=_cp("parallel"),
        name="moe_route",
    )(x, g.reshape(1, d), w, b)


def _rank_kernel(e_ref, rank_ref, cnt_ref, *, tb):
    @pl.when(pl.program_id(0) == 0)
    def _():
        cnt_ref[...] = jnp.zeros_like(cnt_ref)

    e = e_ref[...]
    onehot = jnp.where(lax.broadcasted_iota(jnp.int32, (N_EXPERTS, tb), 0) == e, 1.0, 0.0)
    earlier = jnp.where(lax.broadcasted_iota(jnp.int32, (tb, tb), 0) < lax.broadcasted_iota(jnp.int32, (tb, tb), 1),
                        1.0, 0.0)
    before = _mmb(onehot, earlier) + cnt_ref[...]
    rank_ref[...] = jnp.sum(onehot * before, axis=0, keepdims=True).astype(jnp.int32)
    cnt_ref[...] = cnt_ref[...] + jnp.sum(onehot, axis=1, keepdims=True)


def moe_rank(e_flat):
    n = e_flat.shape[1]
    tb = _tile(n, 512)
    return pl.pallas_call(
        functools.partial(_rank_kernel, tb=tb),
        out_shape=(jax.ShapeDtypeStruct((1, n), jnp.int32), jax.ShapeDtypeStruct((N_EXPERTS, 1), F32)),
        grid=(n // tb,),
        in_specs=[pl.BlockSpec((1, tb), lambda i: (0, i))],
        out_specs=(pl.BlockSpec((1, tb), lambda i: (0, i)), pl.BlockSpec((N_EXPERTS, 1), lambda i: (0, 0))),
        compiler_params=_cp("arbitrary"),
        name="moe_rank",
    )(e_flat)


def _dispatch_kernel(dest_ref, cnt_ref, pstart_ref, x_ref, xs_ref, zero_ref, sem, *, tb, tokens, nblk):
    i = pl.program_id(0)

    def row_copy(src_ref, src_row, dst_row):
        return pltpu.make_async_copy(src_ref.at[pl.ds(src_row, 1)], xs_ref.at[pl.ds(dst_row, 1)], sem)

    def block_copy(blk):
        return pltpu.make_async_copy(zero_ref, xs_ref.at[pl.ds(blk * MOE_BLOCK, MOE_BLOCK)], sem)

    @pl.when(i == 0)
    def _():
        zero_ref[...] = jnp.zeros_like(zero_ref)
        last = N_EXPERTS - 1
        n_used = (pstart_ref[last] + cnt_ref[last] + MOE_BLOCK - 1) // MOE_BLOCK

        def fill_block(blk, carry):
            block_copy(blk).start()
            block_copy(blk).wait()
            return carry

        lax.fori_loop(n_used, nblk, fill_block, 0)

        def per_expert(e, carry):
            npad = (-cnt_ref[e]) % MOE_BLOCK
            base = pstart_ref[e] + cnt_ref[e]

            def fill(r, c2):
                row_copy(zero_ref, 0, base + r).start()
                return c2

            lax.fori_loop(0, npad, fill, 0)

            def drain(r, c2):
                row_copy(zero_ref, 0, base + r).wait()
                return c2

            lax.fori_loop(0, npad, drain, 0)
            return carry

        lax.fori_loop(0, N_EXPERTS, per_expert, 0)

    def issue(j, carry):
        for kk in range(EXPERT_TOPK):
            row_copy(x_ref, j, dest_ref[kk * tokens + i * tb + j]).start()
        return carry

    lax.fori_loop(0, tb, issue, 0)

    def drain(j, carry):
        for kk in range(EXPERT_TOPK):
            row_copy(x_ref, j, dest_ref[kk * tokens + i * tb + j]).wait()
        return carry

    lax.fori_loop(0, tb, drain, 0)


def moe_dispatch(xn, dest_flat, counts, pad_start, n_slots):
    t, d = xn.shape
    tb = _tile(t, 512)
    return pl.pallas_call(
        functools.partial(_dispatch_kernel, tb=tb, tokens=t, nblk=n_slots // MOE_BLOCK),
        out_shape=jax.ShapeDtypeStruct((n_slots, d), xn.dtype),
        grid_spec=pltpu.PrefetchScalarGridSpec(
            num_scalar_prefetch=3,
            grid=(t // tb,),
            in_specs=[pl.BlockSpec((tb, d), lambda i, *_: (i, 0))],
            out_specs=pl.BlockSpec(memory_space=pl.ANY),
            scratch_shapes=[pltpu.VMEM((MOE_BLOCK, d), xn.dtype), pltpu.SemaphoreType.DMA],
        ),
        compiler_params=pltpu.CompilerParams(dimension_semantics=("arbitrary",), has_side_effects=True,
                                             vmem_limit_bytes=VMEM_LIMIT),
        name="moe_dispatch",
    )(dest_flat, counts, pad_start, xn)


def _expert_kernel(be_ref, nused_ref, x_ref, wg_ref, wu_ref, wd_ref, y_ref):
    @pl.when(pl.program_id(0) < nused_ref[0])
    def _():
        x = x_ref[...].astype(BF16)
        hg = jnp.dot(x, wg_ref[...], preferred_element_type=F32)
        hu = jnp.dot(x, wu_ref[...], preferred_element_type=F32)
        hmid = (hg * jax.nn.sigmoid(hg) * hu).astype(BF16)
        y_ref[...] = jnp.dot(hmid, wd_ref[...], preferred_element_type=F32).astype(y_ref.dtype)

    @pl.when(pl.program_id(0) >= nused_ref[0])
    def _():
        y_ref[...] = jnp.zeros_like(y_ref)


def moe_experts(xs, blk_expert, n_used, w_gate, w_up, w_down):
    p, d = xs.shape
    de = w_gate.shape[2]
    nblk = p // MOE_BLOCK
    row = lambda i, be, nu: (jnp.minimum(i, nu[0] - 1), 0)
    wsel = lambda i, be, nu: (be[jnp.minimum(i, nu[0] - 1)], 0, 0)
    return pl.pallas_call(
        _expert_kernel,
        out_shape=jax.ShapeDtypeStruct((p, d), F32),
        grid_spec=pltpu.PrefetchScalarGridSpec(
            num_scalar_prefetch=2,
            grid=(nblk,),
            in_specs=[pl.BlockSpec((MOE_BLOCK, d), row), pl.BlockSpec((None, d, de), wsel),
                      pl.BlockSpec((None, d, de), wsel), pl.BlockSpec((None, de, d), wsel)],
            out_specs=pl.BlockSpec((MOE_BLOCK, d), lambda i, be, nu: (i, 0)),
        ),
        compiler_params=_cp("arbitrary"),
        name="moe_experts",
    )(blk_expert, n_used, xs, w_gate, w_up, w_down)


def _combine_kernel(dest_ref, x_ref, wt_ref, g_ref, ys_ref, o_ref, y0_ref, y1_ref, sem, *, tb, tokens):
    i = pl.program_id(0)
    bufs = (y0_ref, y1_ref)

    def row_copy(kk, j):
        return pltpu.make_async_copy(ys_ref.at[pl.ds(dest_ref[kk * tokens + i * tb + j], 1)],
                                     bufs[kk].at[pl.ds(j, 1)], sem)

    def issue(j, carry):
        for kk in range(EXPERT_TOPK):
            row_copy(kk, j).start()
        return carry

    lax.fori_loop(0, tb, issue, 0)

    def drain(j, carry):
        for kk in range(EXPERT_TOPK):
            row_copy(kk, j).wait()
        return carry

    lax.fori_loop(0, tb, drain, 0)
    h = x_ref[...] + wt_ref[:, 0:1] * y0_ref[...] + wt_ref[:, 1:2] * y1_ref[...]
    r = lax.rsqrt(jnp.mean(h * h, axis=-1, keepdims=True) + EPS)
    o_ref[...] = h * r * g_ref[...]


def moe_combine(x, ys, dest_flat, wts_tok, g_final):
    t, d = x.shape
    tb = _tile(t, 256)
    return pl.pallas_call(
        functools.partial(_combine_kernel, tb=tb, tokens=t),
        out_shape=jax.ShapeDtypeStruct((t, d), F32),
        grid_spec=pltpu.PrefetchScalarGridSpec(
            num_scalar_prefetch=1,
            grid=(t // tb,),
            in_specs=[pl.BlockSpec((tb, d), lambda i, dr: (i, 0)),
                      pl.BlockSpec((tb, EXPERT_TOPK), lambda i, dr: (i, 0)),
                      pl.BlockSpec((1, d), lambda i, dr: (0, 0)),
                      pl.BlockSpec(memory_space=pl.ANY)],
            out_specs=pl.BlockSpec((tb, d), lambda i, dr: (i, 0)),
            scratch_shapes=[pltpu.VMEM((tb, d), F32), pltpu.VMEM((tb, d), F32), pltpu.SemaphoreType.DMA],
        ),
        compiler_params=_cp("arbitrary"),
        name="moe_combine",
    )(dest_flat, x, wts_tok, g_final.reshape(1, d), ys)


def hier_moe_final(x2, g_ffn, w_rg, b_rg, w_re, w_gate, w_up, w_down, g_final):
    t, d = x2.shape
    xn, eid, wts = moe_route(x2, g_ffn, w_rg, b_rg, w_re)
    n = EXPERT_TOPK * t
    e_flat = eid.reshape(1, n)
    rank, counts_f = moe_rank(e_flat)
    counts = counts_f.reshape(N_EXPERTS).astype(jnp.int32)
    padded = (counts + MOE_BLOCK - 1) // MOE_BLOCK * MOE_BLOCK
    pad_end = jnp.cumsum(padded)
    pad_start = pad_end - padded
    dest = (jnp.take(pad_start, e_flat[0]) + rank[0]).astype(jnp.int32)
    nblk = -(-n // MOE_BLOCK) + N_EXPERTS
    blk_expert = jnp.minimum(jnp.searchsorted(pad_end, jnp.arange(nblk) * MOE_BLOCK, side="right"),
                             N_EXPERTS - 1).astype(jnp.int32)
    n_used = (pad_end[-1:] // MOE_BLOCK).astype(jnp.int32)
    xs = moe_dispatch(xn, dest, counts, pad_start.astype(jnp.int32), nblk * MOE_BLOCK)
    ys = moe_experts(xs, blk_expert, n_used, w_gate.astype(BF16), w_up.astype(BF16), w_down.astype(BF16))
    return moe_combine(x2, ys, dest, wts.T, g_final)


def kernel(x, mem, g_mix, w_in, b_gate, gdn_conv, gdn_a_log, gdn_dt_bias, gdn_norm_g, g_mem, w_mem_kv, w_o_swa, w_o_gdn, w_o_mem, w_out, g_ffn, w_route_group, b_route_group, w_route_expert, w_expert_gate, w_expert_up, w_expert_down, g_final):
    batch, seq, d = x.shape
    depth = w_in.shape[0]
    assert depth == 1, "the final RMSNorm is fused into the layer's MoE combine"
    t = batch * seq
    mem_tokens = mem.shape[1]
    h = x.reshape(t, d)
    o0 = 3 * SWA_WIDTH
    o1 = o0 + 3 * GDN_WIDTH
    o2 = o1 + GDN_WIDTH
    o3 = o2 + 4 * GDN_HEADS
    o4 = o3 + MEM_WIDTH
    tables = rope_tables(seq)
    for l in range(depth):
        w = w_in[l].astype(BF16)
        a = rmsnorm_rows(h, g_mix[l], BF16)
        qkv_a = matmul(a, w[:, :o0], F32, rope=(tables, seq, 2 * SWA_WIDTH))
        o_a = dilated_swa(qkv_a, batch, seq)
        qkv_b = gdn_conv_norm(matmul(a, w[:, o0:o1], F32), gdn_conv[l], seq)
        z_b = matmul(a, w[:, o1:o2], BF16)
        w_ba = jnp.zeros((d, LANES), BF16).at[:, :o3 - o2].set(w[:, o2:o3])
        beta, gc = gdn_gates(matmul(a, w_ba, F32), gdn_a_log[l], gdn_dt_bias[l])
        o_f, o_r = gdn_scan(qkv_b, beta, gc, batch, seq)
        o_b = gdn_post(o_f, o_r, z_b, gdn_norm_g[l])
        kv = matmul(rmsnorm_rows(mem.reshape(batch * mem_tokens, d), g_mem[l], BF16), w_mem_kv[l].astype(BF16), BF16)
        o_m = mem_attention(matmul(a, w[:, o3:o4], BF16), kv, batch, seq, mem_tokens)
        gates = matmul(a, w[:, o4:], BF16, bias=b_gate[l].reshape(-1))
        mix = gated_merge(o_a, o_b, o_m, w_o_swa[l].astype(BF16), w_o_gdn[l].astype(BF16),
                          w_o_mem[l].astype(BF16), gates)
        x2 = matmul(mix, w_out[l].astype(BF16), F32, residual=h)
        h = hier_moe_final(x2, g_ffn[l], w_route_group[l], b_route_group[l], w_route_expert[l],
                           w_expert_gate[l], w_expert_up[l], w_expert_down[l], g_final)
    return h.reshape(batch, seq, d)
```

```python
import functools

import jax
import jax.numpy as jnp
from jax import lax
from jax.experimental import pallas as pl
from jax.experimental.pallas import tpu as pltpu

F32 = jnp.float32
BF16 = jnp.bfloat16

HEAD_DIM = 128
SWA_GROUPS = ((128, 1), (512, 4), (2048, 16))
SWA_HEADS_PER_GROUP = 4
SWA_HEADS = SWA_HEADS_PER_GROUP * len(SWA_GROUPS)
SWA_WIDTH = SWA_HEADS * HEAD_DIM
SWA_BLOCK = 64
SWA_UNROLL = 4
ROPE_THETA = 500000.0
ROPE_DIMS = HEAD_DIM // 4
GDN_HEADS = 12
GDN_WIDTH = GDN_HEADS * HEAD_DIM
GDN_CONV = 5
GDN_CHUNK = 64
MEM_HEADS = 4
MEM_HEAD_DIM = 256
MEM_WIDTH = MEM_HEADS * MEM_HEAD_DIM
N_BRANCH = 3
N_GROUPS = 4
EXPERTS_PER_GROUP = 8
N_EXPERTS = N_GROUPS * EXPERTS_PER_GROUP
EXPERT_TOPK = 2
MOE_BLOCK = 256
EPS = 1e-6
NEG_INF = -1e30

LANES = 128
SUBLANES = 8
VMEM_LIMIT = 48 * 1024 * 1024


def _cp(*sem, vmem=VMEM_LIMIT):
    return pltpu.CompilerParams(dimension_semantics=sem, vmem_limit_bytes=vmem)


def _tile(n, pref):
    t = min(n, pref)
    while n % t:
        t //= 2
    return t


def _rmsnorm_kernel(x_ref, g_ref, o_ref):
    x = x_ref[...].astype(F32)
    r = lax.rsqrt(jnp.mean(x * x, axis=-1, keepdims=True) + EPS)
    o_ref[...] = (x * r * g_ref[...]).astype(o_ref.dtype)


def rmsnorm_rows(x, g, out_dtype):
    m, d = x.shape
    tm = _tile(m, 512)
    return pl.pallas_call(
        _rmsnorm_kernel,
        out_shape=jax.ShapeDtypeStruct((m, d), out_dtype),
        grid=(m // tm,),
        in_specs=[pl.BlockSpec((tm, d), lambda i: (i, 0)), pl.BlockSpec((1, d), lambda i: (0, 0))],
        out_specs=pl.BlockSpec((tm, d), lambda i: (i, 0)),
        compiler_params=_cp("parallel"),
        name="rmsnorm_rows",
    )(x, g.reshape(1, d))


def _mm_kernel(a_ref, w_ref, o_ref):
    o_ref[...] = jnp.dot(a_ref[...], w_ref[...], preferred_element_type=F32).astype(o_ref.dtype)


def _mm_sigmoid_kernel(a_ref, w_ref, b_ref, o_ref):
    z = jnp.dot(a_ref[...], w_ref[...], preferred_element_type=F32) + b_ref[...]
    o_ref[...] = jax.nn.sigmoid(z).astype(o_ref.dtype)


def _mm_residual_kernel(a_ref, w_ref, r_ref, o_ref):
    o_ref[...] = r_ref[...] + jnp.dot(a_ref[...], w_ref[...], preferred_element_type=F32)


EPI_COLS = 2 * HEAD_DIM


def _mm_rope_kernel(a_ref, w_ref, c_ref, s1_ref, s2_ref, o_ref):
    c, s1, s2 = c_ref[...], s1_ref[...], s2_ref[...]
    half = ROPE_DIMS // 2
    for c0 in range(0, o_ref.shape[1], EPI_COLS):
        acc = jnp.dot(a_ref[...], w_ref[:, c0:c0 + EPI_COLS], preferred_element_type=F32)
        for h0 in range(0, EPI_COLS, HEAD_DIM):
            xh = acc[:, h0:h0 + HEAD_DIM]
            o_ref[:, c0 + h0:c0 + h0 + HEAD_DIM] = (
                xh * c + pltpu.roll(xh, HEAD_DIM - half, 1) * s1 + pltpu.roll(xh, half, 1) * s2)


MM_TM = 1024
MM_TN = (1536, 1024, 512)


def _col_tile(n):
    return next((t for t in MM_TN if n % t == 0), n)


def matmul(a, w, out_dtype, *, n=None, col_off=0, bias=None, residual=None, rope=None):
    m, k = a.shape
    n = w.shape[1] if n is None else n
    tm, tn = _tile(m, MM_TM), _col_tile(n)
    assert col_off % tn == 0
    joff = col_off // tn
    a_spec = pl.BlockSpec((tm, k), lambda i, j: (i, 0))
    w_spec = pl.BlockSpec((k, tn), lambda i, j: (0, joff + j))
    o_spec = pl.BlockSpec((tm, tn), lambda i, j: (i, j))
    if bias is not None:
        kern, extra, extra_specs = _mm_sigmoid_kernel, (bias.reshape(1, n),), [pl.BlockSpec((1, tn), lambda i, j: (0, j))]
    elif residual is not None:
        kern, extra, extra_specs = _mm_residual_kernel, (residual,), [o_spec]
    elif rope is not None:
        tables, seq = rope
        assert tn == SWA_WIDTH and n == 3 * SWA_WIDTH
        per = seq // tm
        t_spec = pl.BlockSpec((None, tm, HEAD_DIM), lambda i, j: (j, i % per, 0))
        kern = _mm_rope_kernel
        extra, extra_specs = tuple(tables), [t_spec] * 3
    else:
        kern, extra, extra_specs = _mm_kernel, (), []
    return pl.pallas_call(
        kern,
        out_shape=jax.ShapeDtypeStruct((m, n), out_dtype),
        grid=(m // tm, n // tn),
        in_specs=[a_spec, w_spec] + extra_specs,
        out_specs=o_spec,
        compiler_params=_cp("parallel", "arbitrary"),
        name="matmul",
    )(a, w, *extra)


def rope_tables(seq):
    half = ROPE_DIMS // 2
    inv = ROPE_THETA ** (-jnp.arange(half, dtype=F32) / half)
    ang = jnp.arange(seq, dtype=F32)[:, None] * inv[None, :]
    cos, sin = jnp.cos(ang), jnp.sin(ang)
    zeros = jnp.zeros((seq, HEAD_DIM - ROPE_DIMS), F32)
    zh = jnp.zeros((seq, half), F32)
    c = jnp.concatenate([cos, cos, zeros + 1.0], axis=1)
    s1 = jnp.concatenate([-sin, zh, zeros], axis=1)
    s2 = jnp.concatenate([zh, sin, zeros], axis=1)
    ident = (jnp.ones_like(c), jnp.zeros_like(c), jnp.zeros_like(c))
    return tuple(jnp.stack([tab * (HEAD_DIM ** -0.5), tab, idt]) for tab, idt in zip((c, s1, s2), ident))


CONV_HALO = 16


def _mm_conv_kernel(a_ref, ap_ref, an_ref, w_ref, cw_ref, o_ref, buf_ref, *, tm, tiles_per_seq):
    i, kind = pl.program_id(0), pl.program_id(1)
    halo = CONV_HALO
    rows = tm + 2 * halo
    keep_prev = jnp.where((i % tiles_per_seq) == 0, 0.0, 1.0)
    keep_next = jnp.where((i % tiles_per_seq) == tiles_per_seq - 1, 0.0, 1.0)
    qk_mul = jnp.where(kind == 0, HEAD_DIM ** -0.5, 1.0)
    w = w_ref[...]
    buf_ref[0:halo, :] = jnp.dot(ap_ref[...], w, preferred_element_type=F32) * keep_prev
    buf_ref[halo:halo + tm, :] = jnp.dot(a_ref[...], w, preferred_element_type=F32)
    buf_ref[halo + tm:, :] = jnp.dot(an_ref[...], w, preferred_element_type=F32) * keep_next
    for h in range(GDN_HEADS):
        cols = slice(h * HEAD_DIM, (h + 1) * HEAD_DIM)
        xh = buf_ref[:, cols]
        acc = xh * cw_ref[GDN_CONV // 2:GDN_CONV // 2 + 1, cols]
        for t in range(GDN_CONV):
            if t != GDN_CONV // 2:
                acc = acc + pltpu.roll(xh, (GDN_CONV // 2 - t) % rows, 0) * cw_ref[t:t + 1, cols]
        acc = acc[halo:halo + tm, :]
        y = acc * jax.nn.sigmoid(acc)
        inv = lax.rsqrt(jnp.sum(y * y, axis=-1, keepdims=True) + EPS) * qk_mul
        o_ref[h] = (y * jnp.where(kind < 2, inv, 1.0)).astype(o_ref.dtype)


def matmul_conv_norm(a, w, col_off, conv_w, seq):
    m, k = a.shape
    tn = GDN_WIDTH
    tm = _tile(seq, MM_TM)
    assert col_off % tn == 0 and tm % CONV_HALO == 0
    joff = col_off // tn
    per_h = tm // CONV_HALO
    nblk_h = m // CONV_HALO
    return pl.pallas_call(
        functools.partial(_mm_conv_kernel, tm=tm, tiles_per_seq=seq // tm),
        out_shape=jax.ShapeDtypeStruct((3 * GDN_HEADS, m, HEAD_DIM), BF16),
        grid=(m // tm, 3),
        in_specs=[
            pl.BlockSpec((tm, k), lambda i, j: (i, 0)),
            pl.BlockSpec((CONV_HALO, k), lambda i, j: (jnp.maximum(i * per_h - 1, 0), 0)),
            pl.BlockSpec((CONV_HALO, k), lambda i, j: (jnp.minimum((i + 1) * per_h, nblk_h - 1), 0)),
            pl.BlockSpec((k, tn), lambda i, j: (0, joff + j)),
            pl.BlockSpec((GDN_CONV, tn), lambda i, j: (0, j)),
        ],
        out_specs=pl.BlockSpec((GDN_HEADS, tm, HEAD_DIM), lambda i, j: (j, i, 0)),
        scratch_shapes=[pltpu.VMEM((tm + 2 * CONV_HALO, tn), F32)],
        compiler_params=_cp("parallel", "arbitrary"),
        name="matmul_conv_norm",
    )(a, a, a, w, conv_w)


def _swa_kernel(q_ref, k_ref, v_ref, o_ref, acc_ref, m_ref, l_ref, *, seq):
    g = pl.program_id(2)

    @pl.when(g == 0)
    def _():
        acc_ref[...] = jnp.zeros_like(acc_ref)
        m_ref[...] = jnp.full_like(m_ref, NEG_INF)
        l_ref[...] = jnp.zeros_like(l_ref)

    for gi, (window, dil) in enumerate(SWA_GROUPS):
        radius = window // (2 * dil)
        assert radius <= SWA_BLOCK
        sub = seq // dil
        qb = min(128, sub)
        win = min(sub, qb + 2 * SWA_BLOCK)
        nqb = sub // qb

        @pl.when(g == gi)
        def _(dil=dil, radius=radius, sub=sub, qb=qb, win=win, nqb=nqb):
            def rows(start, size):
                return pl.ds(start, size) if dil == 1 else pl.ds(start, size, stride=dil)

            total = dil * nqb
            unroll = SWA_UNROLL if total % SWA_UNROLL == 0 else 1
            idx = range(unroll)

            def body(step, carry):
                its = [step * unroll + u for u in idx]
                qss = [(it % nqb) * qb for it in its]
                wss = [jnp.clip(qs - SWA_BLOCK, 0, sub - win) for qs in qss]
                q_rows = [rows(it // nqb + dil * qs, qb) for it, qs in zip(its, qss)]
                k_rows = [rows(it // nqb + dil * ws, win) for it, ws in zip(its, wss)]
                q = [q_ref[r, :].astype(BF16) for r in q_rows]
                k = [k_ref[r, :].astype(BF16) for r in k_rows]
                v = [v_ref[r, :].astype(BF16) for r in k_rows]
                s = [_nt(q[u], k[u]) for u in idx]
                off = lax.broadcasted_iota(jnp.int32, (qb, win), 0) - lax.broadcasted_iota(jnp.int32, (qb, win), 1)
                s = [jnp.where(jnp.abs(off + (qss[u] - wss[u])) <= radius, s[u], NEG_INF) for u in idx]
                m_old = [m_ref[r, :] for r in q_rows]
                m_new = [jnp.maximum(m_old[u], jnp.max(s[u], axis=1, keepdims=True)) for u in idx]
                p = [jnp.exp(s[u] - m_new[u]) for u in idx]
                corr = [jnp.exp(m_old[u] - m_new[u]) for u in idx]
                pv = [jnp.dot(p[u].astype(BF16), v[u], preferred_element_type=F32) for u in idx]
                for u in idx:
                    l_ref[q_rows[u], :] = l_ref[q_rows[u], :] * corr[u] + jnp.sum(p[u], axis=1, keepdims=True)
                    acc_ref[q_rows[u], :] = acc_ref[q_rows[u], :] * corr[u] + pv[u]
                    m_ref[q_rows[u], :] = m_new[u]
                return carry

            lax.fori_loop(0, total // unroll, body, 0)

    @pl.when(g == len(SWA_GROUPS) - 1)
    def _():
        o_ref[...] = (acc_ref[...] / l_ref[...]).astype(o_ref.dtype)


def dilated_swa(qkv, batch, seq):
    nh = SWA_HEADS_PER_GROUP
    qkv3 = qkv.reshape(batch, seq, 3 * SWA_WIDTH)

    def spec(off):
        return pl.BlockSpec((None, seq, HEAD_DIM), lambda b, h, g: (b, 0, off + g * nh + h))

    out = pl.pallas_call(
        functools.partial(_swa_kernel, seq=seq),
        out_shape=jax.ShapeDtypeStruct((batch, seq, nh * HEAD_DIM), BF16),
        grid=(batch, nh, len(SWA_GROUPS)),
        in_specs=[spec(0), spec(SWA_HEADS), spec(2 * SWA_HEADS)],
        out_specs=pl.BlockSpec((None, seq, HEAD_DIM), lambda b, h, g: (b, 0, h)),
        scratch_shapes=[pltpu.VMEM((seq, HEAD_DIM), F32), pltpu.VMEM((seq, 1), F32), pltpu.VMEM((seq, 1), F32)],
        compiler_params=_cp("parallel", "parallel", "arbitrary"),
        name="dilated_swa",
    )(qkv3, qkv3, qkv3)
    return out.reshape(batch * seq, nh * HEAD_DIM)


def _gdn_gate_kernel(ba_ref, alog_ref, dtb_ref, beta_ref, gc_ref, *, ts):
    x = ba_ref[...]
    nhd = 2 * GDN_HEADS
    lane = lax.broadcasted_iota(jnp.int32, (ts, LANES), 1)
    row = lax.broadcasted_iota(jnp.int32, (ts, LANES), 0) % GDN_CHUNK
    beta_ref[...] = jax.nn.sigmoid(x)
    z = x + dtb_ref[...]
    softplus = jnp.maximum(z, 0.0) + jnp.log(1.0 + jnp.exp(-jnp.abs(z)))
    g = jnp.where(jnp.logical_and(lane >= nhd, lane < 2 * nhd), -jnp.exp(alog_ref[...]) * softplus, 0.0)
    pre, suf = g, g
    s = 1
    while s < GDN_CHUNK:
        pre = pre + jnp.where(row >= s, pltpu.roll(pre, s, 0), 0.0)
        suf = suf + jnp.where(row < GDN_CHUNK - s, pltpu.roll(suf, ts - s, 0), 0.0)
        s *= 2
    gc_ref[...] = jnp.where(lane < nhd + GDN_HEADS, pre, suf)


def gdn_gates(ba, a_log, dt_bias):
    t = ba.shape[0]
    ts = _tile(t, 1024)
    nhd = 2 * GDN_HEADS
    pad = lambda v: jnp.zeros((1, LANES), F32).at[0, nhd:2 * nhd].set(v.reshape(-1).astype(F32))
    shp = jax.ShapeDtypeStruct((t, LANES), F32)
    vec = pl.BlockSpec((1, LANES), lambda i: (0, 0))
    blk = pl.BlockSpec((ts, LANES), lambda i: (i, 0))
    return pl.pallas_call(
        functools.partial(_gdn_gate_kernel, ts=ts),
        out_shape=(shp, shp),
        grid=(t // ts,),
        in_specs=[blk, vec, vec],
        out_specs=(blk, blk),
        compiler_params=_cp("parallel"),
        name="gdn_gates",
    )(ba, pad(a_log), pad(dt_bias))


def _nt(a, b):
    return lax.dot_general(a, b, (((1,), (1,)), ((), ())), preferred_element_type=F32)


def _tn(a, b):
    return lax.dot_general(a, b, (((0,), (0,)), ((), ())), preferred_element_type=F32)


def _mmb(a, b):
    return jnp.dot(a.astype(BF16), b.astype(BF16), preferred_element_type=F32)


def _gdn_chunks(qs, ks, vs, bcols, gcols, states, revs):
    c = GDN_CHUNK
    idx = range(len(qs))
    ri = lax.broadcasted_iota(jnp.int32, (c, c), 0)
    ci = lax.broadcasted_iota(jnp.int32, (c, c), 1)
    strict = {False: ri > ci, True: ri < ci}
    incl = {False: ri >= ci, True: ri <= ci}
    eye = jnp.where(ri == ci, 1.0, 0.0)
    lane = lax.broadcasted_iota(jnp.int32, (c, LANES), 1)
    ones3 = jnp.where(lane < 3, 1.0, 0.0)
    ones3_hi = jnp.where(jnp.logical_and(lane >= 3, lane < 6), 1.0, 0.0)
    g_last = [gcols[i][0:1, :] if revs[i] else gcols[i][c - 1:c, :] for i in idx]
    egc = [jnp.exp(gcols[i]) for i in idx]
    q_state = [_mmb(qs[i] * egc[i], states[i]) for i in idx]
    diff = []
    for i in idx:
        g1 = gcols[i].astype(BF16).astype(F32)
        g2 = (gcols[i] - g1).astype(BF16).astype(F32)
        g3 = (gcols[i] - g1 - g2).astype(BF16).astype(F32)
        pieces = jnp.where(lane == 0, g1, jnp.where(lane == 1, g2, jnp.where(lane == 2, g3, 0.0)))
        xm = pieces + ones3_hi
        ym = ones3 - pltpu.roll(pieces, 3, 1)
        diff.append(_nt(xm.astype(BF16), ym.astype(BF16)))
    decay = [jnp.where(incl[revs[i]], jnp.exp(jnp.where(incl[revs[i]], diff[i], 0.0)), 0.0) for i in idx]
    kb = [ks[i] * bcols[i] for i in idx]
    k16 = [ks[i].astype(BF16) for i in idx]
    lm = [jnp.where(strict[revs[i]], _nt(kb[i].astype(BF16), k16[i]) * decay[i], 0.0) for i in idx]
    attn = [_nt(qs[i].astype(BF16), k16[i]) * decay[i] for i in idx]
    tinv = [eye - lm[i] for i in idx]
    pw = lm
    n = 2
    while n < c:
        pw = [_mmb(pw[i], pw[i]) for i in idx]
        tinv = [tinv[i] + _mmb(tinv[i], pw[i]) for i in idx]
        n *= 2
    u = [_mmb(tinv[i], jnp.concatenate([vs[i] * bcols[i], kb[i] * egc[i]], axis=1)) for i in idx]
    v_new = [u[i][:, :HEAD_DIM] - _mmb(u[i][:, HEAD_DIM:], states[i]) for i in idx]
    outs = [q_state[i] + _mmb(attn[i], v_new[i]) for i in idx]
    k_e = [ks[i] * jnp.exp(g_last[i] - gcols[i]) for i in idx]
    new_states = [states[i] * jnp.exp(g_last[i]) + _tn(k_e[i].astype(BF16), v_new[i].astype(BF16)) for i in idx]
    return outs, new_states


def _gdn_kernel(qf_ref, kf_ref, vf_ref, bf_ref, gf_ref, qb_ref, kb_ref, vb_ref, bb_ref, gb_ref,
                of_ref, ob_ref, state_ref, *, chunks):
    @pl.when(pl.program_id(1) == 0)
    def _():
        state_ref[...] = jnp.zeros_like(state_ref)

    nh = GDN_HEADS

    def body(it, carry):
        rows_f = pl.ds(pl.multiple_of(it * GDN_CHUNK, GDN_CHUNK), GDN_CHUNK)
        rows_b = pl.ds(pl.multiple_of((chunks - 1 - it) * GDN_CHUNK, GDN_CHUNK), GDN_CHUNK)
        tabs = ((qf_ref, kf_ref, vf_ref, bf_ref[rows_f, :], gf_ref[rows_f, :], rows_f, 0),
                (qb_ref, kb_ref, vb_ref, bb_ref[rows_b, :], gb_ref[rows_b, :], rows_b, nh))
        qs, ks, vs, bcols, gcols, states, revs = [], [], [], [], [], [], []
        for d, (q_ref, k_ref, v_ref, bt, gt, rows, lane0) in enumerate(tabs):
            for h in range(nh):
                qs.append(q_ref[h, rows, :].astype(F32))
                ks.append(k_ref[h, rows, :].astype(F32))
                vs.append(v_ref[h, rows, :].astype(F32))
                bcols.append(bt[:, lane0 + h:lane0 + h + 1])
                gcols.append(gt[:, 2 * nh + lane0 + h:2 * nh + lane0 + h + 1])
                states.append(state_ref[d * nh + h])
                revs.append(d == 1)
        outs, new_states = _gdn_chunks(qs, ks, vs, bcols, gcols, states, revs)
        for d, o_ref in enumerate((of_ref, ob_ref)):
            for h in range(nh):
                state_ref[d * nh + h] = new_states[d * nh + h]
                o_ref[h, tabs[d][5], :] = outs[d * nh + h].astype(o_ref.dtype)
        return carry

    lax.fori_loop(0, chunks, body, 0)


def gdn_scan(qkv_h, beta, gc, batch, seq):
    t = batch * seq
    rows = _tile(seq, 512)
    nblk = seq // rows
    fwd = lambda b, ib: b * nblk + ib
    bwd = lambda b, ib: b * nblk + nblk - 1 - ib

    def specs(blk):
        qspec = lambda part: pl.BlockSpec((GDN_HEADS, rows, HEAD_DIM), lambda b, ib: (part, blk(b, ib), 0))
        gspec = pl.BlockSpec((rows, LANES), lambda b, ib: (blk(b, ib), 0))
        return [qspec(0), qspec(1), qspec(2), gspec, gspec]

    ospec = lambda blk: pl.BlockSpec((GDN_HEADS, rows, HEAD_DIM), lambda b, ib: (0, blk(b, ib), 0))
    out = jax.ShapeDtypeStruct((GDN_HEADS, t, HEAD_DIM), BF16)
    return pl.pallas_call(
        functools.partial(_gdn_kernel, chunks=rows // GDN_CHUNK),
        out_shape=(out, out),
        grid=(batch, nblk),
        in_specs=specs(fwd) + specs(bwd),
        out_specs=(ospec(fwd), ospec(bwd)),
        scratch_shapes=[pltpu.VMEM((2 * GDN_HEADS, HEAD_DIM, HEAD_DIM), F32)],
        compiler_params=_cp("parallel", "arbitrary"),
        name="gdn_scan",
    )(qkv_h, qkv_h, qkv_h, beta, gc, qkv_h, qkv_h, qkv_h, beta, gc)


def _gdn_post_kernel(of_ref, ob_ref, z_ref, g_ref, o_ref):
    o = of_ref[...].astype(F32) + ob_ref[...].astype(F32)
    r = lax.rsqrt(jnp.mean(o * o, axis=-1, keepdims=True) + EPS)
    z = z_ref[...].astype(F32)
    o_ref[...] = (o * r * g_ref[...] * (z * jax.nn.sigmoid(z))).astype(o_ref.dtype)


def gdn_post(o_f, o_b, z, norm_g):
    nhd, t, _ = o_f.shape
    ts = _tile(t, 1024)
    hspec = pl.BlockSpec((None, ts, HEAD_DIM), lambda i, h: (h, i, 0))
    tspec = pl.BlockSpec((ts, HEAD_DIM), lambda i, h: (i, h))
    return pl.pallas_call(
        _gdn_post_kernel,
        out_shape=jax.ShapeDtypeStruct((t, nhd * HEAD_DIM), BF16),
        grid=(t // ts, nhd),
        in_specs=[hspec, hspec, tspec, pl.BlockSpec((1, HEAD_DIM), lambda i, h: (0, 0))],
        out_specs=tspec,
        compiler_params=_cp("parallel", "arbitrary"),
        name="gdn_post",
    )(o_f, o_b, z, norm_g.reshape(1, HEAD_DIM).astype(F32))


def _mem_attn_kernel(q_ref, k_ref, v_ref, o_ref):
    scale = MEM_HEAD_DIM ** -0.5
    for h in range(MEM_HEADS):
        cols = slice(h * MEM_HEAD_DIM, (h + 1) * MEM_HEAD_DIM)
        s = _nt(q_ref[:, cols], k_ref[:, cols]) * scale
        p = jnp.exp(s - jnp.max(s, axis=-1, keepdims=True))
        p = p / jnp.sum(p, axis=-1, keepdims=True)
        o_ref[:, cols] = jnp.dot(p.astype(BF16), v_ref[:, cols], preferred_element_type=F32).astype(o_ref.dtype)


def mem_attention(mq, kv, batch, seq, mem_tokens):
    tq = _tile(seq, 512)
    per = seq // tq
    return pl.pallas_call(
        _mem_attn_kernel,
        out_shape=jax.ShapeDtypeStruct(mq.shape, BF16),
        grid=(batch * per,),
        in_specs=[
            pl.BlockSpec((tq, MEM_WIDTH), lambda i: (i, 0)),
            pl.BlockSpec((mem_tokens, MEM_WIDTH), lambda i: (i // per, 0)),
            pl.BlockSpec((mem_tokens, MEM_WIDTH), lambda i: (i // per, 1)),
        ],
        out_specs=pl.BlockSpec((tq, MEM_WIDTH), lambda i: (i, 0)),
        compiler_params=_cp("parallel"),
        name="mem_attention",
    )(mq, kv, kv)


def _merge_kernel(oa_ref, ob_ref, om_ref, wa_ref, wb_ref, wm_ref, g0_ref, g1_ref, g2_ref, o_ref):
    ya = jnp.dot(oa_ref[...], wa_ref[...], preferred_element_type=F32)
    yb = jnp.dot(ob_ref[...], wb_ref[...], preferred_element_type=F32)
    ym = jnp.dot(om_ref[...], wm_ref[...], preferred_element_type=F32)
    mix = g0_ref[...].astype(F32) * ya + g1_ref[...].astype(F32) * yb + g2_ref[...].astype(F32) * ym
    o_ref[...] = mix.astype(o_ref.dtype)


def gated_merge(o_a, o_b, o_m, w_a, w_b, w_m, gates):
    t = o_a.shape[0]
    d = w_a.shape[1]
    tm, tn = _tile(t, 1024), _tile(d, 512)
    nj = d // tn
    act = lambda o: pl.BlockSpec((tm, o.shape[1]), lambda i, j: (i, 0))
    wsp = lambda w: pl.BlockSpec((w.shape[0], tn), lambda i, j: (0, j))
    gsp = lambda br: pl.BlockSpec((tm, tn), lambda i, j: (i, br * nj + j))
    return pl.pallas_call(
        _merge_kernel,
        out_shape=jax.ShapeDtypeStruct((t, d), BF16),
        grid=(t // tm, nj),
        in_specs=[act(o_a), act(o_b), act(o_m), wsp(w_a), wsp(w_b), wsp(w_m), gsp(0), gsp(1), gsp(2)],
        out_specs=pl.BlockSpec((tm, tn), lambda i, j: (i, j)),
        compiler_params=_cp("parallel", "arbitrary"),
        name="gated_merge",
    )(o_a, o_b, o_m, w_a, w_b, w_m, gates, gates, gates)


ROUTE_ROWS = SUBLANES + N_EXPERTS


def _route_kernel(x_ref, g_ref, w_ref, b_ref, xn_ref, eid_ref, wt_ref):
    x = x_ref[...]
    r = lax.rsqrt(jnp.mean(x * x, axis=-1, keepdims=True) + EPS)
    xn = x * r * g_ref[...]
    xn_ref[...] = xn
    tm = x.shape[0]
    lg = lax.dot_general(w_ref[...], xn, (((1,), (1,)), ((), ())), preferred_element_type=F32,
                         precision=lax.Precision.HIGHEST) + b_ref[...]
    gl = [lg[i:i + 1, :] for i in range(N_GROUPS)]
    gmax = functools.reduce(jnp.maximum, gl)
    grp = jnp.full((1, tm), N_GROUPS - 1, jnp.int32)
    for i in range(N_GROUPS - 2, -1, -1):
        grp = jnp.where(gl[i] == gmax, i, grp)
    p_grp = 1.0 / functools.reduce(jnp.add, [jnp.exp(v - gmax) for v in gl])
    sel = lg[SUBLANES + (N_GROUPS - 1) * EXPERTS_PER_GROUP:SUBLANES + N_GROUPS * EXPERTS_PER_GROUP, :]
    for i in range(N_GROUPS - 2, -1, -1):
        lo = SUBLANES + i * EXPERTS_PER_GROUP
        sel = jnp.where(grp == i, lg[lo:lo + EXPERTS_PER_GROUP, :], sel)
    rowi = lax.broadcasted_iota(jnp.int32, (EXPERTS_PER_GROUP, tm), 0)
    v1 = jnp.max(sel, axis=0, keepdims=True)
    i1 = jnp.min(jnp.where(sel == v1, rowi, EXPERTS_PER_GROUP), axis=0, keepdims=True)
    sel2 = jnp.where(rowi == i1, -jnp.inf, sel)
    v2 = jnp.max(sel2, axis=0, keepdims=True)
    i2 = jnp.min(jnp.where(sel2 == v2, rowi, EXPERTS_PER_GROUP), axis=0, keepdims=True)
    e2 = jnp.exp(v2 - v1)
    w1 = p_grp / (1.0 + e2)
    eid_ref[0:1, :] = grp * EXPERTS_PER_GROUP + i1
    eid_ref[1:2, :] = grp * EXPERTS_PER_GROUP + i2
    wt_ref[0:1, :] = w1
    wt_ref[1:2, :] = w1 * e2


def moe_route(x, g, w_grp, b_grp, w_exp):
    t, d = x.shape
    tm = _tile(t, 512)
    w = jnp.zeros((ROUTE_ROWS, d), F32).at[:N_GROUPS].set(w_grp.T).at[SUBLANES:].set(w_exp.T)
    b = jnp.zeros((ROUTE_ROWS, 1), F32).at[:N_GROUPS, 0].set(b_grp.astype(F32))
    return pl.pallas_call(
        _route_kernel,
        out_shape=(jax.ShapeDtypeStruct((t, d), F32), jax.ShapeDtypeStruct((EXPERT_TOPK, t), jnp.int32),
                   jax.ShapeDtypeStruct((EXPERT_TOPK, t), F32)),
        grid=(t // tm,),
        in_specs=[pl.BlockSpec((tm, d), lambda i: (i, 0)), pl.BlockSpec((1, d), lambda i: (0, 0)),
                  pl.BlockSpec((ROUTE_ROWS, d), lambda i: (0, 0)), pl.BlockSpec((ROUTE_ROWS, 1), lambda i: (0, 0))],
        out_specs=(pl.BlockSpec((tm, d), lambda i: (i, 0)), pl.BlockSpec((EXPERT_TOPK, tm), lambda i: (0, i)),
                   pl.BlockSpec((EXPERT_TOPK, tm), lambda i: (0, i))),
        compiler_params=_cp("parallel"),
        name="moe_route",
    )(x, g.reshape(1, d), w, b)


def _rank_kernel(e_ref, rank_ref, cnt_ref, *, tb):
    @pl.when(pl.program_id(0) == 0)
    def _():
        cnt_ref[...] = jnp.zeros_like(cnt_ref)

    e = e_ref[...]
    onehot = jnp.where(lax.broadcasted_iota(jnp.int32, (N_EXPERTS, tb), 0) == e, 1.0, 0.0)
    earlier = jnp.where(lax.broadcasted_iota(jnp.int32, (tb, tb), 0) < lax.broadcasted_iota(jnp.int32, (tb, tb), 1),
                        1.0, 0.0)
    before = _mmb(onehot, earlier) + cnt_ref[...]
    rank_ref[...] = jnp.sum(onehot * before, axis=0, keepdims=True).astype(jnp.int32)
    cnt_ref[...] = cnt_ref[...] + jnp.sum(onehot, axis=1, keepdims=True)


def moe_rank(e_flat):
    n = e_flat.shape[1]
    tb = _tile(n, 512)
    return pl.pallas_call(
        functools.partial(_rank_kernel, tb=tb),
        out_shape=(jax.ShapeDtypeStruct((1, n), jnp.int32), jax.ShapeDtypeStruct((N_EXPERTS, 1), F32)),
        grid=(n // tb,),
        in_specs=[pl.BlockSpec((1, tb), lambda i: (0, i))],
        out_specs=(pl.BlockSpec((1, tb), lambda i: (0, i)), pl.BlockSpec((N_EXPERTS, 1), lambda i: (0, 0))),
        compiler_params=_cp("arbitrary"),
        name="moe_rank",
    )(e_flat)


def _dispatch_kernel(dest_ref, cnt_ref, pstart_ref, x_ref, xs_ref, zero_ref, sem, *, tb, tokens, nblk):
    i = pl.program_id(0)

    def row_copy(src_ref, src_row, dst_row):
        return pltpu.make_async_copy(src_ref.at[pl.ds(src_row, 1)], xs_ref.at[pl.ds(dst_row, 1)], sem)

    def block_copy(blk):
        return pltpu.make_async_copy(zero_ref, xs_ref.at[pl.ds(blk * MOE_BLOCK, MOE_BLOCK)], sem)

    @pl.when(i == 0)
    def _():
        zero_ref[...] = jnp.zeros_like(zero_ref)
        last = N_EXPERTS - 1
        n_used = (pstart_ref[last] + cnt_ref[last] + MOE_BLOCK - 1) // MOE_BLOCK

        def fill_block(blk, carry):
            block_copy(blk).start()
            block_copy(blk).wait()
            return carry

        lax.fori_loop(n_used, nblk, fill_block, 0)

        def per_expert(e, carry):
            npad = (-cnt_ref[e]) % MOE_BLOCK
            base = pstart_ref[e] + cnt_ref[e]

            def fill(r, c2):
                row_copy(zero_ref, 0, base + r).start()
                return c2

            lax.fori_loop(0, npad, fill, 0)

            def drain(r, c2):
                row_copy(zero_ref, 0, base + r).wait()
                return c2

            lax.fori_loop(0, npad, drain, 0)
            return carry

        lax.fori_loop(0, N_EXPERTS, per_expert, 0)

    def issue(j, carry):
        for kk in range(EXPERT_TOPK):
            row_copy(x_ref, j, dest_ref[kk * tokens + i * tb + j]).start()
        return carry

    lax.fori_loop(0, tb, issue, 0)

    def drain(j, carry):
        for kk in range(EXPERT_TOPK):
            row_copy(x_ref, j, dest_ref[kk * tokens + i * tb + j]).wait()
        return carry

    lax.fori_loop(0, tb, drain, 0)


def moe_dispatch(xn, dest_flat, counts, pad_start, n_slots):
    t, d = xn.shape
    tb = _tile(t, 512)
    return pl.pallas_call(
        functools.partial(_dispatch_kernel, tb=tb, tokens=t, nblk=n_slots // MOE_BLOCK),
        out_shape=jax.ShapeDtypeStruct((n_slots, d), xn.dtype),
        grid_spec=pltpu.PrefetchScalarGridSpec(
            num_scalar_prefetch=3,
            grid=(t // tb,),
            in_specs=[pl.BlockSpec((tb, d), lambda i, *_: (i, 0))],
            out_specs=pl.BlockSpec(memory_space=pl.ANY),
            scratch_shapes=[pltpu.VMEM((MOE_BLOCK, d), xn.dtype), pltpu.SemaphoreType.DMA],
        ),
        compiler_params=pltpu.CompilerParams(dimension_semantics=("arbitrary",), has_side_effects=True,
                                             vmem_limit_bytes=VMEM_LIMIT),
        name="moe_dispatch",
    )(dest_flat, counts, pad_start, xn)


def _expert_kernel(be_ref, nused_ref, x_ref, wg_ref, wu_ref, wd_ref, y_ref):
    @pl.when(pl.program_id(0) < nused_ref[0])
    def _():
        x = x_ref[...].astype(BF16)
        hg = jnp.dot(x, wg_ref[...], preferred_element_type=F32)
        hu = jnp.dot(x, wu_ref[...], preferred_element_type=F32)
        hmid = (hg * jax.nn.sigmoid(hg) * hu).astype(BF16)
        y_ref[...] = jnp.dot(hmid, wd_ref[...], preferred_element_type=F32).astype(y_ref.dtype)

    @pl.when(pl.program_id(0) >= nused_ref[0])
    def _():
        y_ref[...] = jnp.zeros_like(y_ref)


def moe_experts(xs, blk_expert, n_used, w_gate, w_up, w_down):
    p, d = xs.shape
    de = w_gate.shape[2]
    nblk = p // MOE_BLOCK
    row = lambda i, be, nu: (jnp.minimum(i, nu[0] - 1), 0)
    wsel = lambda i, be, nu: (be[jnp.minimum(i, nu[0] - 1)], 0, 0)
    return pl.pallas_call(
        _expert_kernel,
        out_shape=jax.ShapeDtypeStruct((p, d), F32),
        grid_spec=pltpu.PrefetchScalarGridSpec(
            num_scalar_prefetch=2,
            grid=(nblk,),
            in_specs=[pl.BlockSpec((MOE_BLOCK, d), row), pl.BlockSpec((None, d, de), wsel),
                      pl.BlockSpec((None, d, de), wsel), pl.BlockSpec((None, de, d), wsel)],
            out_specs=pl.BlockSpec((MOE_BLOCK, d), lambda i, be, nu: (i, 0)),
        ),
        compiler_params=_cp("arbitrary"),
        name="moe_experts",
    )(blk_expert, n_used, xs, w_gate, w_up, w_down)


def _combine_kernel(dest_ref, x_ref, wt_ref, g_ref, ys_ref, o_ref, y0_ref, y1_ref, sem, *, tb, tokens):
    i = pl.program_id(0)
    bufs = (y0_ref, y1_ref)

    def row_copy(kk, j):
        return pltpu.make_async_copy(ys_ref.at[pl.ds(dest_ref[kk * tokens + i * tb + j], 1)],
                                     bufs[kk].at[pl.ds(j, 1)], sem)

    def issue(j, carry):
        for kk in range(EXPERT_TOPK):
            row_copy(kk, j).start()
        return carry

    lax.fori_loop(0, tb, issue, 0)

    def drain(j, carry):
        for kk in range(EXPERT_TOPK):
            row_copy(kk, j).wait()
        return carry

    lax.fori_loop(0, tb, drain, 0)
    h = x_ref[...] + wt_ref[:, 0:1] * y0_ref[...] + wt_ref[:, 1:2] * y1_ref[...]
    r = lax.rsqrt(jnp.mean(h * h, axis=-1, keepdims=True) + EPS)
    o_ref[...] = h * r * g_ref[...]


def moe_combine(x, ys, dest_flat, wts_tok, g_final):
    t, d = x.shape
    tb = _tile(t, 256)
    return pl.pallas_call(
        functools.partial(_combine_kernel, tb=tb, tokens=t),
        out_shape=jax.ShapeDtypeStruct((t, d), F32),
        grid_spec=pltpu.PrefetchScalarGridSpec(
            num_scalar_prefetch=1,
            grid=(t // tb,),
            in_specs=[pl.BlockSpec((tb, d), lambda i, dr: (i, 0)),
                      pl.BlockSpec((tb, EXPERT_TOPK), lambda i, dr: (i, 0)),
                      pl.BlockSpec((1, d), lambda i, dr: (0, 0)),
                      pl.BlockSpec(memory_space=pl.ANY)],
            out_specs=pl.BlockSpec((tb, d), lambda i, dr: (i, 0)),
            scratch_shapes=[pltpu.VMEM((tb, d), F32), pltpu.VMEM((tb, d), F32), pltpu.SemaphoreType.DMA],
        ),
        compiler_params=_cp("arbitrary"),
        name="moe_combine",
    )(dest_flat, x, wts_tok, g_final.reshape(1, d), ys)


def hier_moe_final(x2, g_ffn, w_rg, b_rg, w_re, w_gate, w_up, w_down, g_final):
    t, d = x2.shape
    xn, eid, wts = moe_route(x2, g_ffn, w_rg, b_rg, w_re)
    n = EXPERT_TOPK * t
    e_flat = eid.reshape(1, n)
    rank, counts_f = moe_rank(e_flat)
    counts = counts_f.reshape(N_EXPERTS).astype(jnp.int32)
    padded = (counts + MOE_BLOCK - 1) // MOE_BLOCK * MOE_BLOCK
    pad_end = jnp.cumsum(padded)
    pad_start = pad_end - padded
    dest = (jnp.take(pad_start, e_flat[0]) + rank[0]).astype(jnp.int32)
    nblk = -(-n // MOE_BLOCK) + N_EXPERTS
    blk_start = jnp.arange(nblk, dtype=jnp.int32) * MOE_BLOCK
    blk_expert = jnp.minimum(jnp.sum(pad_end[None, :] <= blk_start[:, None], axis=1), N_EXPERTS - 1).astype(jnp.int32)
    n_used = (pad_end[-1:] // MOE_BLOCK).astype(jnp.int32)
    xs = moe_dispatch(xn, dest, counts, pad_start.astype(jnp.int32), nblk * MOE_BLOCK)
    ys = moe_experts(xs, blk_expert, n_used, w_gate.astype(BF16), w_up.astype(BF16), w_down.astype(BF16))
    return moe_combine(x2, ys, dest, wts.T, g_final)


def kernel(x, mem, g_mix, w_in, b_gate, gdn_conv, gdn_a_log, gdn_dt_bias, gdn_norm_g, g_mem, w_mem_kv, w_o_swa, w_o_gdn, w_o_mem, w_out, g_ffn, w_route_group, b_route_group, w_route_expert, w_expert_gate, w_expert_up, w_expert_down, g_final):
    batch, seq, d = x.shape
    depth = w_in.shape[0]
    assert depth == 1, "the final RMSNorm is fused into the layer's MoE combine"
    t = batch * seq
    mem_tokens = mem.shape[1]
    h = x.reshape(t, d)
    o0 = 3 * SWA_WIDTH
    o1 = o0 + 3 * GDN_WIDTH
    o2 = o1 + GDN_WIDTH
    o3 = o2 + 4 * GDN_HEADS
    o4 = o3 + MEM_WIDTH
    tables = rope_tables(seq)
    for l in range(depth):
        w = w_in[l].astype(BF16)
        a = rmsnorm_rows(h, g_mix[l], BF16)
        qkv_a = matmul(a, w, F32, n=o0, rope=(tables, seq))
        o_a = dilated_swa(qkv_a, batch, seq)
        qkv_b = matmul_conv_norm(a, w, o0, gdn_conv[l], seq)
        z_b = matmul(a, w, BF16, n=GDN_WIDTH, col_off=o1)
        w_ba = jnp.zeros((d, LANES), BF16).at[:, :o3 - o2].set(w[:, o2:o3])
        beta, gc = gdn_gates(matmul(a, w_ba, F32), gdn_a_log[l], gdn_dt_bias[l])
        o_f, o_r = gdn_scan(qkv_b, beta, gc, batch, seq)
        o_b = gdn_post(o_f, o_r, z_b, gdn_norm_g[l])
        kv = matmul(rmsnorm_rows(mem.reshape(batch * mem_tokens, d), g_mem[l], BF16), w_mem_kv[l].astype(BF16), BF16)
        o_m = mem_attention(matmul(a, w[:, o3:o4], BF16), kv, batch, seq, mem_tokens)
        gates = matmul(a, w[:, o4:], BF16, bias=b_gate[l].reshape(-1))
        mix = gated_merge(o_a, o_b, o_m, w_o_swa[l].astype(BF16), w_o_gdn[l].astype(BF16),
                          w_o_mem[l].astype(BF16), gates)
        x2 = matmul(mix, w_out[l].astype(BF16), F32, residual=h)
        h = hier_moe_final(x2, g_ffn[l], w_route_group[l], b_route_group[l], w_route_expert[l],
                           w_expert_gate[l], w_expert_up[l], w_expert_down[l], g_final)
    return h.reshape(batch, seq, d)
```

```python
import functools

import jax
import jax.numpy as jnp
from jax import lax
from jax.experimental import pallas as pl
from jax.experimental.pallas import tpu as pltpu

F32 = jnp.float32
BF16 = jnp.bfloat16

HEAD_DIM = 128
SWA_GROUPS = ((128, 1), (512, 4), (2048, 16))
SWA_HEADS_PER_GROUP = 4
SWA_HEADS = SWA_HEADS_PER_GROUP * len(SWA_GROUPS)
SWA_WIDTH = SWA_HEADS * HEAD_DIM
SWA_BLOCK = 64
SWA_UNROLL = 4
ROPE_THETA = 500000.0
ROPE_DIMS = HEAD_DIM // 4
GDN_HEADS = 12
GDN_WIDTH = GDN_HEADS * HEAD_DIM
GDN_CONV = 5
GDN_CHUNK = 64
MEM_HEADS = 4
MEM_HEAD_DIM = 256
MEM_WIDTH = MEM_HEADS * MEM_HEAD_DIM
N_BRANCH = 3
N_GROUPS = 4
EXPERTS_PER_GROUP = 8
N_EXPERTS = N_GROUPS * EXPERTS_PER_GROUP
EXPERT_TOPK = 2
MOE_BLOCK = 256
EPS = 1e-6
NEG_INF = -1e30

LANES = 128
SUBLANES = 8
VMEM_LIMIT = 48 * 1024 * 1024


def _cp(*sem, vmem=VMEM_LIMIT):
    return pltpu.CompilerParams(dimension_semantics=sem, vmem_limit_bytes=vmem)


def _tile(n, pref):
    t = min(n, pref)
    while n % t:
        t //= 2
    return t


def _rmsnorm_kernel(x_ref, g_ref, o_ref):
    x = x_ref[...].astype(F32)
    r = lax.rsqrt(jnp.mean(x * x, axis=-1, keepdims=True) + EPS)
    o_ref[...] = (x * r * g_ref[...]).astype(o_ref.dtype)


def rmsnorm_rows(x, g, out_dtype):
    m, d = x.shape
    tm = _tile(m, 512)
    return pl.pallas_call(
        _rmsnorm_kernel,
        out_shape=jax.ShapeDtypeStruct((m, d), out_dtype),
        grid=(m // tm,),
        in_specs=[pl.BlockSpec((tm, d), lambda i: (i, 0)), pl.BlockSpec((1, d), lambda i: (0, 0))],
        out_specs=pl.BlockSpec((tm, d), lambda i: (i, 0)),
        compiler_params=_cp("parallel"),
        name="rmsnorm_rows",
    )(x, g.reshape(1, d))


def _mm_kernel(a_ref, w_ref, o_ref):
    o_ref[...] = jnp.dot(a_ref[...], w_ref[...], preferred_element_type=F32).astype(o_ref.dtype)


def _mm_sigmoid_kernel(a_ref, w_ref, b_ref, o_ref):
    z = jnp.dot(a_ref[...], w_ref[...], preferred_element_type=F32) + b_ref[...]
    o_ref[...] = jax.nn.sigmoid(z).astype(o_ref.dtype)


def _mm_residual_kernel(a_ref, w_ref, r_ref, o_ref):
    o_ref[...] = r_ref[...] + jnp.dot(a_ref[...], w_ref[...], preferred_element_type=F32)


EPI_COLS = 2 * HEAD_DIM


def _mm_rope_kernel(a_ref, w_ref, c_ref, s1_ref, s2_ref, o_ref):
    c, s1, s2 = c_ref[...], s1_ref[...], s2_ref[...]
    half = ROPE_DIMS // 2
    for c0 in range(0, o_ref.shape[1], EPI_COLS):
        acc = jnp.dot(a_ref[...], w_ref[:, c0:c0 + EPI_COLS], preferred_element_type=F32)
        for h0 in range(0, EPI_COLS, HEAD_DIM):
            xh = acc[:, h0:h0 + HEAD_DIM]
            o_ref[:, c0 + h0:c0 + h0 + HEAD_DIM] = (
                xh * c + pltpu.roll(xh, HEAD_DIM - half, 1) * s1 + pltpu.roll(xh, half, 1) * s2)


MM_TM = 1024
MM_TN = (1536, 1024, 512)


def _col_tile(n):
    return next((t for t in MM_TN if n % t == 0), n)


def matmul(a, w, out_dtype, *, n=None, col_off=0, bias=None, residual=None, rope=None):
    m, k = a.shape
    n = w.shape[1] if n is None else n
    tm, tn = _tile(m, MM_TM), _col_tile(n)
    assert col_off % tn == 0
    joff = col_off // tn
    a_spec = pl.BlockSpec((tm, k), lambda i, j: (i, 0))
    w_spec = pl.BlockSpec((k, tn), lambda i, j: (0, joff + j))
    o_spec = pl.BlockSpec((tm, tn), lambda i, j: (i, j))
    if bias is not None:
        kern, extra, extra_specs = _mm_sigmoid_kernel, (bias.reshape(1, n),), [pl.BlockSpec((1, tn), lambda i, j: (0, j))]
    elif residual is not None:
        kern, extra, extra_specs = _mm_residual_kernel, (residual,), [o_spec]
    elif rope is not None:
        tables, seq = rope
        assert tn == SWA_WIDTH and n == 3 * SWA_WIDTH
        per = seq // tm
        t_spec = pl.BlockSpec((None, tm, HEAD_DIM), lambda i, j: (j, i % per, 0))
        kern = _mm_rope_kernel
        extra, extra_specs = tuple(tables), [t_spec] * 3
    else:
        kern, extra, extra_specs = _mm_kernel, (), []
    return pl.pallas_call(
        kern,
        out_shape=jax.ShapeDtypeStruct((m, n), out_dtype),
        grid=(m // tm, n // tn),
        in_specs=[a_spec, w_spec] + extra_specs,
        out_specs=o_spec,
        compiler_params=_cp("parallel", "arbitrary"),
        name="matmul",
    )(a, w, *extra)


def rope_tables(seq):
    half = ROPE_DIMS // 2
    inv = ROPE_THETA ** (-jnp.arange(half, dtype=F32) / half)
    ang = jnp.arange(seq, dtype=F32)[:, None] * inv[None, :]
    cos, sin = jnp.cos(ang), jnp.sin(ang)
    zeros = jnp.zeros((seq, HEAD_DIM - ROPE_DIMS), F32)
    zh = jnp.zeros((seq, half), F32)
    c = jnp.concatenate([cos, cos, zeros + 1.0], axis=1)
    s1 = jnp.concatenate([-sin, zh, zeros], axis=1)
    s2 = jnp.concatenate([zh, sin, zeros], axis=1)
    ident = (jnp.ones_like(c), jnp.zeros_like(c), jnp.zeros_like(c))
    return tuple(jnp.stack([tab * (HEAD_DIM ** -0.5), tab, idt]) for tab, idt in zip((c, s1, s2), ident))


CONV_HALO = 16


def _mm_conv_kernel(a_ref, ap_ref, an_ref, w_ref, cw_ref, o_ref, buf_ref, *, tm, tiles_per_seq):
    i, kind = pl.program_id(0), pl.program_id(1)
    halo = CONV_HALO
    rows = tm + 2 * halo
    keep_prev = jnp.where((i % tiles_per_seq) == 0, 0.0, 1.0)
    keep_next = jnp.where((i % tiles_per_seq) == tiles_per_seq - 1, 0.0, 1.0)
    qk_mul = jnp.where(kind == 0, HEAD_DIM ** -0.5, 1.0)
    w = w_ref[...]
    buf_ref[0:halo, :] = jnp.dot(ap_ref[...], w, preferred_element_type=F32) * keep_prev
    buf_ref[halo:halo + tm, :] = jnp.dot(a_ref[...], w, preferred_element_type=F32)
    buf_ref[halo + tm:, :] = jnp.dot(an_ref[...], w, preferred_element_type=F32) * keep_next
    for h in range(GDN_HEADS):
        cols = slice(h * HEAD_DIM, (h + 1) * HEAD_DIM)
        xh = buf_ref[:, cols]
        acc = xh * cw_ref[GDN_CONV // 2:GDN_CONV // 2 + 1, cols]
        for t in range(GDN_CONV):
            if t != GDN_CONV // 2:
                acc = acc + pltpu.roll(xh, (GDN_CONV // 2 - t) % rows, 0) * cw_ref[t:t + 1, cols]
        acc = acc[halo:halo + tm, :]
        y = acc * jax.nn.sigmoid(acc)
        inv = lax.rsqrt(jnp.sum(y * y, axis=-1, keepdims=True) + EPS) * qk_mul
        o_ref[h] = (y * jnp.where(kind < 2, inv, 1.0)).astype(o_ref.dtype)


def matmul_conv_norm(a, w, col_off, conv_w, seq):
    m, k = a.shape
    tn = GDN_WIDTH
    tm = _tile(seq, MM_TM)
    assert col_off % tn == 0 and tm % CONV_HALO == 0
    joff = col_off // tn
    per_h = tm // CONV_HALO
    nblk_h = m // CONV_HALO
    return pl.pallas_call(
        functools.partial(_mm_conv_kernel, tm=tm, tiles_per_seq=seq // tm),
        out_shape=jax.ShapeDtypeStruct((3 * GDN_HEADS, m, HEAD_DIM), BF16),
        grid=(m // tm, 3),
        in_specs=[
            pl.BlockSpec((tm, k), lambda i, j: (i, 0)),
            pl.BlockSpec((CONV_HALO, k), lambda i, j: (jnp.maximum(i * per_h - 1, 0), 0)),
            pl.BlockSpec((CONV_HALO, k), lambda i, j: (jnp.minimum((i + 1) * per_h, nblk_h - 1), 0)),
            pl.BlockSpec((k, tn), lambda i, j: (0, joff + j)),
            pl.BlockSpec((GDN_CONV, tn), lambda i, j: (0, j)),
        ],
        out_specs=pl.BlockSpec((GDN_HEADS, tm, HEAD_DIM), lambda i, j: (j, i, 0)),
        scratch_shapes=[pltpu.VMEM((tm + 2 * CONV_HALO, tn), F32)],
        compiler_params=_cp("parallel", "arbitrary"),
        name="matmul_conv_norm",
    )(a, a, a, w, conv_w)


def _swa_kernel(q_ref, k_ref, v_ref, o_ref, acc_ref, m_ref, l_ref, *, seq):
    g = pl.program_id(2)

    @pl.when(g == 0)
    def _():
        acc_ref[...] = jnp.zeros_like(acc_ref)
        m_ref[...] = jnp.full_like(m_ref, NEG_INF)
        l_ref[...] = jnp.zeros_like(l_ref)

    for gi, (window, dil) in enumerate(SWA_GROUPS):
        radius = window // (2 * dil)
        assert radius <= SWA_BLOCK
        sub = seq // dil
        qb = min(128, sub)
        win = min(sub, qb + 2 * SWA_BLOCK)
        nqb = sub // qb

        @pl.when(g == gi)
        def _(dil=dil, radius=radius, sub=sub, qb=qb, win=win, nqb=nqb):
            def rows(start, size):
                return pl.ds(start, size) if dil == 1 else pl.ds(start, size, stride=dil)

            total = dil * nqb
            unroll = SWA_UNROLL if total % SWA_UNROLL == 0 else 1
            idx = range(unroll)

            def body(step, carry):
                its = [step * unroll + u for u in idx]
                qss = [(it % nqb) * qb for it in its]
                wss = [jnp.clip(qs - SWA_BLOCK, 0, sub - win) for qs in qss]
                q_rows = [rows(it // nqb + dil * qs, qb) for it, qs in zip(its, qss)]
                k_rows = [rows(it // nqb + dil * ws, win) for it, ws in zip(its, wss)]
                q = [q_ref[r, :].astype(BF16) for r in q_rows]
                k = [k_ref[r, :].astype(BF16) for r in k_rows]
                v = [v_ref[r, :].astype(BF16) for r in k_rows]
                s = [_nt(q[u], k[u]) for u in idx]
                off = lax.broadcasted_iota(jnp.int32, (qb, win), 0) - lax.broadcasted_iota(jnp.int32, (qb, win), 1)
                s = [jnp.where(jnp.abs(off + (qss[u] - wss[u])) <= radius, s[u], NEG_INF) for u in idx]
                m_old = [m_ref[r, :] for r in q_rows]
                m_new = [jnp.maximum(m_old[u], jnp.max(s[u], axis=1, keepdims=True)) for u in idx]
                p = [jnp.exp(s[u] - m_new[u]) for u in idx]
                corr = [jnp.exp(m_old[u] - m_new[u]) for u in idx]
                pv = [jnp.dot(p[u].astype(BF16), v[u], preferred_element_type=F32) for u in idx]
                for u in idx:
                    l_ref[q_rows[u], :] = l_ref[q_rows[u], :] * corr[u] + jnp.sum(p[u], axis=1, keepdims=True)
                    acc_ref[q_rows[u], :] = acc_ref[q_rows[u], :] * corr[u] + pv[u]
                    m_ref[q_rows[u], :] = m_new[u]
                return carry

            lax.fori_loop(0, total // unroll, body, 0)

    @pl.when(g == len(SWA_GROUPS) - 1)
    def _():
        o_ref[...] = (acc_ref[...] / l_ref[...]).astype(o_ref.dtype)


def dilated_swa(qkv, batch, seq):
    nh = SWA_HEADS_PER_GROUP
    qkv3 = qkv.reshape(batch, seq, 3 * SWA_WIDTH)

    def spec(off):
        return pl.BlockSpec((None, seq, HEAD_DIM), lambda b, h, g: (b, 0, off + g * nh + h))

    out = pl.pallas_call(
        functools.partial(_swa_kernel, seq=seq),
        out_shape=jax.ShapeDtypeStruct((batch, seq, nh * HEAD_DIM), BF16),
        grid=(batch, nh, len(SWA_GROUPS)),
        in_specs=[spec(0), spec(SWA_HEADS), spec(2 * SWA_HEADS)],
        out_specs=pl.BlockSpec((None, seq, HEAD_DIM), lambda b, h, g: (b, 0, h)),
        scratch_shapes=[pltpu.VMEM((seq, HEAD_DIM), F32), pltpu.VMEM((seq, 1), F32), pltpu.VMEM((seq, 1), F32)],
        compiler_params=_cp("parallel", "parallel", "arbitrary"),
        name="dilated_swa",
    )(qkv3, qkv3, qkv3)
    return out.reshape(batch * seq, nh * HEAD_DIM)


def _gdn_gate_kernel(ba_ref, alog_ref, dtb_ref, beta_ref, gc_ref, *, ts):
    x = ba_ref[...]
    nhd = 2 * GDN_HEADS
    lane = lax.broadcasted_iota(jnp.int32, (ts, LANES), 1)
    row = lax.broadcasted_iota(jnp.int32, (ts, LANES), 0) % GDN_CHUNK
    beta_ref[...] = jax.nn.sigmoid(x)
    z = x + dtb_ref[...]
    softplus = jnp.maximum(z, 0.0) + jnp.log(1.0 + jnp.exp(-jnp.abs(z)))
    g = jnp.where(jnp.logical_and(lane >= nhd, lane < 2 * nhd), -jnp.exp(alog_ref[...]) * softplus, 0.0)
    pre, suf = g, g
    s = 1
    while s < GDN_CHUNK:
        pre = pre + jnp.where(row >= s, pltpu.roll(pre, s, 0), 0.0)
        suf = suf + jnp.where(row < GDN_CHUNK - s, pltpu.roll(suf, ts - s, 0), 0.0)
        s *= 2
    gc_ref[...] = jnp.where(lane < nhd + GDN_HEADS, pre, suf)


def gdn_gates(ba, a_log, dt_bias):
    t = ba.shape[0]
    ts = _tile(t, 1024)
    nhd = 2 * GDN_HEADS
    pad = lambda v: jnp.zeros((1, LANES), F32).at[0, nhd:2 * nhd].set(v.reshape(-1).astype(F32))
    shp = jax.ShapeDtypeStruct((t, LANES), F32)
    vec = pl.BlockSpec((1, LANES), lambda i: (0, 0))
    blk = pl.BlockSpec((ts, LANES), lambda i: (i, 0))
    return pl.pallas_call(
        functools.partial(_gdn_gate_kernel, ts=ts),
        out_shape=(shp, shp),
        grid=(t // ts,),
        in_specs=[blk, vec, vec],
        out_specs=(blk, blk),
        compiler_params=_cp("parallel"),
        name="gdn_gates",
    )(ba, pad(a_log), pad(dt_bias))


def _nt(a, b):
    return lax.dot_general(a, b, (((1,), (1,)), ((), ())), preferred_element_type=F32)


def _tn(a, b):
    return lax.dot_general(a, b, (((0,), (0,)), ((), ())), preferred_element_type=F32)


def _mmb(a, b):
    return jnp.dot(a.astype(BF16), b.astype(BF16), preferred_element_type=F32)


def _gdn_chunks(qs, ks, vs, bcols, gcols, states, revs):
    c = GDN_CHUNK
    idx = range(len(qs))
    ri = lax.broadcasted_iota(jnp.int32, (c, c), 0)
    ci = lax.broadcasted_iota(jnp.int32, (c, c), 1)
    strict = {False: ri > ci, True: ri < ci}
    incl = {False: ri >= ci, True: ri <= ci}
    eye = jnp.where(ri == ci, 1.0, 0.0)
    lane = lax.broadcasted_iota(jnp.int32, (c, LANES), 1)
    ones3 = jnp.where(lane < 3, 1.0, 0.0)
    ones3_hi = jnp.where(jnp.logical_and(lane >= 3, lane < 6), 1.0, 0.0)
    g_last = [gcols[i][0:1, :] if revs[i] else gcols[i][c - 1:c, :] for i in idx]
    egc = [jnp.exp(gcols[i]) for i in idx]
    q_state = [_mmb(qs[i] * egc[i], states[i]) for i in idx]
    diff = []
    for i in idx:
        g1 = gcols[i].astype(BF16).astype(F32)
        g2 = (gcols[i] - g1).astype(BF16).astype(F32)
        g3 = (gcols[i] - g1 - g2).astype(BF16).astype(F32)
        pieces = jnp.where(lane == 0, g1, jnp.where(lane == 1, g2, jnp.where(lane == 2, g3, 0.0)))
        xm = pieces + ones3_hi
        ym = ones3 - pltpu.roll(pieces, 3, 1)
        diff.append(_nt(xm.astype(BF16), ym.astype(BF16)))
    decay = [jnp.where(incl[revs[i]], jnp.exp(jnp.where(incl[revs[i]], diff[i], 0.0)), 0.0) for i in idx]
    kb = [ks[i] * bcols[i] for i in idx]
    k16 = [ks[i].astype(BF16) for i in idx]
    lm = [jnp.where(strict[revs[i]], _nt(kb[i].astype(BF16), k16[i]) * decay[i], 0.0) for i in idx]
    attn = [_nt(qs[i].astype(BF16), k16[i]) * decay[i] for i in idx]
    tinv = [eye - lm[i] for i in idx]
    pw = [_mmb(lm[i], lm[i]) for i in idx]
    n = 4
    while n < c:
        both = [_mmb(jnp.concatenate([tinv[i], pw[i]], axis=0), pw[i]) for i in idx]
        tinv = [tinv[i] + both[i][:c] for i in idx]
        pw = [both[i][c:] for i in idx]
        n *= 2
    tinv = [tinv[i] + _mmb(tinv[i], pw[i]) for i in idx]
    u = [_mmb(tinv[i], jnp.concatenate([vs[i] * bcols[i], kb[i] * egc[i]], axis=1)) for i in idx]
    v_new = [u[i][:, :HEAD_DIM] - _mmb(u[i][:, HEAD_DIM:], states[i]) for i in idx]
    outs = [q_state[i] + _mmb(attn[i], v_new[i]) for i in idx]
    k_e = [ks[i] * jnp.exp(g_last[i] - gcols[i]) for i in idx]
    new_states = [states[i] * jnp.exp(g_last[i]) + _tn(k_e[i].astype(BF16), v_new[i].astype(BF16)) for i in idx]
    return outs, new_states


def _gdn_kernel(qf_ref, kf_ref, vf_ref, bf_ref, gf_ref, qb_ref, kb_ref, vb_ref, bb_ref, gb_ref,
                of_ref, ob_ref, state_ref, *, chunks):
    @pl.when(pl.program_id(1) == 0)
    def _():
        state_ref[...] = jnp.zeros_like(state_ref)

    nh = GDN_HEADS

    def body(it, carry):
        rows_f = pl.ds(pl.multiple_of(it * GDN_CHUNK, GDN_CHUNK), GDN_CHUNK)
        rows_b = pl.ds(pl.multiple_of((chunks - 1 - it) * GDN_CHUNK, GDN_CHUNK), GDN_CHUNK)
        tabs = ((qf_ref, kf_ref, vf_ref, bf_ref[rows_f, :], gf_ref[rows_f, :], rows_f, 0),
                (qb_ref, kb_ref, vb_ref, bb_ref[rows_b, :], gb_ref[rows_b, :], rows_b, nh))
        qs, ks, vs, bcols, gcols, states, revs = [], [], [], [], [], [], []
        for d, (q_ref, k_ref, v_ref, bt, gt, rows, lane0) in enumerate(tabs):
            for h in range(nh):
                qs.append(q_ref[h, rows, :].astype(F32))
                ks.append(k_ref[h, rows, :].astype(F32))
                vs.append(v_ref[h, rows, :].astype(F32))
                bcols.append(bt[:, lane0 + h:lane0 + h + 1])
                gcols.append(gt[:, 2 * nh + lane0 + h:2 * nh + lane0 + h + 1])
                states.append(state_ref[d * nh + h])
                revs.append(d == 1)
        outs, new_states = _gdn_chunks(qs, ks, vs, bcols, gcols, states, revs)
        for d, o_ref in enumerate((of_ref, ob_ref)):
            for h in range(nh):
                state_ref[d * nh + h] = new_states[d * nh + h]
                o_ref[h, tabs[d][5], :] = outs[d * nh + h].astype(o_ref.dtype)
        return carry

    lax.fori_loop(0, chunks, body, 0)


def gdn_scan(qkv_h, beta, gc, batch, seq):
    t = batch * seq
    rows = _tile(seq, 512)
    nblk = seq // rows
    fwd = lambda b, ib: b * nblk + ib
    bwd = lambda b, ib: b * nblk + nblk - 1 - ib

    def specs(blk):
        qspec = lambda part: pl.BlockSpec((GDN_HEADS, rows, HEAD_DIM), lambda b, ib: (part, blk(b, ib), 0))
        gspec = pl.BlockSpec((rows, LANES), lambda b, ib: (blk(b, ib), 0))
        return [qspec(0), qspec(1), qspec(2), gspec, gspec]

    ospec = lambda blk: pl.BlockSpec((GDN_HEADS, rows, HEAD_DIM), lambda b, ib: (0, blk(b, ib), 0))
    out = jax.ShapeDtypeStruct((GDN_HEADS, t, HEAD_DIM), BF16)
    return pl.pallas_call(
        functools.partial(_gdn_kernel, chunks=rows // GDN_CHUNK),
        out_shape=(out, out),
        grid=(batch, nblk),
        in_specs=specs(fwd) + specs(bwd),
        out_specs=(ospec(fwd), ospec(bwd)),
        scratch_shapes=[pltpu.VMEM((2 * GDN_HEADS, HEAD_DIM, HEAD_DIM), F32)],
        compiler_params=_cp("parallel", "arbitrary"),
        name="gdn_scan",
    )(qkv_h, qkv_h, qkv_h, beta, gc, qkv_h, qkv_h, qkv_h, beta, gc)


def _gdn_post_kernel(of_ref, ob_ref, z_ref, g_ref, o_ref):
    o = of_ref[...].astype(F32) + ob_ref[...].astype(F32)
    r = lax.rsqrt(jnp.mean(o * o, axis=-1, keepdims=True) + EPS)
    z = z_ref[...].astype(F32)
    o_ref[...] = (o * r * g_ref[...] * (z * jax.nn.sigmoid(z))).astype(o_ref.dtype)


def gdn_post(o_f, o_b, z, norm_g):
    nhd, t, _ = o_f.shape
    ts = _tile(t, 1024)
    hspec = pl.BlockSpec((None, ts, HEAD_DIM), lambda i, h: (h, i, 0))
    tspec = pl.BlockSpec((ts, HEAD_DIM), lambda i, h: (i, h))
    return pl.pallas_call(
        _gdn_post_kernel,
        out_shape=jax.ShapeDtypeStruct((t, nhd * HEAD_DIM), BF16),
        grid=(t // ts, nhd),
        in_specs=[hspec, hspec, tspec, pl.BlockSpec((1, HEAD_DIM), lambda i, h: (0, 0))],
        out_specs=tspec,
        compiler_params=_cp("parallel", "arbitrary"),
        name="gdn_post",
    )(o_f, o_b, z, norm_g.reshape(1, HEAD_DIM).astype(F32))


def _mem_attn_kernel(q_ref, k_ref, v_ref, o_ref):
    scale = MEM_HEAD_DIM ** -0.5
    for h in range(MEM_HEADS):
        cols = slice(h * MEM_HEAD_DIM, (h + 1) * MEM_HEAD_DIM)
        s = _nt(q_ref[:, cols], k_ref[:, cols]) * scale
        p = jnp.exp(s - jnp.max(s, axis=-1, keepdims=True))
        p = p / jnp.sum(p, axis=-1, keepdims=True)
        o_ref[:, cols] = jnp.dot(p.astype(BF16), v_ref[:, cols], preferred_element_type=F32).astype(o_ref.dtype)


def mem_attention(mq, kv, batch, seq, mem_tokens):
    tq = _tile(seq, 512)
    per = seq // tq
    return pl.pallas_call(
        _mem_attn_kernel,
        out_shape=jax.ShapeDtypeStruct(mq.shape, BF16),
        grid=(batch * per,),
        in_specs=[
            pl.BlockSpec((tq, MEM_WIDTH), lambda i: (i, 0)),
            pl.BlockSpec((mem_tokens, MEM_WIDTH), lambda i: (i // per, 0)),
            pl.BlockSpec((mem_tokens, MEM_WIDTH), lambda i: (i // per, 1)),
        ],
        out_specs=pl.BlockSpec((tq, MEM_WIDTH), lambda i: (i, 0)),
        compiler_params=_cp("parallel"),
        name="mem_attention",
    )(mq, kv, kv)


def _merge_kernel(oa_ref, ob_ref, om_ref, wa_ref, wb_ref, wm_ref, g0_ref, g1_ref, g2_ref, o_ref):
    ya = jnp.dot(oa_ref[...], wa_ref[...], preferred_element_type=F32)
    yb = jnp.dot(ob_ref[...], wb_ref[...], preferred_element_type=F32)
    ym = jnp.dot(om_ref[...], wm_ref[...], preferred_element_type=F32)
    mix = g0_ref[...].astype(F32) * ya + g1_ref[...].astype(F32) * yb + g2_ref[...].astype(F32) * ym
    o_ref[...] = mix.astype(o_ref.dtype)


def gated_merge(o_a, o_b, o_m, w_a, w_b, w_m, gates):
    t = o_a.shape[0]
    d = w_a.shape[1]
    tm, tn = _tile(t, 1024), _tile(d, 512)
    nj = d // tn
    act = lambda o: pl.BlockSpec((tm, o.shape[1]), lambda i, j: (i, 0))
    wsp = lambda w: pl.BlockSpec((w.shape[0], tn), lambda i, j: (0, j))
    gsp = lambda br: pl.BlockSpec((tm, tn), lambda i, j: (i, br * nj + j))
    return pl.pallas_call(
        _merge_kernel,
        out_shape=jax.ShapeDtypeStruct((t, d), BF16),
        grid=(t // tm, nj),
        in_specs=[act(o_a), act(o_b), act(o_m), wsp(w_a), wsp(w_b), wsp(w_m), gsp(0), gsp(1), gsp(2)],
        out_specs=pl.BlockSpec((tm, tn), lambda i, j: (i, j)),
        compiler_params=_cp("parallel", "arbitrary"),
        name="gated_merge",
    )(o_a, o_b, o_m, w_a, w_b, w_m, gates, gates, gates)


MOE_ROWS_PER_STEP = 128
ROUTE_ROWS = SUBLANES + N_EXPERTS


def _route_kernel(x_ref, g_ref, w_ref, b_ref, xn_ref, eid_ref, wt_ref):
    x = x_ref[...]
    r = lax.rsqrt(jnp.mean(x * x, axis=-1, keepdims=True) + EPS)
    xn = x * r * g_ref[...]
    xn_ref[...] = xn
    tm = x.shape[0]
    lg = lax.dot_general(w_ref[...], xn, (((1,), (1,)), ((), ())), preferred_element_type=F32,
                         precision=lax.Precision.HIGHEST) + b_ref[...]
    gl = [lg[i:i + 1, :] for i in range(N_GROUPS)]
    gmax = functools.reduce(jnp.maximum, gl)
    grp = jnp.full((1, tm), N_GROUPS - 1, jnp.int32)
    for i in range(N_GROUPS - 2, -1, -1):
        grp = jnp.where(gl[i] == gmax, i, grp)
    p_grp = 1.0 / functools.reduce(jnp.add, [jnp.exp(v - gmax) for v in gl])
    sel = lg[SUBLANES + (N_GROUPS - 1) * EXPERTS_PER_GROUP:SUBLANES + N_GROUPS * EXPERTS_PER_GROUP, :]
    for i in range(N_GROUPS - 2, -1, -1):
        lo = SUBLANES + i * EXPERTS_PER_GROUP
        sel = jnp.where(grp == i, lg[lo:lo + EXPERTS_PER_GROUP, :], sel)
    rowi = lax.broadcasted_iota(jnp.int32, (EXPERTS_PER_GROUP, tm), 0)
    v1 = jnp.max(sel, axis=0, keepdims=True)
    i1 = jnp.min(jnp.where(sel == v1, rowi, EXPERTS_PER_GROUP), axis=0, keepdims=True)
    sel2 = jnp.where(rowi == i1, -jnp.inf, sel)
    v2 = jnp.max(sel2, axis=0, keepdims=True)
    i2 = jnp.min(jnp.where(sel2 == v2, rowi, EXPERTS_PER_GROUP), axis=0, keepdims=True)
    e2 = jnp.exp(v2 - v1)
    w1 = p_grp / (1.0 + e2)
    eid_ref[0:1, :] = grp * EXPERTS_PER_GROUP + i1
    eid_ref[1:2, :] = grp * EXPERTS_PER_GROUP + i2
    wt_ref[0:1, :] = w1
    wt_ref[1:2, :] = w1 * e2


def moe_route(x, g, w_grp, b_grp, w_exp):
    t, d = x.shape
    tm = _tile(t, 512)
    w = jnp.zeros((ROUTE_ROWS, d), F32).at[:N_GROUPS].set(w_grp.T).at[SUBLANES:].set(w_exp.T)
    b = jnp.zeros((ROUTE_ROWS, 1), F32).at[:N_GROUPS, 0].set(b_grp.astype(F32))
    return pl.pallas_call(
        _route_kernel,
        out_shape=(jax.ShapeDtypeStruct((t, d), F32), jax.ShapeDtypeStruct((EXPERT_TOPK, t), jnp.int32),
                   jax.ShapeDtypeStruct((EXPERT_TOPK, t), F32)),
        grid=(t // tm,),
        in_specs=[pl.BlockSpec((tm, d), lambda i: (i, 0)), pl.BlockSpec((1, d), lambda i: (0, 0)),
                  pl.BlockSpec((ROUTE_ROWS, d), lambda i: (0, 0)), pl.BlockSpec((ROUTE_ROWS, 1), lambda i: (0, 0))],
        out_specs=(pl.BlockSpec((tm, d), lambda i: (i, 0)), pl.BlockSpec((EXPERT_TOPK, tm), lambda i: (0, i)),
                   pl.BlockSpec((EXPERT_TOPK, tm), lambda i: (0, i))),
        compiler_params=_cp("parallel"),
        name="moe_route",
    )(x, g.reshape(1, d), w, b)


def _rank_kernel(e_ref, rank_ref, cnt_ref, *, tb):
    @pl.when(pl.program_id(0) == 0)
    def _():
        cnt_ref[...] = jnp.zeros_like(cnt_ref)

    e = e_ref[...]
    onehot = jnp.where(lax.broadcasted_iota(jnp.int32, (N_EXPERTS, tb), 0) == e, 1.0, 0.0)
    earlier = jnp.where(lax.broadcasted_iota(jnp.int32, (tb, tb), 0) < lax.broadcasted_iota(jnp.int32, (tb, tb), 1),
                        1.0, 0.0)
    before = _mmb(onehot, earlier) + cnt_ref[...]
    rank_ref[...] = jnp.sum(onehot * before, axis=0, keepdims=True).astype(jnp.int32)
    cnt_ref[...] = cnt_ref[...] + jnp.sum(onehot, axis=1, keepdims=True)


def moe_rank(e_flat):
    n = e_flat.shape[1]
    tb = _tile(n, 512)
    return pl.pallas_call(
        functools.partial(_rank_kernel, tb=tb),
        out_shape=(jax.ShapeDtypeStruct((1, n), jnp.int32), jax.ShapeDtypeStruct((N_EXPERTS, 1), F32)),
        grid=(n // tb,),
        in_specs=[pl.BlockSpec((1, tb), lambda i: (0, i))],
        out_specs=(pl.BlockSpec((1, tb), lambda i: (0, i)), pl.BlockSpec((N_EXPERTS, 1), lambda i: (0, 0))),
        compiler_params=_cp("arbitrary"),
        name="moe_rank",
    )(e_flat)


def _dispatch_kernel(dest_ref, cnt_ref, pstart_ref, x_ref, xs_ref, zero_ref, sem, *, tb, tokens, nblk):
    i = pl.program_id(0)

    def row_copy(src_ref, src_row, dst_row):
        return pltpu.make_async_copy(src_ref.at[pl.ds(src_row, 1)], xs_ref.at[pl.ds(dst_row, 1)], sem)

    def block_copy(blk):
        return pltpu.make_async_copy(zero_ref, xs_ref.at[pl.ds(blk * MOE_BLOCK, MOE_BLOCK)], sem)

    @pl.when(i == 0)
    def _():
        zero_ref[...] = jnp.zeros_like(zero_ref)
        last = N_EXPERTS - 1
        n_used = (pstart_ref[last] + cnt_ref[last] + MOE_BLOCK - 1) // MOE_BLOCK

        def fill_block(blk, carry):
            block_copy(blk).start()
            block_copy(blk).wait()
            return carry

        lax.fori_loop(n_used, nblk, fill_block, 0)

        def per_expert(e, carry):
            npad = (-cnt_ref[e]) % MOE_BLOCK
            base = pstart_ref[e] + cnt_ref[e]

            def fill(r, c2):
                row_copy(zero_ref, 0, base + r).start()
                return c2

            lax.fori_loop(0, npad, fill, 0)

            def drain(r, c2):
                row_copy(zero_ref, 0, base + r).wait()
                return c2

            lax.fori_loop(0, npad, drain, 0)
            return carry

        lax.fori_loop(0, N_EXPERTS, per_expert, 0)

    copies = [row_copy(x_ref, j, dest_ref[kk * tokens + i * tb + j]) for j in range(tb) for kk in range(EXPERT_TOPK)]
    for cp in copies:
        cp.start()
    for cp in copies:
        cp.wait()


def moe_dispatch(xn, dest_flat, counts, pad_start, n_slots):
    t, d = xn.shape
    tb = _tile(t, MOE_ROWS_PER_STEP)
    return pl.pallas_call(
        functools.partial(_dispatch_kernel, tb=tb, tokens=t, nblk=n_slots // MOE_BLOCK),
        out_shape=jax.ShapeDtypeStruct((n_slots, d), xn.dtype),
        grid_spec=pltpu.PrefetchScalarGridSpec(
            num_scalar_prefetch=3,
            grid=(t // tb,),
            in_specs=[pl.BlockSpec((tb, d), lambda i, *_: (i, 0))],
            out_specs=pl.BlockSpec(memory_space=pl.ANY),
            scratch_shapes=[pltpu.VMEM((MOE_BLOCK, d), xn.dtype), pltpu.SemaphoreType.DMA],
        ),
        compiler_params=pltpu.CompilerParams(dimension_semantics=("arbitrary",), has_side_effects=True,
                                             vmem_limit_bytes=VMEM_LIMIT),
        name="moe_dispatch",
    )(dest_flat, counts, pad_start, xn)


def _expert_kernel(be_ref, nused_ref, x_ref, wg_ref, wu_ref, wd_ref, y_ref):
    @pl.when(pl.program_id(0) < nused_ref[0])
    def _():
        x = x_ref[...].astype(BF16)
        hg = jnp.dot(x, wg_ref[...], preferred_element_type=F32)
        hu = jnp.dot(x, wu_ref[...], preferred_element_type=F32)
        hmid = (hg * jax.nn.sigmoid(hg) * hu).astype(BF16)
        y_ref[...] = jnp.dot(hmid, wd_ref[...], preferred_element_type=F32).astype(y_ref.dtype)

    @pl.when(pl.program_id(0) >= nused_ref[0])
    def _():
        y_ref[...] = jnp.zeros_like(y_ref)


def moe_experts(xs, blk_expert, n_used, w_gate, w_up, w_down):
    p, d = xs.shape
    de = w_gate.shape[2]
    nblk = p // MOE_BLOCK
    row = lambda i, be, nu: (jnp.minimum(i, nu[0] - 1), 0)
    wsel = lambda i, be, nu: (be[jnp.minimum(i, nu[0] - 1)], 0, 0)
    return pl.pallas_call(
        _expert_kernel,
        out_shape=jax.ShapeDtypeStruct((p, d), F32),
        grid_spec=pltpu.PrefetchScalarGridSpec(
            num_scalar_prefetch=2,
            grid=(nblk,),
            in_specs=[pl.BlockSpec((MOE_BLOCK, d), row), pl.BlockSpec((None, d, de), wsel),
                      pl.BlockSpec((None, d, de), wsel), pl.BlockSpec((None, de, d), wsel)],
            out_specs=pl.BlockSpec((MOE_BLOCK, d), lambda i, be, nu: (i, 0)),
        ),
        compiler_params=_cp("arbitrary"),
        name="moe_experts",
    )(blk_expert, n_used, xs, w_gate, w_up, w_down)


def _combine_kernel(dest_ref, x_ref, wt_ref, g_ref, ys_ref, o_ref, y0_ref, y1_ref, sem, *, tb, tokens):
    i = pl.program_id(0)
    bufs = (y0_ref, y1_ref)

    def row_copy(kk, j):
        return pltpu.make_async_copy(ys_ref.at[pl.ds(dest_ref[kk * tokens + i * tb + j], 1)],
                                     bufs[kk].at[pl.ds(j, 1)], sem)

    copies = [row_copy(kk, j) for j in range(tb) for kk in range(EXPERT_TOPK)]
    for cp in copies:
        cp.start()
    for cp in copies:
        cp.wait()
    h = x_ref[...] + wt_ref[:, 0:1] * y0_ref[...] + wt_ref[:, 1:2] * y1_ref[...]
    r = lax.rsqrt(jnp.mean(h * h, axis=-1, keepdims=True) + EPS)
    o_ref[...] = h * r * g_ref[...]


def moe_combine(x, ys, dest_flat, wts_tok, g_final):
    t, d = x.shape
    tb = _tile(t, MOE_ROWS_PER_STEP)
    return pl.pallas_call(
        functools.partial(_combine_kernel, tb=tb, tokens=t),
        out_shape=jax.ShapeDtypeStruct((t, d), F32),
        grid_spec=pltpu.PrefetchScalarGridSpec(
            num_scalar_prefetch=1,
            grid=(t // tb,),
            in_specs=[pl.BlockSpec((tb, d), lambda i, dr: (i, 0)),
                      pl.BlockSpec((tb, EXPERT_TOPK), lambda i, dr: (i, 0)),
                      pl.BlockSpec((1, d), lambda i, dr: (0, 0)),
                      pl.BlockSpec(memory_space=pl.ANY)],
            out_specs=pl.BlockSpec((tb, d), lambda i, dr: (i, 0)),
            scratch_shapes=[pltpu.VMEM((tb, d), F32), pltpu.VMEM((tb, d), F32), pltpu.SemaphoreType.DMA],
        ),
        compiler_params=_cp("arbitrary"),
        name="moe_combine",
    )(dest_flat, x, wts_tok, g_final.reshape(1, d), ys)


def hier_moe_final(x2, g_ffn, w_rg, b_rg, w_re, w_gate, w_up, w_down, g_final):
    t, d = x2.shape
    xn, eid, wts = moe_route(x2, g_ffn, w_rg, b_rg, w_re)
    n = EXPERT_TOPK * t
    e_flat = eid.reshape(1, n)
    rank, counts_f = moe_rank(e_flat)
    counts = counts_f.reshape(N_EXPERTS).astype(jnp.int32)
    padded = (counts + MOE_BLOCK - 1) // MOE_BLOCK * MOE_BLOCK
    pad_end = jnp.cumsum(padded)
    pad_start = pad_end - padded
    dest = (jnp.take(pad_start, e_flat[0]) + rank[0]).astype(jnp.int32)
    nblk = -(-n // MOE_BLOCK) + N_EXPERTS
    blk_start = jnp.arange(nblk, dtype=jnp.int32) * MOE_BLOCK
    blk_expert = jnp.minimum(jnp.sum(pad_end[None, :] <= blk_start[:, None], axis=1), N_EXPERTS - 1).astype(jnp.int32)
    n_used = (pad_end[-1:] // MOE_BLOCK).astype(jnp.int32)
    xs = moe_dispatch(xn, dest, counts, pad_start.astype(jnp.int32), nblk * MOE_BLOCK)
    ys = moe_experts(xs, blk_expert, n_used, w_gate.astype(BF16), w_up.astype(BF16), w_down.astype(BF16))
    return moe_combine(x2, ys, dest, wts.T, g_final)


def kernel(x, mem, g_mix, w_in, b_gate, gdn_conv, gdn_a_log, gdn_dt_bias, gdn_norm_g, g_mem, w_mem_kv, w_o_swa, w_o_gdn, w_o_mem, w_out, g_ffn, w_route_group, b_route_group, w_route_expert, w_expert_gate, w_expert_up, w_expert_down, g_final):
    batch, seq, d = x.shape
    depth = w_in.shape[0]
    assert depth == 1, "the final RMSNorm is fused into the layer's MoE combine"
    t = batch * seq
    mem_tokens = mem.shape[1]
    h = x.reshape(t, d)
    o0 = 3 * SWA_WIDTH
    o1 = o0 + 3 * GDN_WIDTH
    o2 = o1 + GDN_WIDTH
    o3 = o2 + 4 * GDN_HEADS
    o4 = o3 + MEM_WIDTH
    tables = rope_tables(seq)
    for l in range(depth):
        w = w_in[l].astype(BF16)
        a = rmsnorm_rows(h, g_mix[l], BF16)
        qkv_a = matmul(a, w, F32, n=o0, rope=(tables, seq))
        o_a = dilated_swa(qkv_a, batch, seq)
        qkv_b = matmul_conv_norm(a, w, o0, gdn_conv[l], seq)
        z_b = matmul(a, w, BF16, n=GDN_WIDTH, col_off=o1)
        w_ba = jnp.zeros((d, LANES), BF16).at[:, :o3 - o2].set(w[:, o2:o3])
        beta, gc = gdn_gates(matmul(a, w_ba, F32), gdn_a_log[l], gdn_dt_bias[l])
        o_f, o_r = gdn_scan(qkv_b, beta, gc, batch, seq)
        o_b = gdn_post(o_f, o_r, z_b, gdn_norm_g[l])
        kv = matmul(rmsnorm_rows(mem.reshape(batch * mem_tokens, d), g_mem[l], BF16), w_mem_kv[l].astype(BF16), BF16)
        o_m = mem_attention(matmul(a, w[:, o3:o4], BF16), kv, batch, seq, mem_tokens)
        gates = matmul(a, w[:, o4:], BF16, bias=b_gate[l].reshape(-1))
        mix = gated_merge(o_a, o_b, o_m, w_o_swa[l].astype(BF16), w_o_gdn[l].astype(BF16),
                          w_o_mem[l].astype(BF16), gates)
        x2 = matmul(mix, w_out[l].astype(BF16), F32, residual=h)
        h = hier_moe_final(x2, g_ffn[l], w_route_group[l], b_route_group[l], w_route_expert[l],
                           w_expert_gate[l], w_expert_up[l], w_expert_down[l], g_final)
    return h.reshape(batch, seq, d)
```

```python
import functools

import jax
import jax.numpy as jnp
from jax import lax
from jax.experimental import pallas as pl
from jax.experimental.pallas import tpu as pltpu

F32 = jnp.float32
BF16 = jnp.bfloat16

HEAD_DIM = 128
SWA_GROUPS = ((128, 1), (512, 4), (2048, 16))
SWA_HEADS_PER_GROUP = 4
SWA_HEADS = SWA_HEADS_PER_GROUP * len(SWA_GROUPS)
SWA_WIDTH = SWA_HEADS * HEAD_DIM
SWA_BLOCK = 64
SWA_UNROLL = 4
ROPE_THETA = 500000.0
ROPE_DIMS = HEAD_DIM // 4
GDN_HEADS = 12
GDN_WIDTH = GDN_HEADS * HEAD_DIM
GDN_CONV = 5
GDN_CHUNK = 64
MEM_HEADS = 4
MEM_HEAD_DIM = 256
MEM_WIDTH = MEM_HEADS * MEM_HEAD_DIM
N_BRANCH = 3
N_GROUPS = 4
EXPERTS_PER_GROUP = 8
N_EXPERTS = N_GROUPS * EXPERTS_PER_GROUP
EXPERT_TOPK = 2
MOE_BLOCK = 256
EPS = 1e-6
NEG_INF = -1e30

LANES = 128
SUBLANES = 8
VMEM_LIMIT = 48 * 1024 * 1024


def _cp(*sem, vmem=VMEM_LIMIT):
    return pltpu.CompilerParams(dimension_semantics=sem, vmem_limit_bytes=vmem)


def _tile(n, pref):
    t = min(n, pref)
    while n % t:
        t //= 2
    return t


def _rmsnorm_kernel(x_ref, g_ref, o_ref):
    x = x_ref[...].astype(F32)
    r = lax.rsqrt(jnp.mean(x * x, axis=-1, keepdims=True) + EPS)
    o_ref[...] = (x * r * g_ref[...]).astype(o_ref.dtype)


def rmsnorm_rows(x, g, out_dtype):
    m, d = x.shape
    tm = _tile(m, 512)
    return pl.pallas_call(
        _rmsnorm_kernel,
        out_shape=jax.ShapeDtypeStruct((m, d), out_dtype),
        grid=(m // tm,),
        in_specs=[pl.BlockSpec((tm, d), lambda i: (i, 0)), pl.BlockSpec((1, d), lambda i: (0, 0))],
        out_specs=pl.BlockSpec((tm, d), lambda i: (i, 0)),
        compiler_params=_cp("parallel"),
        name="rmsnorm_rows",
    )(x, g.reshape(1, d))


def _mm_kernel(a_ref, w_ref, o_ref):
    o_ref[...] = jnp.dot(a_ref[...], w_ref[...], preferred_element_type=F32).astype(o_ref.dtype)


def _mm_sigmoid_kernel(a_ref, w_ref, b_ref, o_ref):
    z = jnp.dot(a_ref[...], w_ref[...], preferred_element_type=F32) + b_ref[...]
    o_ref[...] = jax.nn.sigmoid(z).astype(o_ref.dtype)


def _mm_residual_kernel(a_ref, w_ref, r_ref, o_ref):
    o_ref[...] = r_ref[...] + jnp.dot(a_ref[...], w_ref[...], preferred_element_type=F32)


EPI_COLS = 2 * HEAD_DIM


def _mm_rope_kernel(a_ref, w_ref, c_ref, s1_ref, s2_ref, o_ref):
    c, s1, s2 = c_ref[...], s1_ref[...], s2_ref[...]
    half = ROPE_DIMS // 2
    for c0 in range(0, o_ref.shape[1], EPI_COLS):
        acc = jnp.dot(a_ref[...], w_ref[:, c0:c0 + EPI_COLS], preferred_element_type=F32)
        for h0 in range(0, EPI_COLS, HEAD_DIM):
            xh = acc[:, h0:h0 + HEAD_DIM]
            o_ref[:, c0 + h0:c0 + h0 + HEAD_DIM] = (
                xh * c + pltpu.roll(xh, HEAD_DIM - half, 1) * s1 + pltpu.roll(xh, half, 1) * s2)


MM_TM = 1024
MM_TN = (1536, 1024, 512)


def _col_tile(n):
    return next((t for t in MM_TN if n % t == 0), n)


def matmul(a, w, out_dtype, *, n=None, col_off=0, bias=None, residual=None, rope=None):
    m, k = a.shape
    n = w.shape[1] if n is None else n
    tm, tn = _tile(m, MM_TM), _col_tile(n)
    assert col_off % tn == 0
    joff = col_off // tn
    a_spec = pl.BlockSpec((tm, k), lambda i, j: (i, 0))
    w_spec = pl.BlockSpec((k, tn), lambda i, j: (0, joff + j))
    o_spec = pl.BlockSpec((tm, tn), lambda i, j: (i, j))
    if bias is not None:
        kern, extra, extra_specs = _mm_sigmoid_kernel, (bias.reshape(1, n),), [pl.BlockSpec((1, tn), lambda i, j: (0, j))]
    elif residual is not None:
        kern, extra, extra_specs = _mm_residual_kernel, (residual,), [o_spec]
    elif rope is not None:
        tables, seq = rope
        assert tn == SWA_WIDTH and n == 3 * SWA_WIDTH
        per = seq // tm
        t_spec = pl.BlockSpec((None, tm, HEAD_DIM), lambda i, j: (j, i % per, 0))
        kern = _mm_rope_kernel
        extra, extra_specs = tuple(tables), [t_spec] * 3
    else:
        kern, extra, extra_specs = _mm_kernel, (), []
    return pl.pallas_call(
        kern,
        out_shape=jax.ShapeDtypeStruct((m, n), out_dtype),
        grid=(m // tm, n // tn),
        in_specs=[a_spec, w_spec] + extra_specs,
        out_specs=o_spec,
        compiler_params=_cp("parallel", "arbitrary"),
        name="matmul",
    )(a, w, *extra)


def rope_tables(seq):
    half = ROPE_DIMS // 2
    inv = ROPE_THETA ** (-jnp.arange(half, dtype=F32) / half)
    ang = jnp.arange(seq, dtype=F32)[:, None] * inv[None, :]
    cos, sin = jnp.cos(ang), jnp.sin(ang)
    zeros = jnp.zeros((seq, HEAD_DIM - ROPE_DIMS), F32)
    zh = jnp.zeros((seq, half), F32)
    c = jnp.concatenate([cos, cos, zeros + 1.0], axis=1)
    s1 = jnp.concatenate([-sin, zh, zeros], axis=1)
    s2 = jnp.concatenate([zh, sin, zeros], axis=1)
    ident = (jnp.ones_like(c), jnp.zeros_like(c), jnp.zeros_like(c))
    return tuple(jnp.stack([tab * (HEAD_DIM ** -0.5), tab, idt]) for tab, idt in zip((c, s1, s2), ident))


CONV_HALO = 16


def _mm_conv_kernel(a_ref, ap_ref, an_ref, w_ref, cw_ref, o_ref, buf_ref, *, tm, tiles_per_seq):
    i, kind = pl.program_id(0), pl.program_id(1)
    halo = CONV_HALO
    rows = tm + 2 * halo
    keep_prev = jnp.where((i % tiles_per_seq) == 0, 0.0, 1.0)
    keep_next = jnp.where((i % tiles_per_seq) == tiles_per_seq - 1, 0.0, 1.0)
    qk_mul = jnp.where(kind == 0, HEAD_DIM ** -0.5, 1.0)
    w = w_ref[...]
    buf_ref[0:halo, :] = jnp.dot(ap_ref[...], w, preferred_element_type=F32) * keep_prev
    buf_ref[halo:halo + tm, :] = jnp.dot(a_ref[...], w, preferred_element_type=F32)
    buf_ref[halo + tm:, :] = jnp.dot(an_ref[...], w, preferred_element_type=F32) * keep_next
    for h in range(GDN_HEADS):
        cols = slice(h * HEAD_DIM, (h + 1) * HEAD_DIM)
        xh = buf_ref[:, cols]
        acc = xh * cw_ref[GDN_CONV // 2:GDN_CONV // 2 + 1, cols]
        for t in range(GDN_CONV):
            if t != GDN_CONV // 2:
                acc = acc + pltpu.roll(xh, (GDN_CONV // 2 - t) % rows, 0) * cw_ref[t:t + 1, cols]
        acc = acc[halo:halo + tm, :]
        y = acc * jax.nn.sigmoid(acc)
        inv = lax.rsqrt(jnp.sum(y * y, axis=-1, keepdims=True) + EPS) * qk_mul
        o_ref[h] = (y * jnp.where(kind < 2, inv, 1.0)).astype(o_ref.dtype)


def matmul_conv_norm(a, w, col_off, conv_w, seq):
    m, k = a.shape
    tn = GDN_WIDTH
    tm = _tile(seq, MM_TM)
    assert col_off % tn == 0 and tm % CONV_HALO == 0
    joff = col_off // tn
    per_h = tm // CONV_HALO
    nblk_h = m // CONV_HALO
    return pl.pallas_call(
        functools.partial(_mm_conv_kernel, tm=tm, tiles_per_seq=seq // tm),
        out_shape=jax.ShapeDtypeStruct((3 * GDN_HEADS, m, HEAD_DIM), BF16),
        grid=(m // tm, 3),
        in_specs=[
            pl.BlockSpec((tm, k), lambda i, j: (i, 0)),
            pl.BlockSpec((CONV_HALO, k), lambda i, j: (jnp.maximum(i * per_h - 1, 0), 0)),
            pl.BlockSpec((CONV_HALO, k), lambda i, j: (jnp.minimum((i + 1) * per_h, nblk_h - 1), 0)),
            pl.BlockSpec((k, tn), lambda i, j: (0, joff + j)),
            pl.BlockSpec((GDN_CONV, tn), lambda i, j: (0, j)),
        ],
        out_specs=pl.BlockSpec((GDN_HEADS, tm, HEAD_DIM), lambda i, j: (j, i, 0)),
        scratch_shapes=[pltpu.VMEM((tm + 2 * CONV_HALO, tn), F32)],
        compiler_params=_cp("parallel", "arbitrary"),
        name="matmul_conv_norm",
    )(a, a, a, w, conv_w)


def _swa_kernel(q_ref, k_ref, v_ref, o_ref, acc_ref, m_ref, l_ref, *, seq):
    g = pl.program_id(2)

    @pl.when(g == 0)
    def _():
        acc_ref[...] = jnp.zeros_like(acc_ref)
        m_ref[...] = jnp.full_like(m_ref, NEG_INF)
        l_ref[...] = jnp.zeros_like(l_ref)

    for gi, (window, dil) in enumerate(SWA_GROUPS):
        radius = window // (2 * dil)
        assert radius <= SWA_BLOCK
        sub = seq // dil
        qb = min(128, sub)
        win = min(sub, qb + 2 * SWA_BLOCK)
        nqb = sub // qb

        @pl.when(g == gi)
        def _(dil=dil, radius=radius, sub=sub, qb=qb, win=win, nqb=nqb):
            def rows(start, size):
                return pl.ds(start, size) if dil == 1 else pl.ds(start, size, stride=dil)

            total = dil * nqb
            unroll = SWA_UNROLL if total % SWA_UNROLL == 0 else 1
            idx = range(unroll)

            def body(step, carry):
                its = [step * unroll + u for u in idx]
                qss = [(it % nqb) * qb for it in its]
                wss = [jnp.clip(qs - SWA_BLOCK, 0, sub - win) for qs in qss]
                q_rows = [rows(it // nqb + dil * qs, qb) for it, qs in zip(its, qss)]
                k_rows = [rows(it // nqb + dil * ws, win) for it, ws in zip(its, wss)]
                q = [q_ref[r, :].astype(BF16) for r in q_rows]
                k = [k_ref[r, :].astype(BF16) for r in k_rows]
                v = [v_ref[r, :].astype(BF16) for r in k_rows]
                s = [_nt(q[u], k[u]) for u in idx]
                off = lax.broadcasted_iota(jnp.int32, (qb, win), 0) - lax.broadcasted_iota(jnp.int32, (qb, win), 1)
                s = [jnp.where(jnp.abs(off + (qss[u] - wss[u])) <= radius, s[u], NEG_INF) for u in idx]
                m_old = [m_ref[r, :] for r in q_rows]
                m_new = [jnp.maximum(m_old[u], jnp.max(s[u], axis=1, keepdims=True)) for u in idx]
                p = [jnp.exp(s[u] - m_new[u]) for u in idx]
                corr = [jnp.exp(m_old[u] - m_new[u]) for u in idx]
                pv = [jnp.dot(p[u].astype(BF16), v[u], preferred_element_type=F32) for u in idx]
                for u in idx:
                    l_ref[q_rows[u], :] = l_ref[q_rows[u], :] * corr[u] + jnp.sum(p[u], axis=1, keepdims=True)
                    acc_ref[q_rows[u], :] = acc_ref[q_rows[u], :] * corr[u] + pv[u]
                    m_ref[q_rows[u], :] = m_new[u]
                return carry

            lax.fori_loop(0, total // unroll, body, 0)

    @pl.when(g == len(SWA_GROUPS) - 1)
    def _():
        o_ref[...] = (acc_ref[...] / l_ref[...]).astype(o_ref.dtype)


def dilated_swa(qkv, batch, seq):
    nh = SWA_HEADS_PER_GROUP
    qkv3 = qkv.reshape(batch, seq, 3 * SWA_WIDTH)

    def spec(off):
        return pl.BlockSpec((None, seq, HEAD_DIM), lambda b, h, g: (b, 0, off + g * nh + h))

    out = pl.pallas_call(
        functools.partial(_swa_kernel, seq=seq),
        out_shape=jax.ShapeDtypeStruct((batch, seq, nh * HEAD_DIM), BF16),
        grid=(batch, nh, len(SWA_GROUPS)),
        in_specs=[spec(0), spec(SWA_HEADS), spec(2 * SWA_HEADS)],
        out_specs=pl.BlockSpec((None, seq, HEAD_DIM), lambda b, h, g: (b, 0, h)),
        scratch_shapes=[pltpu.VMEM((seq, HEAD_DIM), F32), pltpu.VMEM((seq, 1), F32), pltpu.VMEM((seq, 1), F32)],
        compiler_params=_cp("parallel", "parallel", "arbitrary"),
        name="dilated_swa",
    )(qkv3, qkv3, qkv3)
    return out.reshape(batch * seq, nh * HEAD_DIM)


def _gdn_gate_kernel(ba_ref, alog_ref, dtb_ref, beta_ref, gc_ref, *, ts):
    x = ba_ref[...]
    nhd = 2 * GDN_HEADS
    lane = lax.broadcasted_iota(jnp.int32, (ts, LANES), 1)
    row = lax.broadcasted_iota(jnp.int32, (ts, LANES), 0) % GDN_CHUNK
    beta_ref[...] = jax.nn.sigmoid(x)
    z = x + dtb_ref[...]
    softplus = jnp.maximum(z, 0.0) + jnp.log(1.0 + jnp.exp(-jnp.abs(z)))
    g = jnp.where(jnp.logical_and(lane >= nhd, lane < 2 * nhd), -jnp.exp(alog_ref[...]) * softplus, 0.0)
    pre, suf = g, g
    s = 1
    while s < GDN_CHUNK:
        pre = pre + jnp.where(row >= s, pltpu.roll(pre, s, 0), 0.0)
        suf = suf + jnp.where(row < GDN_CHUNK - s, pltpu.roll(suf, ts - s, 0), 0.0)
        s *= 2
    gc_ref[...] = jnp.where(lane < nhd + GDN_HEADS, pre, suf)


def gdn_gates(ba, a_log, dt_bias):
    t = ba.shape[0]
    ts = _tile(t, 1024)
    nhd = 2 * GDN_HEADS
    pad = lambda v: jnp.zeros((1, LANES), F32).at[0, nhd:2 * nhd].set(v.reshape(-1).astype(F32))
    shp = jax.ShapeDtypeStruct((t, LANES), F32)
    vec = pl.BlockSpec((1, LANES), lambda i: (0, 0))
    blk = pl.BlockSpec((ts, LANES), lambda i: (i, 0))
    return pl.pallas_call(
        functools.partial(_gdn_gate_kernel, ts=ts),
        out_shape=(shp, shp),
        grid=(t // ts,),
        in_specs=[blk, vec, vec],
        out_specs=(blk, blk),
        compiler_params=_cp("parallel"),
        name="gdn_gates",
    )(ba, pad(a_log), pad(dt_bias))


def _nt(a, b):
    return lax.dot_general(a, b, (((1,), (1,)), ((), ())), preferred_element_type=F32)


def _tn(a, b):
    return lax.dot_general(a, b, (((0,), (0,)), ((), ())), preferred_element_type=F32)


def _mmb(a, b):
    return jnp.dot(a.astype(BF16), b.astype(BF16), preferred_element_type=F32)


def _gdn_chunks(qs, ks, vs, bcols, gcols, states, revs):
    c = GDN_CHUNK
    idx = range(len(qs))
    ri = lax.broadcasted_iota(jnp.int32, (c, c), 0)
    ci = lax.broadcasted_iota(jnp.int32, (c, c), 1)
    strict = {False: ri > ci, True: ri < ci}
    incl = {False: ri >= ci, True: ri <= ci}
    eye = jnp.where(ri == ci, 1.0, 0.0)
    lane = lax.broadcasted_iota(jnp.int32, (c, LANES), 1)
    ones3 = jnp.where(lane < 3, 1.0, 0.0)
    ones3_hi = jnp.where(jnp.logical_and(lane >= 3, lane < 6), 1.0, 0.0)
    k16 = [ks[i].astype(BF16) for i in idx]
    qk = [_nt(qs[i].astype(BF16), k16[i]) for i in idx]
    g_last = [gcols[i][0:1, :] if revs[i] else gcols[i][c - 1:c, :] for i in idx]
    egc = [jnp.exp(gcols[i]) for i in idx]
    q_state = [_mmb(qs[i] * egc[i], states[i]) for i in idx]
    diff = []
    for i in idx:
        g1 = gcols[i].astype(BF16).astype(F32)
        g2 = (gcols[i] - g1).astype(BF16).astype(F32)
        g3 = (gcols[i] - g1 - g2).astype(BF16).astype(F32)
        pieces = jnp.where(lane == 0, g1, jnp.where(lane == 1, g2, jnp.where(lane == 2, g3, 0.0)))
        xm = pieces + ones3_hi
        ym = ones3 - pltpu.roll(pieces, 3, 1)
        diff.append(_nt(xm.astype(BF16), ym.astype(BF16)))
    decay = [jnp.where(incl[revs[i]], jnp.exp(jnp.where(incl[revs[i]], diff[i], 0.0)), 0.0) for i in idx]
    kb = [ks[i] * bcols[i] for i in idx]
    lm = [jnp.where(strict[revs[i]], _nt(kb[i].astype(BF16), k16[i]) * decay[i], 0.0) for i in idx]
    attn = [qk[i] * decay[i] for i in idx]
    tinv = [eye - lm[i] for i in idx]
    pw = [_mmb(lm[i], lm[i]) for i in idx]
    n = 4
    while n < c:
        both = [_mmb(jnp.concatenate([tinv[i], pw[i]], axis=0), pw[i]) for i in idx]
        tinv = [tinv[i] + both[i][:c] for i in idx]
        pw = [both[i][c:] for i in idx]
        n *= 2
    tinv = [tinv[i] + _mmb(tinv[i], pw[i]) for i in idx]
    u = [_mmb(tinv[i], jnp.concatenate([vs[i] * bcols[i], kb[i] * egc[i]], axis=1)) for i in idx]
    v_new = [u[i][:, :HEAD_DIM] - _mmb(u[i][:, HEAD_DIM:], states[i]) for i in idx]
    outs = [q_state[i] + _mmb(attn[i], v_new[i]) for i in idx]
    k_e = [ks[i] * jnp.exp(g_last[i] - gcols[i]) for i in idx]
    new_states = [states[i] * jnp.exp(g_last[i]) + _tn(k_e[i].astype(BF16), v_new[i].astype(BF16)) for i in idx]
    return outs, new_states


def _gdn_kernel(qf_ref, kf_ref, vf_ref, bf_ref, gf_ref, qb_ref, kb_ref, vb_ref, bb_ref, gb_ref,
                of_ref, ob_ref, state_ref, *, chunks):
    @pl.when(pl.program_id(1) == 0)
    def _():
        state_ref[...] = jnp.zeros_like(state_ref)

    nh = GDN_HEADS

    def body(it, carry):
        rows_f = pl.ds(pl.multiple_of(it * GDN_CHUNK, GDN_CHUNK), GDN_CHUNK)
        rows_b = pl.ds(pl.multiple_of((chunks - 1 - it) * GDN_CHUNK, GDN_CHUNK), GDN_CHUNK)
        tabs = ((qf_ref, kf_ref, vf_ref, bf_ref[rows_f, :], gf_ref[rows_f, :], rows_f, 0),
                (qb_ref, kb_ref, vb_ref, bb_ref[rows_b, :], gb_ref[rows_b, :], rows_b, nh))
        qs, ks, vs, bcols, gcols, states, revs = [], [], [], [], [], [], []
        for d, (q_ref, k_ref, v_ref, bt, gt, rows, lane0) in enumerate(tabs):
            for h in range(nh):
                qs.append(q_ref[h, rows, :].astype(F32))
                ks.append(k_ref[h, rows, :].astype(F32))
                vs.append(v_ref[h, rows, :].astype(F32))
                bcols.append(bt[:, lane0 + h:lane0 + h + 1])
                gcols.append(gt[:, 2 * nh + lane0 + h:2 * nh + lane0 + h + 1])
                states.append(state_ref[d * nh + h])
                revs.append(d == 1)
        outs, new_states = _gdn_chunks(qs, ks, vs, bcols, gcols, states, revs)
        for d, o_ref in enumerate((of_ref, ob_ref)):
            for h in range(nh):
                state_ref[d * nh + h] = new_states[d * nh + h]
                o_ref[h, tabs[d][5], :] = outs[d * nh + h].astype(o_ref.dtype)
        return carry

    lax.fori_loop(0, chunks, body, 0)


def gdn_scan(qkv_h, beta, gc, batch, seq):
    t = batch * seq
    rows = _tile(seq, 512)
    nblk = seq // rows
    fwd = lambda b, ib: b * nblk + ib
    bwd = lambda b, ib: b * nblk + nblk - 1 - ib

    def specs(blk):
        qspec = lambda part: pl.BlockSpec((GDN_HEADS, rows, HEAD_DIM), lambda b, ib: (part, blk(b, ib), 0))
        gspec = pl.BlockSpec((rows, LANES), lambda b, ib: (blk(b, ib), 0))
        return [qspec(0), qspec(1), qspec(2), gspec, gspec]

    ospec = lambda blk: pl.BlockSpec((GDN_HEADS, rows, HEAD_DIM), lambda b, ib: (0, blk(b, ib), 0))
    out = jax.ShapeDtypeStruct((GDN_HEADS, t, HEAD_DIM), BF16)
    return pl.pallas_call(
        functools.partial(_gdn_kernel, chunks=rows // GDN_CHUNK),
        out_shape=(out, out),
        grid=(batch, nblk),
        in_specs=specs(fwd) + specs(bwd),
        out_specs=(ospec(fwd), ospec(bwd)),
        scratch_shapes=[pltpu.VMEM((2 * GDN_HEADS, HEAD_DIM, HEAD_DIM), F32)],
        compiler_params=_cp("parallel", "arbitrary"),
        name="gdn_scan",
    )(qkv_h, qkv_h, qkv_h, beta, gc, qkv_h, qkv_h, qkv_h, beta, gc)


def _gdn_post_kernel(of_ref, ob_ref, z_ref, g_ref, o_ref):
    o = of_ref[...].astype(F32) + ob_ref[...].astype(F32)
    r = lax.rsqrt(jnp.mean(o * o, axis=-1, keepdims=True) + EPS)
    z = z_ref[...].astype(F32)
    o_ref[...] = (o * r * g_ref[...] * (z * jax.nn.sigmoid(z))).astype(o_ref.dtype)


def gdn_post(o_f, o_b, z, norm_g):
    nhd, t, _ = o_f.shape
    ts = _tile(t, 1024)
    hspec = pl.BlockSpec((None, ts, HEAD_DIM), lambda i, h: (h, i, 0))
    tspec = pl.BlockSpec((ts, HEAD_DIM), lambda i, h: (i, h))
    return pl.pallas_call(
        _gdn_post_kernel,
        out_shape=jax.ShapeDtypeStruct((t, nhd * HEAD_DIM), BF16),
        grid=(t // ts, nhd),
        in_specs=[hspec, hspec, tspec, pl.BlockSpec((1, HEAD_DIM), lambda i, h: (0, 0))],
        out_specs=tspec,
        compiler_params=_cp("parallel", "arbitrary"),
        name="gdn_post",
    )(o_f, o_b, z, norm_g.reshape(1, HEAD_DIM).astype(F32))


def _mem_attn_kernel(q_ref, k_ref, v_ref, o_ref):
    scale = MEM_HEAD_DIM ** -0.5
    for h in range(MEM_HEADS):
        cols = slice(h * MEM_HEAD_DIM, (h + 1) * MEM_HEAD_DIM)
        s = _nt(q_ref[:, cols], k_ref[:, cols]) * scale
        p = jnp.exp(s - jnp.max(s, axis=-1, keepdims=True))
        p = p / jnp.sum(p, axis=-1, keepdims=True)
        o_ref[:, cols] = jnp.dot(p.astype(BF16), v_ref[:, cols], preferred_element_type=F32).astype(o_ref.dtype)


def mem_attention(mq, kv, batch, seq, mem_tokens):
    tq = _tile(seq, 512)
    per = seq // tq
    return pl.pallas_call(
        _mem_attn_kernel,
        out_shape=jax.ShapeDtypeStruct(mq.shape, BF16),
        grid=(batch * per,),
        in_specs=[
            pl.BlockSpec((tq, MEM_WIDTH), lambda i: (i, 0)),
            pl.BlockSpec((mem_tokens, MEM_WIDTH), lambda i: (i // per, 0)),
            pl.BlockSpec((mem_tokens, MEM_WIDTH), lambda i: (i // per, 1)),
        ],
        out_specs=pl.BlockSpec((tq, MEM_WIDTH), lambda i: (i, 0)),
        compiler_params=_cp("parallel"),
        name="mem_attention",
    )(mq, kv, kv)


def _merge_kernel(oa_ref, ob_ref, om_ref, wa_ref, wb_ref, wm_ref, g0_ref, g1_ref, g2_ref, o_ref):
    ya = jnp.dot(oa_ref[...], wa_ref[...], preferred_element_type=F32)
    yb = jnp.dot(ob_ref[...], wb_ref[...], preferred_element_type=F32)
    ym = jnp.dot(om_ref[...], wm_ref[...], preferred_element_type=F32)
    mix = g0_ref[...].astype(F32) * ya + g1_ref[...].astype(F32) * yb + g2_ref[...].astype(F32) * ym
    o_ref[...] = mix.astype(o_ref.dtype)


def gated_merge(o_a, o_b, o_m, w_a, w_b, w_m, gates):
    t = o_a.shape[0]
    d = w_a.shape[1]
    tm, tn = _tile(t, 1024), _tile(d, 512)
    nj = d // tn
    act = lambda o: pl.BlockSpec((tm, o.shape[1]), lambda i, j: (i, 0))
    wsp = lambda w: pl.BlockSpec((w.shape[0], tn), lambda i, j: (0, j))
    gsp = lambda br: pl.BlockSpec((tm, tn), lambda i, j: (i, br * nj + j))
    return pl.pallas_call(
        _merge_kernel,
        out_shape=jax.ShapeDtypeStruct((t, d), BF16),
        grid=(t // tm, nj),
        in_specs=[act(o_a), act(o_b), act(o_m), wsp(w_a), wsp(w_b), wsp(w_m), gsp(0), gsp(1), gsp(2)],
        out_specs=pl.BlockSpec((tm, tn), lambda i, j: (i, j)),
        compiler_params=_cp("parallel", "arbitrary"),
        name="gated_merge",
    )(o_a, o_b, o_m, w_a, w_b, w_m, gates, gates, gates)


ROW_SPLIT = 16


def _split_rows(ref, x):
    n = x.shape[0]
    for s in range(ROW_SPLIT):
        ref[pl.ds(s, n, stride=ROW_SPLIT), :] = x[:, s * LANES:(s + 1) * LANES]


def _merge_rows(ref, n):
    return jnp.concatenate([ref[pl.ds(s, n, stride=ROW_SPLIT), :] for s in range(ROW_SPLIT)], axis=1)


MOE_ROWS_PER_STEP = 128
ROUTE_ROWS = SUBLANES + N_EXPERTS


def _route_kernel(x_ref, g_ref, w_ref, b_ref, xn_ref, eid_ref, wt_ref):
    x = x_ref[...]
    r = lax.rsqrt(jnp.mean(x * x, axis=-1, keepdims=True) + EPS)
    xn = x * r * g_ref[...]
    _split_rows(xn_ref, xn)
    tm = x.shape[0]
    lg = lax.dot_general(w_ref[...], xn, (((1,), (1,)), ((), ())), preferred_element_type=F32,
                         precision=lax.Precision.HIGHEST) + b_ref[...]
    gl = [lg[i:i + 1, :] for i in range(N_GROUPS)]
    gmax = functools.reduce(jnp.maximum, gl)
    grp = jnp.full((1, tm), N_GROUPS - 1, jnp.int32)
    for i in range(N_GROUPS - 2, -1, -1):
        grp = jnp.where(gl[i] == gmax, i, grp)
    p_grp = 1.0 / functools.reduce(jnp.add, [jnp.exp(v - gmax) for v in gl])
    sel = lg[SUBLANES + (N_GROUPS - 1) * EXPERTS_PER_GROUP:SUBLANES + N_GROUPS * EXPERTS_PER_GROUP, :]
    for i in range(N_GROUPS - 2, -1, -1):
        lo = SUBLANES + i * EXPERTS_PER_GROUP
        sel = jnp.where(grp == i, lg[lo:lo + EXPERTS_PER_GROUP, :], sel)
    rowi = lax.broadcasted_iota(jnp.int32, (EXPERTS_PER_GROUP, tm), 0)
    v1 = jnp.max(sel, axis=0, keepdims=True)
    i1 = jnp.min(jnp.where(sel == v1, rowi, EXPERTS_PER_GROUP), axis=0, keepdims=True)
    sel2 = jnp.where(rowi == i1, -jnp.inf, sel)
    v2 = jnp.max(sel2, axis=0, keepdims=True)
    i2 = jnp.min(jnp.where(sel2 == v2, rowi, EXPERTS_PER_GROUP), axis=0, keepdims=True)
    e2 = jnp.exp(v2 - v1)
    w1 = p_grp / (1.0 + e2)
    eid_ref[0:1, :] = grp * EXPERTS_PER_GROUP + i1
    eid_ref[1:2, :] = grp * EXPERTS_PER_GROUP + i2
    wt_ref[0:1, :] = w1
    wt_ref[1:2, :] = w1 * e2


def moe_route(x, g, w_grp, b_grp, w_exp):
    t, d = x.shape
    assert d == ROW_SPLIT * LANES
    tm = _tile(t, 512)
    w = jnp.zeros((ROUTE_ROWS, d), F32).at[:N_GROUPS].set(w_grp.T).at[SUBLANES:].set(w_exp.T)
    b = jnp.zeros((ROUTE_ROWS, 1), F32).at[:N_GROUPS, 0].set(b_grp.astype(F32))
    return pl.pallas_call(
        _route_kernel,
        out_shape=(jax.ShapeDtypeStruct((t * ROW_SPLIT, LANES), F32), jax.ShapeDtypeStruct((EXPERT_TOPK, t), jnp.int32),
                   jax.ShapeDtypeStruct((EXPERT_TOPK, t), F32)),
        grid=(t // tm,),
        in_specs=[pl.BlockSpec((tm, d), lambda i: (i, 0)), pl.BlockSpec((1, d), lambda i: (0, 0)),
                  pl.BlockSpec((ROUTE_ROWS, d), lambda i: (0, 0)), pl.BlockSpec((ROUTE_ROWS, 1), lambda i: (0, 0))],
        out_specs=(pl.BlockSpec((tm * ROW_SPLIT, LANES), lambda i: (i, 0)), pl.BlockSpec((EXPERT_TOPK, tm), lambda i: (0, i)),
                   pl.BlockSpec((EXPERT_TOPK, tm), lambda i: (0, i))),
        compiler_params=_cp("parallel"),
        name="moe_route",
    )(x, g.reshape(1, d), w, b)


def _rank_kernel(e_ref, rank_ref, cnt_ref, *, tb):
    @pl.when(pl.program_id(0) == 0)
    def _():
        cnt_ref[...] = jnp.zeros_like(cnt_ref)

    e = e_ref[...]
    onehot = jnp.where(lax.broadcasted_iota(jnp.int32, (N_EXPERTS, tb), 0) == e, 1.0, 0.0)
    earlier = jnp.where(lax.broadcasted_iota(jnp.int32, (tb, tb), 0) < lax.broadcasted_iota(jnp.int32, (tb, tb), 1),
                        1.0, 0.0)
    before = _mmb(onehot, earlier) + cnt_ref[...]
    rank_ref[...] = jnp.sum(onehot * before, axis=0, keepdims=True).astype(jnp.int32)
    cnt_ref[...] = cnt_ref[...] + jnp.sum(onehot, axis=1, keepdims=True)


def moe_rank(e_flat):
    n = e_flat.shape[1]
    tb = _tile(n, 512)
    return pl.pallas_call(
        functools.partial(_rank_kernel, tb=tb),
        out_shape=(jax.ShapeDtypeStruct((1, n), jnp.int32), jax.ShapeDtypeStruct((N_EXPERTS, 1), F32)),
        grid=(n // tb,),
        in_specs=[pl.BlockSpec((1, tb), lambda i: (0, i))],
        out_specs=(pl.BlockSpec((1, tb), lambda i: (0, i)), pl.BlockSpec((N_EXPERTS, 1), lambda i: (0, 0))),
        compiler_params=_cp("arbitrary"),
        name="moe_rank",
    )(e_flat)


def _dispatch_kernel(dest_ref, cnt_ref, pstart_ref, x_ref, xs_ref, zero_ref, sem, *, tb, tokens, nblk):
    i = pl.program_id(0)

    rs = ROW_SPLIT

    def row_copy(src_ref, src_row, dst_row):
        dst = pl.multiple_of(dst_row * rs, rs)
        return pltpu.make_async_copy(src_ref.at[pl.ds(src_row * rs, rs)], xs_ref.at[pl.ds(dst, rs)], sem)

    def block_copy(blk):
        dst = pl.multiple_of(blk * (MOE_BLOCK * rs), MOE_BLOCK * rs)
        return pltpu.make_async_copy(zero_ref, xs_ref.at[pl.ds(dst, MOE_BLOCK * rs)], sem)

    @pl.when(i == 0)
    def _():
        zero_ref[...] = jnp.zeros_like(zero_ref)
        last = N_EXPERTS - 1
        n_used = (pstart_ref[last] + cnt_ref[last] + MOE_BLOCK - 1) // MOE_BLOCK

        def fill_block(blk, carry):
            block_copy(blk).start()
            block_copy(blk).wait()
            return carry

        lax.fori_loop(n_used, nblk, fill_block, 0)

        def per_expert(e, carry):
            npad = (-cnt_ref[e]) % MOE_BLOCK
            base = pstart_ref[e] + cnt_ref[e]

            def fill(r, c2):
                row_copy(zero_ref, 0, base + r).start()
                return c2

            lax.fori_loop(0, npad, fill, 0)

            def drain(r, c2):
                row_copy(zero_ref, 0, base + r).wait()
                return c2

            lax.fori_loop(0, npad, drain, 0)
            return carry

        lax.fori_loop(0, N_EXPERTS, per_expert, 0)

    copies = [row_copy(x_ref, j, dest_ref[kk * tokens + i * tb + j]) for j in range(tb) for kk in range(EXPERT_TOPK)]
    for cp in copies:
        cp.start()
    for cp in copies:
        cp.wait()


def moe_dispatch(xn, dest_flat, counts, pad_start, n_slots):
    t = xn.shape[0] // ROW_SPLIT
    tb = _tile(t, MOE_ROWS_PER_STEP)
    return pl.pallas_call(
        functools.partial(_dispatch_kernel, tb=tb, tokens=t, nblk=n_slots // MOE_BLOCK),
        out_shape=jax.ShapeDtypeStruct((n_slots * ROW_SPLIT, LANES), xn.dtype),
        grid_spec=pltpu.PrefetchScalarGridSpec(
            num_scalar_prefetch=3,
            grid=(t // tb,),
            in_specs=[pl.BlockSpec((tb * ROW_SPLIT, LANES), lambda i, *_: (i, 0))],
            out_specs=pl.BlockSpec(memory_space=pl.ANY),
            scratch_shapes=[pltpu.VMEM((MOE_BLOCK * ROW_SPLIT, LANES), xn.dtype), pltpu.SemaphoreType.DMA],
        ),
        compiler_params=pltpu.CompilerParams(dimension_semantics=("arbitrary",), has_side_effects=True,
                                             vmem_limit_bytes=VMEM_LIMIT),
        name="moe_dispatch",
    )(dest_flat, counts, pad_start, xn)


def _expert_kernel(be_ref, nused_ref, x_ref, wg_ref, wu_ref, wd_ref, y_ref):
    @pl.when(pl.program_id(0) < nused_ref[0])
    def _():
        x = _merge_rows(x_ref, MOE_BLOCK).astype(BF16)
        hg = jnp.dot(x, wg_ref[...], preferred_element_type=F32)
        hu = jnp.dot(x, wu_ref[...], preferred_element_type=F32)
        hmid = (hg * jax.nn.sigmoid(hg) * hu).astype(BF16)
        _split_rows(y_ref, jnp.dot(hmid, wd_ref[...], preferred_element_type=F32))

    @pl.when(pl.program_id(0) >= nused_ref[0])
    def _():
        y_ref[...] = jnp.zeros_like(y_ref)


def moe_experts(xs, blk_expert, n_used, w_gate, w_up, w_down):
    d, de = w_gate.shape[1:]
    blk_rows = MOE_BLOCK * ROW_SPLIT
    nblk = xs.shape[0] // blk_rows
    row = lambda i, be, nu: (jnp.minimum(i, nu[0] - 1), 0)
    wsel = lambda i, be, nu: (be[jnp.minimum(i, nu[0] - 1)], 0, 0)
    return pl.pallas_call(
        _expert_kernel,
        out_shape=jax.ShapeDtypeStruct(xs.shape, F32),
        grid_spec=pltpu.PrefetchScalarGridSpec(
            num_scalar_prefetch=2,
            grid=(nblk,),
            in_specs=[pl.BlockSpec((blk_rows, LANES), row), pl.BlockSpec((None, d, de), wsel),
                      pl.BlockSpec((None, d, de), wsel), pl.BlockSpec((None, de, d), wsel)],
            out_specs=pl.BlockSpec((blk_rows, LANES), lambda i, be, nu: (i, 0)),
        ),
        compiler_params=_cp("arbitrary"),
        name="moe_experts",
    )(blk_expert, n_used, xs, w_gate, w_up, w_down)


def _combine_kernel(dest_ref, x_ref, wt_ref, g_ref, ys_ref, o_ref, y0_ref, y1_ref, sem, *, tb, tokens):
    i = pl.program_id(0)
    bufs = (y0_ref, y1_ref)

    def row_copy(kk, j):
        src = pl.multiple_of(dest_ref[kk * tokens + i * tb + j] * ROW_SPLIT, ROW_SPLIT)
        return pltpu.make_async_copy(ys_ref.at[pl.ds(src, ROW_SPLIT)],
                                     bufs[kk].at[pl.ds(j * ROW_SPLIT, ROW_SPLIT)], sem)

    copies = [row_copy(kk, j) for j in range(tb) for kk in range(EXPERT_TOPK)]
    for cp in copies:
        cp.start()
    for cp in copies:
        cp.wait()
    h = x_ref[...] + wt_ref[:, 0:1] * _merge_rows(y0_ref, tb) + wt_ref[:, 1:2] * _merge_rows(y1_ref, tb)
    r = lax.rsqrt(jnp.mean(h * h, axis=-1, keepdims=True) + EPS)
    o_ref[...] = h * r * g_ref[...]


def moe_combine(x, ys, dest_flat, wts_tok, g_final):
    t, d = x.shape
    tb = _tile(t, MOE_ROWS_PER_STEP)
    return pl.pallas_call(
        functools.partial(_combine_kernel, tb=tb, tokens=t),
        out_shape=jax.ShapeDtypeStruct((t, d), F32),
        grid_spec=pltpu.PrefetchScalarGridSpec(
            num_scalar_prefetch=1,
            grid=(t // tb,),
            in_specs=[pl.BlockSpec((tb, d), lambda i, dr: (i, 0)),
                      pl.BlockSpec((tb, EXPERT_TOPK), lambda i, dr: (i, 0)),
                      pl.BlockSpec((1, d), lambda i, dr: (0, 0)),
                      pl.BlockSpec(memory_space=pl.ANY)],
            out_specs=pl.BlockSpec((tb, d), lambda i, dr: (i, 0)),
            scratch_shapes=[pltpu.VMEM((tb * ROW_SPLIT, LANES), F32), pltpu.VMEM((tb * ROW_SPLIT, LANES), F32),
                            pltpu.SemaphoreType.DMA],
        ),
        compiler_params=_cp("arbitrary"),
        name="moe_combine",
    )(dest_flat, x, wts_tok, g_final.reshape(1, d), ys)


def hier_moe_final(x2, g_ffn, w_rg, b_rg, w_re, w_gate, w_up, w_down, g_final):
    t, d = x2.shape
    xn, eid, wts = moe_route(x2, g_ffn, w_rg, b_rg, w_re)
    n = EXPERT_TOPK * t
    e_flat = eid.reshape(1, n)
    rank, counts_f = moe_rank(e_flat)
    counts = counts_f.reshape(N_EXPERTS).astype(jnp.int32)
    padded = (counts + MOE_BLOCK - 1) // MOE_BLOCK * MOE_BLOCK
    pad_end = jnp.cumsum(padded)
    pad_start = pad_end - padded
    dest = (jnp.take(pad_start, e_flat[0]) + rank[0]).astype(jnp.int32)
    nblk = -(-n // MOE_BLOCK) + N_EXPERTS
    blk_start = jnp.arange(nblk, dtype=jnp.int32) * MOE_BLOCK
    blk_expert = jnp.minimum(jnp.sum(pad_end[None, :] <= blk_start[:, None], axis=1), N_EXPERTS - 1).astype(jnp.int32)
    n_used = (pad_end[-1:] // MOE_BLOCK).astype(jnp.int32)
    xs = moe_dispatch(xn, dest, counts, pad_start.astype(jnp.int32), nblk * MOE_BLOCK)
    ys = moe_experts(xs, blk_expert, n_used, w_gate.astype(BF16), w_up.astype(BF16), w_down.astype(BF16))
    return moe_combine(x2, ys, dest, wts.T, g_final)


def kernel(x, mem, g_mix, w_in, b_gate, gdn_conv, gdn_a_log, gdn_dt_bias, gdn_norm_g, g_mem, w_mem_kv, w_o_swa, w_o_gdn, w_o_mem, w_out, g_ffn, w_route_group, b_route_group, w_route_expert, w_expert_gate, w_expert_up, w_expert_down, g_final):
    batch, seq, d = x.shape
    depth = w_in.shape[0]
    assert depth == 1, "the final RMSNorm is fused into the layer's MoE combine"
    t = batch * seq
    mem_tokens = mem.shape[1]
    h = x.reshape(t, d)
    o0 = 3 * SWA_WIDTH
    o1 = o0 + 3 * GDN_WIDTH
    o2 = o1 + GDN_WIDTH
    o3 = o2 + 4 * GDN_HEADS
    o4 = o3 + MEM_WIDTH
    tables = rope_tables(seq)
    for l in range(depth):
        w = w_in[l].astype(BF16)
        a = rmsnorm_rows(h, g_mix[l], BF16)
        qkv_a = matmul(a, w, F32, n=o0, rope=(tables, seq))
        o_a = dilated_swa(qkv_a, batch, seq)
        qkv_b = matmul_conv_norm(a, w, o0, gdn_conv[l], seq)
        z_b = matmul(a, w, BF16, n=GDN_WIDTH, col_off=o1)
        w_ba = jnp.zeros((d, LANES), BF16).at[:, :o3 - o2].set(w[:, o2:o3])
        beta, gc = gdn_gates(matmul(a, w_ba, F32), gdn_a_log[l], gdn_dt_bias[l])
        o_f, o_r = gdn_scan(qkv_b, beta, gc, batch, seq)
        o_b = gdn_post(o_f, o_r, z_b, gdn_norm_g[l])
        kv = matmul(rmsnorm_rows(mem.reshape(batch * mem_tokens, d), g_mem[l], BF16), w_mem_kv[l].astype(BF16), BF16)
        o_m = mem_attention(matmul(a, w[:, o3:o4], BF16), kv, batch, seq, mem_tokens)
        gates = matmul(a, w[:, o4:], BF16, bias=b_gate[l].reshape(-1))
        mix = gated_merge(o_a, o_b, o_m, w_o_swa[l].astype(BF16), w_o_gdn[l].astype(BF16),
                          w_o_mem[l].astype(BF16), gates)
        x2 = matmul(mix, w_out[l].astype(BF16), F32, residual=h)
        h = hier_moe_final(x2, g_ffn[l], w_route_group[l], b_route_group[l], w_route_expert[l],
                           w_expert_gate[l], w_expert_up[l], w_expert_down[l], g_final)
    return h.reshape(batch, seq, d)
```

```python
import functools

import jax
import jax.numpy as jnp
from jax import lax
from jax.experimental import pallas as pl
from jax.experimental.pallas import tpu as pltpu

F32 = jnp.float32
BF16 = jnp.bfloat16

HEAD_DIM = 128
SWA_GROUPS = ((128, 1), (512, 4), (2048, 16))
SWA_HEADS_PER_GROUP = 4
SWA_HEADS = SWA_HEADS_PER_GROUP * len(SWA_GROUPS)
SWA_WIDTH = SWA_HEADS * HEAD_DIM
SWA_BLOCK = 64
SWA_UNROLL = 4
ROPE_THETA = 500000.0
ROPE_DIMS = HEAD_DIM // 4
GDN_HEADS = 12
GDN_WIDTH = GDN_HEADS * HEAD_DIM
GDN_CONV = 5
GDN_CHUNK = 64
MEM_HEADS = 4
MEM_HEAD_DIM = 256
MEM_WIDTH = MEM_HEADS * MEM_HEAD_DIM
N_BRANCH = 3
N_GROUPS = 4
EXPERTS_PER_GROUP = 8
N_EXPERTS = N_GROUPS * EXPERTS_PER_GROUP
EXPERT_TOPK = 2
MOE_BLOCK = 256
EPS = 1e-6
NEG_INF = -1e30

LANES = 128
SUBLANES = 8
VMEM_LIMIT = 48 * 1024 * 1024


def _cp(*sem, vmem=VMEM_LIMIT):
    return pltpu.CompilerParams(dimension_semantics=sem, vmem_limit_bytes=vmem)


def _tile(n, pref):
    t = min(n, pref)
    while n % t:
        t //= 2
    return t


def _rmsnorm_kernel(x_ref, g_ref, o_ref):
    x = x_ref[...].astype(F32)
    r = lax.rsqrt(jnp.mean(x * x, axis=-1, keepdims=True) + EPS)
    o_ref[...] = (x * r * g_ref[...]).astype(o_ref.dtype)


def rmsnorm_rows(x, g, out_dtype):
    m, d = x.shape
    tm = _tile(m, 512)
    return pl.pallas_call(
        _rmsnorm_kernel,
        out_shape=jax.ShapeDtypeStruct((m, d), out_dtype),
        grid=(m // tm,),
        in_specs=[pl.BlockSpec((tm, d), lambda i: (i, 0)), pl.BlockSpec((1, d), lambda i: (0, 0))],
        out_specs=pl.BlockSpec((tm, d), lambda i: (i, 0)),
        compiler_params=_cp("parallel"),
        name="rmsnorm_rows",
    )(x, g.reshape(1, d))


def _mm_kernel(a_ref, w_ref, o_ref):
    o_ref[...] = jnp.dot(a_ref[...], w_ref[...], preferred_element_type=F32).astype(o_ref.dtype)


def _mm_sigmoid_kernel(a_ref, w_ref, b_ref, o_ref):
    z = jnp.dot(a_ref[...], w_ref[...], preferred_element_type=F32) + b_ref[...]
    o_ref[...] = jax.nn.sigmoid(z).astype(o_ref.dtype)


def _mm_residual_kernel(a_ref, w_ref, r_ref, o_ref):
    o_ref[...] = r_ref[...] + jnp.dot(a_ref[...], w_ref[...], preferred_element_type=F32)


EPI_COLS = 2 * HEAD_DIM


def _mm_rope_kernel(a_ref, w_ref, c_ref, s1_ref, s2_ref, o_ref):
    c, s1, s2 = c_ref[...], s1_ref[...], s2_ref[...]
    half = ROPE_DIMS // 2
    for c0 in range(0, o_ref.shape[1], EPI_COLS):
        acc = jnp.dot(a_ref[...], w_ref[:, c0:c0 + EPI_COLS], preferred_element_type=F32)
        for h0 in range(0, EPI_COLS, HEAD_DIM):
            xh = acc[:, h0:h0 + HEAD_DIM]
            o_ref[:, c0 + h0:c0 + h0 + HEAD_DIM] = (
                xh * c + pltpu.roll(xh, HEAD_DIM - half, 1) * s1 + pltpu.roll(xh, half, 1) * s2)


MM_TM = 1024
MM_TN = (1536, 1024, 512)


def _col_tile(n):
    return next((t for t in MM_TN if n % t == 0), n)


def matmul(a, w, out_dtype, *, n=None, col_off=0, bias=None, residual=None, rope=None):
    m, k = a.shape
    n = w.shape[1] if n is None else n
    tm, tn = _tile(m, MM_TM), _col_tile(n)
    assert col_off % tn == 0
    joff = col_off // tn
    a_spec = pl.BlockSpec((tm, k), lambda i, j: (i, 0))
    w_spec = pl.BlockSpec((k, tn), lambda i, j: (0, joff + j))
    o_spec = pl.BlockSpec((tm, tn), lambda i, j: (i, j))
    if bias is not None:
        kern, extra, extra_specs = _mm_sigmoid_kernel, (bias.reshape(1, n),), [pl.BlockSpec((1, tn), lambda i, j: (0, j))]
    elif residual is not None:
        kern, extra, extra_specs = _mm_residual_kernel, (residual,), [o_spec]
    elif rope is not None:
        tables, seq = rope
        assert tn == SWA_WIDTH and n == 3 * SWA_WIDTH
        per = seq // tm
        t_spec = pl.BlockSpec((None, tm, HEAD_DIM), lambda i, j: (j, i % per, 0))
        kern = _mm_rope_kernel
        extra, extra_specs = tuple(tables), [t_spec] * 3
    else:
        kern, extra, extra_specs = _mm_kernel, (), []
    return pl.pallas_call(
        kern,
        out_shape=jax.ShapeDtypeStruct((m, n), out_dtype),
        grid=(m // tm, n // tn),
        in_specs=[a_spec, w_spec] + extra_specs,
        out_specs=o_spec,
        compiler_params=_cp("parallel", "arbitrary"),
        name="matmul",
    )(a, w, *extra)


def rope_tables(seq):
    half = ROPE_DIMS // 2
    inv = ROPE_THETA ** (-jnp.arange(half, dtype=F32) / half)
    ang = jnp.arange(seq, dtype=F32)[:, None] * inv[None, :]
    cos, sin = jnp.cos(ang), jnp.sin(ang)
    zeros = jnp.zeros((seq, HEAD_DIM - ROPE_DIMS), F32)
    zh = jnp.zeros((seq, half), F32)
    c = jnp.concatenate([cos, cos, zeros + 1.0], axis=1)
    s1 = jnp.concatenate([-sin, zh, zeros], axis=1)
    s2 = jnp.concatenate([zh, sin, zeros], axis=1)
    ident = (jnp.ones_like(c), jnp.zeros_like(c), jnp.zeros_like(c))
    return tuple(jnp.stack([tab * (HEAD_DIM ** -0.5), tab, idt]) for tab, idt in zip((c, s1, s2), ident))


CONV_HALO = 16


def _mm_conv_kernel(a_ref, ap_ref, an_ref, w_ref, cw_ref, o_ref, buf_ref, *, tm, tiles_per_seq):
    i, kind = pl.program_id(0), pl.program_id(1)
    halo = CONV_HALO
    rows = tm + 2 * halo
    keep_prev = jnp.where((i % tiles_per_seq) == 0, 0.0, 1.0)
    keep_next = jnp.where((i % tiles_per_seq) == tiles_per_seq - 1, 0.0, 1.0)
    qk_mul = jnp.where(kind == 0, HEAD_DIM ** -0.5, 1.0)
    w = w_ref[...]
    buf_ref[0:halo, :] = jnp.dot(ap_ref[...], w, preferred_element_type=F32) * keep_prev
    buf_ref[halo:halo + tm, :] = jnp.dot(a_ref[...], w, preferred_element_type=F32)
    buf_ref[halo + tm:, :] = jnp.dot(an_ref[...], w, preferred_element_type=F32) * keep_next
    for h in range(GDN_HEADS):
        cols = slice(h * HEAD_DIM, (h + 1) * HEAD_DIM)
        xh = buf_ref[:, cols]
        acc = xh * cw_ref[GDN_CONV // 2:GDN_CONV // 2 + 1, cols]
        for t in range(GDN_CONV):
            if t != GDN_CONV // 2:
                acc = acc + pltpu.roll(xh, (GDN_CONV // 2 - t) % rows, 0) * cw_ref[t:t + 1, cols]
        acc = acc[halo:halo + tm, :]
        y = acc * jax.nn.sigmoid(acc)
        inv = lax.rsqrt(jnp.sum(y * y, axis=-1, keepdims=True) + EPS) * qk_mul
        o_ref[h] = (y * jnp.where(kind < 2, inv, 1.0)).astype(o_ref.dtype)


def matmul_conv_norm(a, w, col_off, conv_w, seq):
    m, k = a.shape
    tn = GDN_WIDTH
    tm = _tile(seq, MM_TM)
    assert col_off % tn == 0 and tm % CONV_HALO == 0
    joff = col_off // tn
    per_h = tm // CONV_HALO
    nblk_h = m // CONV_HALO
    return pl.pallas_call(
        functools.partial(_mm_conv_kernel, tm=tm, tiles_per_seq=seq // tm),
        out_shape=jax.ShapeDtypeStruct((3 * GDN_HEADS, m, HEAD_DIM), BF16),
        grid=(m // tm, 3),
        in_specs=[
            pl.BlockSpec((tm, k), lambda i, j: (i, 0)),
            pl.BlockSpec((CONV_HALO, k), lambda i, j: (jnp.maximum(i * per_h - 1, 0), 0)),
            pl.BlockSpec((CONV_HALO, k), lambda i, j: (jnp.minimum((i + 1) * per_h, nblk_h - 1), 0)),
            pl.BlockSpec((k, tn), lambda i, j: (0, joff + j)),
            pl.BlockSpec((GDN_CONV, tn), lambda i, j: (0, j)),
        ],
        out_specs=pl.BlockSpec((GDN_HEADS, tm, HEAD_DIM), lambda i, j: (j, i, 0)),
        scratch_shapes=[pltpu.VMEM((tm + 2 * CONV_HALO, tn), F32)],
        compiler_params=_cp("parallel", "arbitrary"),
        name="matmul_conv_norm",
    )(a, a, a, w, conv_w)


def _swa_kernel(q_ref, k_ref, v_ref, o_ref, acc_ref, m_ref, l_ref, *, seq):
    g = pl.program_id(2)

    @pl.when(g == 0)
    def _():
        acc_ref[...] = jnp.zeros_like(acc_ref)
        m_ref[...] = jnp.full_like(m_ref, NEG_INF)
        l_ref[...] = jnp.zeros_like(l_ref)

    for gi, (window, dil) in enumerate(SWA_GROUPS):
        radius = window // (2 * dil)
        assert radius <= SWA_BLOCK
        sub = seq // dil
        qb = min(128, sub)
        win = min(sub, qb + 2 * SWA_BLOCK)
        nqb = sub // qb

        @pl.when(g == gi)
        def _(dil=dil, radius=radius, sub=sub, qb=qb, win=win, nqb=nqb):
            def rows(start, size):
                return pl.ds(start, size) if dil == 1 else pl.ds(start, size, stride=dil)

            total = dil * nqb
            unroll = SWA_UNROLL if total % SWA_UNROLL == 0 else 1
            idx = range(unroll)

            def body(step, carry):
                its = [step * unroll + u for u in idx]
                qss = [(it % nqb) * qb for it in its]
                wss = [jnp.clip(qs - SWA_BLOCK, 0, sub - win) for qs in qss]
                q_rows = [rows(it // nqb + dil * qs, qb) for it, qs in zip(its, qss)]
                k_rows = [rows(it // nqb + dil * ws, win) for it, ws in zip(its, wss)]
                q = [q_ref[r, :].astype(BF16) for r in q_rows]
                k = [k_ref[r, :].astype(BF16) for r in k_rows]
                v = [v_ref[r, :].astype(BF16) for r in k_rows]
                s = [_nt(q[u], k[u]) for u in idx]
                off = lax.broadcasted_iota(jnp.int32, (qb, win), 0) - lax.broadcasted_iota(jnp.int32, (qb, win), 1)
                s = [jnp.where(jnp.abs(off + (qss[u] - wss[u])) <= radius, s[u], NEG_INF) for u in idx]
                m_old = [m_ref[r, :] for r in q_rows]
                m_new = [jnp.maximum(m_old[u], jnp.max(s[u], axis=1, keepdims=True)) for u in idx]
                p = [jnp.exp(s[u] - m_new[u]) for u in idx]
                corr = [jnp.exp(m_old[u] - m_new[u]) for u in idx]
                pv = [jnp.dot(p[u].astype(BF16), v[u], preferred_element_type=F32) for u in idx]
                for u in idx:
                    l_ref[q_rows[u], :] = l_ref[q_rows[u], :] * corr[u] + jnp.sum(p[u], axis=1, keepdims=True)
                    acc_ref[q_rows[u], :] = acc_ref[q_rows[u], :] * corr[u] + pv[u]
                    m_ref[q_rows[u], :] = m_new[u]
                return carry

            lax.fori_loop(0, total // unroll, body, 0)

    @pl.when(g == len(SWA_GROUPS) - 1)
    def _():
        o_ref[...] = (acc_ref[...] / l_ref[...]).astype(o_ref.dtype)


def dilated_swa(qkv, batch, seq):
    nh = SWA_HEADS_PER_GROUP
    qkv3 = qkv.reshape(batch, seq, 3 * SWA_WIDTH)

    def spec(off):
        return pl.BlockSpec((None, seq, HEAD_DIM), lambda b, h, g: (b, 0, off + g * nh + h))

    out = pl.pallas_call(
        functools.partial(_swa_kernel, seq=seq),
        out_shape=jax.ShapeDtypeStruct((batch, seq, nh * HEAD_DIM), BF16),
        grid=(batch, nh, len(SWA_GROUPS)),
        in_specs=[spec(0), spec(SWA_HEADS), spec(2 * SWA_HEADS)],
        out_specs=pl.BlockSpec((None, seq, HEAD_DIM), lambda b, h, g: (b, 0, h)),
        scratch_shapes=[pltpu.VMEM((seq, HEAD_DIM), F32), pltpu.VMEM((seq, 1), F32), pltpu.VMEM((seq, 1), F32)],
        compiler_params=_cp("parallel", "parallel", "arbitrary"),
        name="dilated_swa",
    )(qkv3, qkv3, qkv3)
    return out.reshape(batch * seq, nh * HEAD_DIM)


def _gdn_gate_kernel(ba_ref, alog_ref, dtb_ref, beta_ref, gc_ref, *, ts):
    x = ba_ref[...]
    nhd = 2 * GDN_HEADS
    lane = lax.broadcasted_iota(jnp.int32, (ts, LANES), 1)
    row = lax.broadcasted_iota(jnp.int32, (ts, LANES), 0) % GDN_CHUNK
    beta_ref[...] = jax.nn.sigmoid(x)
    z = x + dtb_ref[...]
    softplus = jnp.maximum(z, 0.0) + jnp.log(1.0 + jnp.exp(-jnp.abs(z)))
    g = jnp.where(jnp.logical_and(lane >= nhd, lane < 2 * nhd), -jnp.exp(alog_ref[...]) * softplus, 0.0)
    pre, suf = g, g
    s = 1
    while s < GDN_CHUNK:
        pre = pre + jnp.where(row >= s, pltpu.roll(pre, s, 0), 0.0)
        suf = suf + jnp.where(row < GDN_CHUNK - s, pltpu.roll(suf, ts - s, 0), 0.0)
        s *= 2
    gc_ref[...] = jnp.where(lane < nhd + GDN_HEADS, pre, suf)


def gdn_gates(ba, a_log, dt_bias):
    t = ba.shape[0]
    ts = _tile(t, 1024)
    nhd = 2 * GDN_HEADS
    pad = lambda v: jnp.zeros((1, LANES), F32).at[0, nhd:2 * nhd].set(v.reshape(-1).astype(F32))
    shp = jax.ShapeDtypeStruct((t, LANES), F32)
    vec = pl.BlockSpec((1, LANES), lambda i: (0, 0))
    blk = pl.BlockSpec((ts, LANES), lambda i: (i, 0))
    return pl.pallas_call(
        functools.partial(_gdn_gate_kernel, ts=ts),
        out_shape=(shp, shp),
        grid=(t // ts,),
        in_specs=[blk, vec, vec],
        out_specs=(blk, blk),
        compiler_params=_cp("parallel"),
        name="gdn_gates",
    )(ba, pad(a_log), pad(dt_bias))


def _nt(a, b):
    return lax.dot_general(a, b, (((1,), (1,)), ((), ())), preferred_element_type=F32)


def _tn(a, b):
    return lax.dot_general(a, b, (((0,), (0,)), ((), ())), preferred_element_type=F32)


def _mmb(a, b):
    return jnp.dot(a.astype(BF16), b.astype(BF16), preferred_element_type=F32)


def _gdn_chunks(qs, ks, vs, bcols, gcols, states, revs):
    c = GDN_CHUNK
    idx = range(len(qs))
    ri = lax.broadcasted_iota(jnp.int32, (c, c), 0)
    ci = lax.broadcasted_iota(jnp.int32, (c, c), 1)
    strict = {False: ri > ci, True: ri < ci}
    incl = {False: ri >= ci, True: ri <= ci}
    eye = jnp.where(ri == ci, 1.0, 0.0)
    lane = lax.broadcasted_iota(jnp.int32, (c, LANES), 1)
    ones3 = jnp.where(lane < 3, 1.0, 0.0)
    ones3_hi = jnp.where(jnp.logical_and(lane >= 3, lane < 6), 1.0, 0.0)
    g_last = [gcols[i][0:1, :] if revs[i] else gcols[i][c - 1:c, :] for i in idx]
    egc = [jnp.exp(gcols[i]) for i in idx]
    q_state = [_mmb(qs[i] * egc[i], states[i]) for i in idx]
    diff = []
    for i in idx:
        g1 = gcols[i].astype(BF16).astype(F32)
        g2 = (gcols[i] - g1).astype(BF16).astype(F32)
        g3 = (gcols[i] - g1 - g2).astype(BF16).astype(F32)
        pieces = jnp.where(lane == 0, g1, jnp.where(lane == 1, g2, jnp.where(lane == 2, g3, 0.0)))
        xm = pieces + ones3_hi
        ym = ones3 - pltpu.roll(pieces, 3, 1)
        diff.append(_nt(xm.astype(BF16), ym.astype(BF16)))
    decay = [jnp.where(incl[revs[i]], jnp.exp(jnp.where(incl[revs[i]], diff[i], 0.0)), 0.0) for i in idx]
    kb = [ks[i] * bcols[i] for i in idx]
    k16 = [ks[i].astype(BF16) for i in idx]
    lm = [jnp.where(strict[revs[i]], _nt(kb[i].astype(BF16), k16[i]) * decay[i], 0.0) for i in idx]
    attn = [_nt(qs[i].astype(BF16), k16[i]) * decay[i] for i in idx]
    tinv = [eye - lm[i] for i in idx]
    pw = [_mmb(lm[i], lm[i]) for i in idx]
    n = 4
    while n < c:
        both = [_mmb(jnp.concatenate([tinv[i], pw[i]], axis=0), pw[i]) for i in idx]
        tinv = [tinv[i] + both[i][:c] for i in idx]
        pw = [both[i][c:] for i in idx]
        n *= 2
    tinv = [tinv[i] + _mmb(tinv[i], pw[i]) for i in idx]
    u = [_mmb(tinv[i], jnp.concatenate([vs[i] * bcols[i], kb[i] * egc[i]], axis=1)) for i in idx]
    v_new = [u[i][:, :HEAD_DIM] - _mmb(u[i][:, HEAD_DIM:], states[i]) for i in idx]
    outs = [q_state[i] + _mmb(attn[i], v_new[i]) for i in idx]
    k_e = [ks[i] * jnp.exp(g_last[i] - gcols[i]) for i in idx]
    new_states = [states[i] * jnp.exp(g_last[i]) + _tn(k_e[i].astype(BF16), v_new[i].astype(BF16)) for i in idx]
    return outs, new_states


def _gdn_kernel(qf_ref, kf_ref, vf_ref, bf_ref, gf_ref, qb_ref, kb_ref, vb_ref, bb_ref, gb_ref,
                of_ref, ob_ref, state_ref, *, chunks):
    @pl.when(pl.program_id(1) == 0)
    def _():
        state_ref[...] = jnp.zeros_like(state_ref)

    nh = GDN_HEADS

    def body(it, carry):
        rows_f = pl.ds(pl.multiple_of(it * GDN_CHUNK, GDN_CHUNK), GDN_CHUNK)
        rows_b = pl.ds(pl.multiple_of((chunks - 1 - it) * GDN_CHUNK, GDN_CHUNK), GDN_CHUNK)
        tabs = ((qf_ref, kf_ref, vf_ref, bf_ref[rows_f, :], gf_ref[rows_f, :], rows_f, 0),
                (qb_ref, kb_ref, vb_ref, bb_ref[rows_b, :], gb_ref[rows_b, :], rows_b, nh))
        qs, ks, vs, bcols, gcols, states, revs = [], [], [], [], [], [], []
        for d, (q_ref, k_ref, v_ref, bt, gt, rows, lane0) in enumerate(tabs):
            for h in range(nh):
                qs.append(q_ref[h, rows, :].astype(F32))
                ks.append(k_ref[h, rows, :].astype(F32))
                vs.append(v_ref[h, rows, :].astype(F32))
                bcols.append(bt[:, lane0 + h:lane0 + h + 1])
                gcols.append(gt[:, 2 * nh + lane0 + h:2 * nh + lane0 + h + 1])
                states.append(state_ref[d * nh + h])
                revs.append(d == 1)
        outs, new_states = _gdn_chunks(qs, ks, vs, bcols, gcols, states, revs)
        for d, o_ref in enumerate((of_ref, ob_ref)):
            for h in range(nh):
                state_ref[d * nh + h] = new_states[d * nh + h]
                o_ref[h, tabs[d][5], :] = outs[d * nh + h].astype(o_ref.dtype)
        return carry

    lax.fori_loop(0, chunks, body, 0)


def gdn_scan(qkv_h, beta, gc, batch, seq):
    t = batch * seq
    rows = _tile(seq, 512)
    nblk = seq // rows
    fwd = lambda b, ib: b * nblk + ib
    bwd = lambda b, ib: b * nblk + nblk - 1 - ib

    def specs(blk):
        qspec = lambda part: pl.BlockSpec((GDN_HEADS, rows, HEAD_DIM), lambda b, ib: (part, blk(b, ib), 0))
        gspec = pl.BlockSpec((rows, LANES), lambda b, ib: (blk(b, ib), 0))
        return [qspec(0), qspec(1), qspec(2), gspec, gspec]

    ospec = lambda blk: pl.BlockSpec((GDN_HEADS, rows, HEAD_DIM), lambda b, ib: (0, blk(b, ib), 0))
    out = jax.ShapeDtypeStruct((GDN_HEADS, t, HEAD_DIM), BF16)
    return pl.pallas_call(
        functools.partial(_gdn_kernel, chunks=rows // GDN_CHUNK),
        out_shape=(out, out),
        grid=(batch, nblk),
        in_specs=specs(fwd) + specs(bwd),
        out_specs=(ospec(fwd), ospec(bwd)),
        scratch_shapes=[pltpu.VMEM((2 * GDN_HEADS, HEAD_DIM, HEAD_DIM), F32)],
        compiler_params=_cp("parallel", "arbitrary"),
        name="gdn_scan",
    )(qkv_h, qkv_h, qkv_h, beta, gc, qkv_h, qkv_h, qkv_h, beta, gc)


def _gdn_post_kernel(of_ref, ob_ref, z_ref, g_ref, o_ref):
    o = of_ref[...].astype(F32) + ob_ref[...].astype(F32)
    r = lax.rsqrt(jnp.mean(o * o, axis=-1, keepdims=True) + EPS)
    z = z_ref[...].astype(F32)
    o_ref[...] = (o * r * g_ref[...] * (z * jax.nn.sigmoid(z))).astype(o_ref.dtype)


def gdn_post(o_f, o_b, z, norm_g):
    nhd, t, _ = o_f.shape
    ts = _tile(t, 1024)
    hspec = pl.BlockSpec((None, ts, HEAD_DIM), lambda i, h: (h, i, 0))
    tspec = pl.BlockSpec((ts, HEAD_DIM), lambda i, h: (i, h))
    return pl.pallas_call(
        _gdn_post_kernel,
        out_shape=jax.ShapeDtypeStruct((t, nhd * HEAD_DIM), BF16),
        grid=(t // ts, nhd),
        in_specs=[hspec, hspec, tspec, pl.BlockSpec((1, HEAD_DIM), lambda i, h: (0, 0))],
        out_specs=tspec,
        compiler_params=_cp("parallel", "arbitrary"),
        name="gdn_post",
    )(o_f, o_b, z, norm_g.reshape(1, HEAD_DIM).astype(F32))


def _mem_attn_kernel(q_ref, k_ref, v_ref, o_ref):
    scale = MEM_HEAD_DIM ** -0.5
    for h in range(MEM_HEADS):
        cols = slice(h * MEM_HEAD_DIM, (h + 1) * MEM_HEAD_DIM)
        s = _nt(q_ref[:, cols], k_ref[:, cols]) * scale
        p = jnp.exp(s - jnp.max(s, axis=-1, keepdims=True))
        p = p / jnp.sum(p, axis=-1, keepdims=True)
        o_ref[:, cols] = jnp.dot(p.astype(BF16), v_ref[:, cols], preferred_element_type=F32).astype(o_ref.dtype)


def mem_attention(mq, kv, batch, seq, mem_tokens):
    tq = _tile(seq, 512)
    per = seq // tq
    return pl.pallas_call(
        _mem_attn_kernel,
        out_shape=jax.ShapeDtypeStruct(mq.shape, BF16),
        grid=(batch * per,),
        in_specs=[
            pl.BlockSpec((tq, MEM_WIDTH), lambda i: (i, 0)),
            pl.BlockSpec((mem_tokens, MEM_WIDTH), lambda i: (i // per, 0)),
            pl.BlockSpec((mem_tokens, MEM_WIDTH), lambda i: (i // per, 1)),
        ],
        out_specs=pl.BlockSpec((tq, MEM_WIDTH), lambda i: (i, 0)),
        compiler_params=_cp("parallel"),
        name="mem_attention",
    )(mq, kv, kv)


def _merge_kernel(oa_ref, ob_ref, om_ref, wa_ref, wb_ref, wm_ref, g0_ref, g1_ref, g2_ref, o_ref):
    ya = jnp.dot(oa_ref[...], wa_ref[...], preferred_element_type=F32)
    yb = jnp.dot(ob_ref[...], wb_ref[...], preferred_element_type=F32)
    ym = jnp.dot(om_ref[...], wm_ref[...], preferred_element_type=F32)
    mix = g0_ref[...].astype(F32) * ya + g1_ref[...].astype(F32) * yb + g2_ref[...].astype(F32) * ym
    o_ref[...] = mix.astype(o_ref.dtype)


def gated_merge(o_a, o_b, o_m, w_a, w_b, w_m, gates):
    t = o_a.shape[0]
    d = w_a.shape[1]
    tm, tn = _tile(t, 1024), _tile(d, 512)
    nj = d // tn
    act = lambda o: pl.BlockSpec((tm, o.shape[1]), lambda i, j: (i, 0))
    wsp = lambda w: pl.BlockSpec((w.shape[0], tn), lambda i, j: (0, j))
    gsp = lambda br: pl.BlockSpec((tm, tn), lambda i, j: (i, br * nj + j))
    return pl.pallas_call(
        _merge_kernel,
        out_shape=jax.ShapeDtypeStruct((t, d), BF16),
        grid=(t // tm, nj),
        in_specs=[act(o_a), act(o_b), act(o_m), wsp(w_a), wsp(w_b), wsp(w_m), gsp(0), gsp(1), gsp(2)],
        out_specs=pl.BlockSpec((tm, tn), lambda i, j: (i, j)),
        compiler_params=_cp("parallel", "arbitrary"),
        name="gated_merge",
    )(o_a, o_b, o_m, w_a, w_b, w_m, gates, gates, gates)


MOE_ROWS_PER_STEP = 128
ROUTE_ROWS = SUBLANES + N_EXPERTS


def _route_kernel(x_ref, g_ref, w_ref, b_ref, xn_ref, eid_ref, wt_ref):
    x = x_ref[...]
    r = lax.rsqrt(jnp.mean(x * x, axis=-1, keepdims=True) + EPS)
    xn = x * r * g_ref[...]
    xn_ref[...] = xn
    tm = x.shape[0]
    lg = lax.dot_general(w_ref[...], xn, (((1,), (1,)), ((), ())), preferred_element_type=F32,
                         precision=lax.Precision.HIGHEST) + b_ref[...]
    gl = [lg[i:i + 1, :] for i in range(N_GROUPS)]
    gmax = functools.reduce(jnp.maximum, gl)
    grp = jnp.full((1, tm), N_GROUPS - 1, jnp.int32)
    for i in range(N_GROUPS - 2, -1, -1):
        grp = jnp.where(gl[i] == gmax, i, grp)
    p_grp = 1.0 / functools.reduce(jnp.add, [jnp.exp(v - gmax) for v in gl])
    sel = lg[SUBLANES + (N_GROUPS - 1) * EXPERTS_PER_GROUP:SUBLANES + N_GROUPS * EXPERTS_PER_GROUP, :]
    for i in range(N_GROUPS - 2, -1, -1):
        lo = SUBLANES + i * EXPERTS_PER_GROUP
        sel = jnp.where(grp == i, lg[lo:lo + EXPERTS_PER_GROUP, :], sel)
    rowi = lax.broadcasted_iota(jnp.int32, (EXPERTS_PER_GROUP, tm), 0)
    v1 = jnp.max(sel, axis=0, keepdims=True)
    i1 = jnp.min(jnp.where(sel == v1, rowi, EXPERTS_PER_GROUP), axis=0, keepdims=True)
    sel2 = jnp.where(rowi == i1, -jnp.inf, sel)
    v2 = jnp.max(sel2, axis=0, keepdims=True)
    i2 = jnp.min(jnp.where(sel2 == v2, rowi, EXPERTS_PER_GROUP), axis=0, keepdims=True)
    e2 = jnp.exp(v2 - v1)
    w1 = p_grp / (1.0 + e2)
    eid_ref[0:1, :] = grp * EXPERTS_PER_GROUP + i1
    eid_ref[1:2, :] = grp * EXPERTS_PER_GROUP + i2
    wt_ref[0:1, :] = w1
    wt_ref[1:2, :] = w1 * e2


def moe_route(x, g, w_grp, b_grp, w_exp):
    t, d = x.shape
    tm = _tile(t, 512)
    w = jnp.zeros((ROUTE_ROWS, d), F32).at[:N_GROUPS].set(w_grp.T).at[SUBLANES:].set(w_exp.T)
    b = jnp.zeros((ROUTE_ROWS, 1), F32).at[:N_GROUPS, 0].set(b_grp.astype(F32))
    return pl.pallas_call(
        _route_kernel,
        out_shape=(jax.ShapeDtypeStruct((t, d), F32), jax.ShapeDtypeStruct((EXPERT_TOPK, t), jnp.int32),
                   jax.ShapeDtypeStruct((EXPERT_TOPK, t), F32)),
        grid=(t // tm,),
        in_specs=[pl.BlockSpec((tm, d), lambda i: (i, 0)), pl.BlockSpec((1, d), lambda i: (0, 0)),
                  pl.BlockSpec((ROUTE_ROWS, d), lambda i: (0, 0)), pl.BlockSpec((ROUTE_ROWS, 1), lambda i: (0, 0))],
        out_specs=(pl.BlockSpec((tm, d), lambda i: (i, 0)), pl.BlockSpec((EXPERT_TOPK, tm), lambda i: (0, i)),
                   pl.BlockSpec((EXPERT_TOPK, tm), lambda i: (0, i))),
        compiler_params=_cp("parallel"),
        name="moe_route",
    )(x, g.reshape(1, d), w, b)


def _rank_kernel(e_ref, rank_ref, cnt_ref, *, tb):
    @pl.when(pl.program_id(0) == 0)
    def _():
        cnt_ref[...] = jnp.zeros_like(cnt_ref)

    e = e_ref[...]
    onehot = jnp.where(lax.broadcasted_iota(jnp.int32, (N_EXPERTS, tb), 0) == e, 1.0, 0.0)
    earlier = jnp.where(lax.broadcasted_iota(jnp.int32, (tb, tb), 0) < lax.broadcasted_iota(jnp.int32, (tb, tb), 1),
                        1.0, 0.0)
    before = _mmb(onehot, earlier) + cnt_ref[...]
    rank_ref[...] = jnp.sum(onehot * before, axis=0, keepdims=True).astype(jnp.int32)
    cnt_ref[...] = cnt_ref[...] + jnp.sum(onehot, axis=1, keepdims=True)


def moe_rank(e_flat):
    n = e_flat.shape[1]
    tb = _tile(n, 512)
    return pl.pallas_call(
        functools.partial(_rank_kernel, tb=tb),
        out_shape=(jax.ShapeDtypeStruct((1, n), jnp.int32), jax.ShapeDtypeStruct((N_EXPERTS, 1), F32)),
        grid=(n // tb,),
        in_specs=[pl.BlockSpec((1, tb), lambda i: (0, i))],
        out_specs=(pl.BlockSpec((1, tb), lambda i: (0, i)), pl.BlockSpec((N_EXPERTS, 1), lambda i: (0, 0))),
        compiler_params=_cp("arbitrary"),
        name="moe_rank",
    )(e_flat)


def _dispatch_kernel(dest_ref, cnt_ref, pstart_ref, xn_ref, xs_ref, xbuf_ref, zero_ref, load_sem, scat_sem, fill_sem,
                     *, tb, tokens, nblk):
    i, nsteps = pl.program_id(0), pl.num_programs(0)
    slot = i % 2

    def load(step, sl):
        return pltpu.make_async_copy(xn_ref.at[pl.ds(step * tb, tb)], xbuf_ref.at[sl], load_sem.at[sl])

    def scatters(step, sl):
        return [pltpu.make_async_copy(xbuf_ref.at[sl, pl.ds(j, 1)],
                                      xs_ref.at[pl.ds(dest_ref[kk * tokens + step * tb + j], 1)], scat_sem.at[sl])
                for j in range(tb) for kk in range(EXPERT_TOPK)]

    def zero_rows(dst_row, nrows):
        return pltpu.make_async_copy(zero_ref.at[pl.ds(0, nrows)], xs_ref.at[pl.ds(dst_row, nrows)], fill_sem)

    @pl.when(i == 0)
    def _():
        load(0, 0).start()
        zero_ref[...] = jnp.zeros_like(zero_ref)
        last = N_EXPERTS - 1
        n_used = (pstart_ref[last] + cnt_ref[last] + MOE_BLOCK - 1) // MOE_BLOCK

        def fill_block(blk, carry):
            zero_rows(blk * MOE_BLOCK, MOE_BLOCK).start()
            zero_rows(blk * MOE_BLOCK, MOE_BLOCK).wait()
            return carry

        lax.fori_loop(n_used, nblk, fill_block, 0)

        def per_expert(e, carry):
            npad = (-cnt_ref[e]) % MOE_BLOCK
            base = pstart_ref[e] + cnt_ref[e]

            def fill(r, c2):
                zero_rows(base + r, 1).start()
                return c2

            lax.fori_loop(0, npad, fill, 0)

            def drain(r, c2):
                zero_rows(base + r, 1).wait()
                return c2

            lax.fori_loop(0, npad, drain, 0)
            return carry

        lax.fori_loop(0, N_EXPERTS, per_expert, 0)

    @pl.when(i >= 1)
    def _():
        for cp in scatters(i - 1, 1 - slot):
            cp.wait()

    @pl.when(i + 1 < nsteps)
    def _():
        load(i + 1, 1 - slot).start()

    load(i, slot).wait()
    for cp in scatters(i, slot):
        cp.start()

    @pl.when(i == nsteps - 1)
    def _():
        for cp in scatters(i, slot):
            cp.wait()


def moe_dispatch(xn, dest_flat, counts, pad_start, n_slots):
    t, d = xn.shape
    tb = _tile(t, MOE_ROWS_PER_STEP)
    return pl.pallas_call(
        functools.partial(_dispatch_kernel, tb=tb, tokens=t, nblk=n_slots // MOE_BLOCK),
        out_shape=jax.ShapeDtypeStruct((n_slots, d), xn.dtype),
        grid_spec=pltpu.PrefetchScalarGridSpec(
            num_scalar_prefetch=3,
            grid=(t // tb,),
            in_specs=[pl.BlockSpec(memory_space=pl.ANY)],
            out_specs=pl.BlockSpec(memory_space=pl.ANY),
            scratch_shapes=[pltpu.VMEM((2, tb, d), xn.dtype), pltpu.VMEM((MOE_BLOCK, d), xn.dtype),
                            pltpu.SemaphoreType.DMA((2,)), pltpu.SemaphoreType.DMA((2,)), pltpu.SemaphoreType.DMA],
        ),
        compiler_params=pltpu.CompilerParams(dimension_semantics=("arbitrary",), has_side_effects=True,
                                             vmem_limit_bytes=VMEM_LIMIT),
        name="moe_dispatch",
    )(dest_flat, counts, pad_start, xn)


def _expert_kernel(be_ref, nused_ref, x_ref, wg_ref, wu_ref, wd_ref, y_ref):
    @pl.when(pl.program_id(0) < nused_ref[0])
    def _():
        x = x_ref[...].astype(BF16)
        hg = jnp.dot(x, wg_ref[...], preferred_element_type=F32)
        hu = jnp.dot(x, wu_ref[...], preferred_element_type=F32)
        hmid = (hg * jax.nn.sigmoid(hg) * hu).astype(BF16)
        y_ref[...] = jnp.dot(hmid, wd_ref[...], preferred_element_type=F32).astype(y_ref.dtype)

    @pl.when(pl.program_id(0) >= nused_ref[0])
    def _():
        y_ref[...] = jnp.zeros_like(y_ref)


def moe_experts(xs, blk_expert, n_used, w_gate, w_up, w_down):
    p, d = xs.shape
    de = w_gate.shape[2]
    nblk = p // MOE_BLOCK
    row = lambda i, be, nu: (jnp.minimum(i, nu[0] - 1), 0)
    wsel = lambda i, be, nu: (be[jnp.minimum(i, nu[0] - 1)], 0, 0)
    return pl.pallas_call(
        _expert_kernel,
        out_shape=jax.ShapeDtypeStruct((p, d), F32),
        grid_spec=pltpu.PrefetchScalarGridSpec(
            num_scalar_prefetch=2,
            grid=(nblk,),
            in_specs=[pl.BlockSpec((MOE_BLOCK, d), row), pl.BlockSpec((None, d, de), wsel),
                      pl.BlockSpec((None, d, de), wsel), pl.BlockSpec((None, de, d), wsel)],
            out_specs=pl.BlockSpec((MOE_BLOCK, d), lambda i, be, nu: (i, 0)),
        ),
        compiler_params=_cp("arbitrary"),
        name="moe_experts",
    )(blk_expert, n_used, xs, w_gate, w_up, w_down)


def _combine_kernel(dest_ref, x_ref, wt_ref, g_ref, ys_ref, o_ref, y_ref, sems, *, tb, tokens):
    i, nsteps = pl.program_id(0), pl.num_programs(0)
    slot = i % 2

    def gathers(step, sl):
        return [pltpu.make_async_copy(ys_ref.at[pl.ds(dest_ref[kk * tokens + step * tb + j], 1)],
                                      y_ref.at[sl, kk, pl.ds(j, 1)], sems.at[sl])
                for j in range(tb) for kk in range(EXPERT_TOPK)]

    @pl.when(i == 0)
    def _():
        for cp in gathers(0, 0):
            cp.start()

    @pl.when(i + 1 < nsteps)
    def _():
        for cp in gathers(i + 1, 1 - slot):
            cp.start()

    for cp in gathers(i, slot):
        cp.wait()
    h = x_ref[...] + wt_ref[:, 0:1] * y_ref[slot, 0] + wt_ref[:, 1:2] * y_ref[slot, 1]
    r = lax.rsqrt(jnp.mean(h * h, axis=-1, keepdims=True) + EPS)
    o_ref[...] = h * r * g_ref[...]


def moe_combine(x, ys, dest_flat, wts_tok, g_final):
    t, d = x.shape
    tb = _tile(t, MOE_ROWS_PER_STEP)
    return pl.pallas_call(
        functools.partial(_combine_kernel, tb=tb, tokens=t),
        out_shape=jax.ShapeDtypeStruct((t, d), F32),
        grid_spec=pltpu.PrefetchScalarGridSpec(
            num_scalar_prefetch=1,
            grid=(t // tb,),
            in_specs=[pl.BlockSpec((tb, d), lambda i, dr: (i, 0)),
                      pl.BlockSpec((tb, EXPERT_TOPK), lambda i, dr: (i, 0)),
                      pl.BlockSpec((1, d), lambda i, dr: (0, 0)),
                      pl.BlockSpec(memory_space=pl.ANY)],
            out_specs=pl.BlockSpec((tb, d), lambda i, dr: (i, 0)),
            scratch_shapes=[pltpu.VMEM((2, EXPERT_TOPK, tb, d), F32), pltpu.SemaphoreType.DMA((2,))],
        ),
        compiler_params=_cp("arbitrary"),
        name="moe_combine",
    )(dest_flat, x, wts_tok, g_final.reshape(1, d), ys)


def hier_moe_final(x2, g_ffn, w_rg, b_rg, w_re, w_gate, w_up, w_down, g_final):
    t, d = x2.shape
    xn, eid, wts = moe_route(x2, g_ffn, w_rg, b_rg, w_re)
    n = EXPERT_TOPK * t
    e_flat = eid.reshape(1, n)
    rank, counts_f = moe_rank(e_flat)
    counts = counts_f.reshape(N_EXPERTS).astype(jnp.int32)
    padded = (counts + MOE_BLOCK - 1) // MOE_BLOCK * MOE_BLOCK
    pad_end = jnp.cumsum(padded)
    pad_start = pad_end - padded
    dest = (jnp.take(pad_start, e_flat[0]) + rank[0]).astype(jnp.int32)
    nblk = -(-n // MOE_BLOCK) + N_EXPERTS
    blk_start = jnp.arange(nblk, dtype=jnp.int32) * MOE_BLOCK
    blk_expert = jnp.minimum(jnp.sum(pad_end[None, :] <= blk_start[:, None], axis=1), N_EXPERTS - 1).astype(jnp.int32)
    n_used = (pad_end[-1:] // MOE_BLOCK).astype(jnp.int32)
    xs = moe_dispatch(xn, dest, counts, pad_start.astype(jnp.int32), nblk * MOE_BLOCK)
    ys = moe_experts(xs, blk_expert, n_used, w_gate.astype(BF16), w_up.astype(BF16), w_down.astype(BF16))
    return moe_combine(x2, ys, dest, wts.T, g_final)


def kernel(x, mem, g_mix, w_in, b_gate, gdn_conv, gdn_a_log, gdn_dt_bias, gdn_norm_g, g_mem, w_mem_kv, w_o_swa, w_o_gdn, w_o_mem, w_out, g_ffn, w_route_group, b_route_group, w_route_expert, w_expert_gate, w_expert_up, w_expert_down, g_final):
    batch, seq, d = x.shape
    depth = w_in.shape[0]
    assert depth == 1, "the final RMSNorm is fused into the layer's MoE combine"
    t = batch * seq
    mem_tokens = mem.shape[1]
    h = x.reshape(t, d)
    o0 = 3 * SWA_WIDTH
    o1 = o0 + 3 * GDN_WIDTH
    o2 = o1 + GDN_WIDTH
    o3 = o2 + 4 * GDN_HEADS
    o4 = o3 + MEM_WIDTH
    tables = rope_tables(seq)
    for l in range(depth):
        w = w_in[l].astype(BF16)
        a = rmsnorm_rows(h, g_mix[l], BF16)
        qkv_a = matmul(a, w, F32, n=o0, rope=(tables, seq))
        o_a = dilated_swa(qkv_a, batch, seq)
        qkv_b = matmul_conv_norm(a, w, o0, gdn_conv[l], seq)
        z_b = matmul(a, w, BF16, n=GDN_WIDTH, col_off=o1)
        w_ba = jnp.zeros((d, LANES), BF16).at[:, :o3 - o2].set(w[:, o2:o3])
        beta, gc = gdn_gates(matmul(a, w_ba, F32), gdn_a_log[l], gdn_dt_bias[l])
        o_f, o_r = gdn_scan(qkv_b, beta, gc, batch, seq)
        o_b = gdn_post(o_f, o_r, z_b, gdn_norm_g[l])
        kv = matmul(rmsnorm_rows(mem.reshape(batch * mem_tokens, d), g_mem[l], BF16), w_mem_kv[l].astype(BF16), BF16)
        o_m = mem_attention(matmul(a, w[:, o3:o4], BF16), kv, batch, seq, mem_tokens)
        gates = matmul(a, w[:, o4:], BF16, bias=b_gate[l].reshape(-1))
        mix = gated_merge(o_a, o_b, o_m, w_o_swa[l].astype(BF16), w_o_gdn[l].astype(BF16),
                          w_o_mem[l].astype(BF16), gates)
        x2 = matmul(mix, w_out[l].astype(BF16), F32, residual=h)
        h = hier_moe_final(x2, g_ffn[l], w_route_group[l], b_route_group[l], w_route_expert[l],
                           w_expert_gate[l], w_expert_up[l], w_expert_down[l], g_final)
    return h.reshape(batch, seq, d)
```

```python
import functools

import jax
import jax.numpy as jnp
from jax import lax
from jax.experimental import pallas as pl
from jax.experimental.pallas import tpu as pltpu

F32 = jnp.float32
BF16 = jnp.bfloat16

HEAD_DIM = 128
SWA_GROUPS = ((128, 1), (512, 4), (2048, 16))
SWA_HEADS_PER_GROUP = 4
SWA_HEADS = SWA_HEADS_PER_GROUP * len(SWA_GROUPS)
SWA_WIDTH = SWA_HEADS * HEAD_DIM
SWA_BLOCK = 64
SWA_UNROLL = 4
ROPE_THETA = 500000.0
ROPE_DIMS = HEAD_DIM // 4
GDN_HEADS = 12
GDN_WIDTH = GDN_HEADS * HEAD_DIM
GDN_CONV = 5
GDN_CHUNK = 64
MEM_HEADS = 4
MEM_HEAD_DIM = 256
MEM_WIDTH = MEM_HEADS * MEM_HEAD_DIM
N_BRANCH = 3
N_GROUPS = 4
EXPERTS_PER_GROUP = 8
N_EXPERTS = N_GROUPS * EXPERTS_PER_GROUP
EXPERT_TOPK = 2
MOE_BLOCK = 256
EPS = 1e-6
NEG_INF = -1e30

LANES = 128
SUBLANES = 8
VMEM_LIMIT = 48 * 1024 * 1024


def _cp(*sem, vmem=VMEM_LIMIT):
    return pltpu.CompilerParams(dimension_semantics=sem, vmem_limit_bytes=vmem)


def _tile(n, pref):
    t = min(n, pref)
    while n % t:
        t //= 2
    return t


def _rmsnorm_kernel(x_ref, g_ref, o_ref):
    x = x_ref[...].astype(F32)
    r = lax.rsqrt(jnp.mean(x * x, axis=-1, keepdims=True) + EPS)
    o_ref[...] = (x * r * g_ref[...]).astype(o_ref.dtype)


def rmsnorm_rows(x, g, out_dtype):
    m, d = x.shape
    tm = _tile(m, 512)
    return pl.pallas_call(
        _rmsnorm_kernel,
        out_shape=jax.ShapeDtypeStruct((m, d), out_dtype),
        grid=(m // tm,),
        in_specs=[pl.BlockSpec((tm, d), lambda i: (i, 0)), pl.BlockSpec((1, d), lambda i: (0, 0))],
        out_specs=pl.BlockSpec((tm, d), lambda i: (i, 0)),
        compiler_params=_cp("parallel"),
        name="rmsnorm_rows",
    )(x, g.reshape(1, d))


def _mm_kernel(a_ref, w_ref, o_ref):
    o_ref[...] = jnp.dot(a_ref[...], w_ref[...], preferred_element_type=F32).astype(o_ref.dtype)


def _mm_sigmoid_kernel(a_ref, w_ref, b_ref, o_ref):
    z = jnp.dot(a_ref[...], w_ref[...], preferred_element_type=F32) + b_ref[...]
    o_ref[...] = jax.nn.sigmoid(z).astype(o_ref.dtype)


def _mm_residual_kernel(a_ref, w_ref, r_ref, o_ref):
    o_ref[...] = r_ref[...] + jnp.dot(a_ref[...], w_ref[...], preferred_element_type=F32)


EPI_COLS = 2 * HEAD_DIM


def _mm_rope_kernel(a_ref, w_ref, c_ref, s1_ref, s2_ref, o_ref):
    c, s1, s2 = c_ref[...], s1_ref[...], s2_ref[...]
    half = ROPE_DIMS // 2
    for c0 in range(0, o_ref.shape[1], EPI_COLS):
        acc = jnp.dot(a_ref[...], w_ref[:, c0:c0 + EPI_COLS], preferred_element_type=F32)
        for h0 in range(0, EPI_COLS, HEAD_DIM):
            xh = acc[:, h0:h0 + HEAD_DIM]
            o_ref[:, c0 + h0:c0 + h0 + HEAD_DIM] = (
                xh * c + pltpu.roll(xh, HEAD_DIM - half, 1) * s1 + pltpu.roll(xh, half, 1) * s2)


MM_TM = 1024
MM_TN = (1536, 1024, 512)


def _col_tile(n):
    return next((t for t in MM_TN if n % t == 0), n)


def matmul(a, w, out_dtype, *, n=None, col_off=0, bias=None, residual=None, rope=None):
    m, k = a.shape
    n = w.shape[1] if n is None else n
    tm, tn = _tile(m, MM_TM), _col_tile(n)
    assert col_off % tn == 0
    joff = col_off // tn
    a_spec = pl.BlockSpec((tm, k), lambda i, j: (i, 0))
    w_spec = pl.BlockSpec((k, tn), lambda i, j: (0, joff + j))
    o_spec = pl.BlockSpec((tm, tn), lambda i, j: (i, j))
    if bias is not None:
        kern, extra, extra_specs = _mm_sigmoid_kernel, (bias.reshape(1, n),), [pl.BlockSpec((1, tn), lambda i, j: (0, j))]
    elif residual is not None:
        kern, extra, extra_specs = _mm_residual_kernel, (residual,), [o_spec]
    elif rope is not None:
        tables, seq = rope
        assert tn == SWA_WIDTH and n == 3 * SWA_WIDTH
        per = seq // tm
        t_spec = pl.BlockSpec((None, tm, HEAD_DIM), lambda i, j: (j, i % per, 0))
        kern = _mm_rope_kernel
        extra, extra_specs = tuple(tables), [t_spec] * 3
    else:
        kern, extra, extra_specs = _mm_kernel, (), []
    return pl.pallas_call(
        kern,
        out_shape=jax.ShapeDtypeStruct((m, n), out_dtype),
        grid=(m // tm, n // tn),
        in_specs=[a_spec, w_spec] + extra_specs,
        out_specs=o_spec,
        compiler_params=_cp("parallel", "arbitrary"),
        name="matmul",
    )(a, w, *extra)


def rope_tables(seq):
    half = ROPE_DIMS // 2
    inv = ROPE_THETA ** (-jnp.arange(half, dtype=F32) / half)
    ang = jnp.arange(seq, dtype=F32)[:, None] * inv[None, :]
    cos, sin = jnp.cos(ang), jnp.sin(ang)
    zeros = jnp.zeros((seq, HEAD_DIM - ROPE_DIMS), F32)
    zh = jnp.zeros((seq, half), F32)
    c = jnp.concatenate([cos, cos, zeros + 1.0], axis=1)
    s1 = jnp.concatenate([-sin, zh, zeros], axis=1)
    s2 = jnp.concatenate([zh, sin, zeros], axis=1)
    ident = (jnp.ones_like(c), jnp.zeros_like(c), jnp.zeros_like(c))
    return tuple(jnp.stack([tab * (HEAD_DIM ** -0.5), tab, idt]) for tab, idt in zip((c, s1, s2), ident))


CONV_HALO = 16


def _mm_conv_kernel(a_ref, ap_ref, an_ref, w_ref, cw_ref, o_ref, buf_ref, *, tm, tiles_per_seq):
    i, kind = pl.program_id(0), pl.program_id(1)
    halo = CONV_HALO
    rows = tm + 2 * halo
    keep_prev = jnp.where((i % tiles_per_seq) == 0, 0.0, 1.0)
    keep_next = jnp.where((i % tiles_per_seq) == tiles_per_seq - 1, 0.0, 1.0)
    qk_mul = jnp.where(kind == 0, HEAD_DIM ** -0.5, 1.0)
    w = w_ref[...]
    buf_ref[0:halo, :] = jnp.dot(ap_ref[...], w, preferred_element_type=F32) * keep_prev
    buf_ref[halo:halo + tm, :] = jnp.dot(a_ref[...], w, preferred_element_type=F32)
    buf_ref[halo + tm:, :] = jnp.dot(an_ref[...], w, preferred_element_type=F32) * keep_next
    for h in range(GDN_HEADS):
        cols = slice(h * HEAD_DIM, (h + 1) * HEAD_DIM)
        xh = buf_ref[:, cols]
        acc = xh * cw_ref[GDN_CONV // 2:GDN_CONV // 2 + 1, cols]
        for t in range(GDN_CONV):
            if t != GDN_CONV // 2:
                acc = acc + pltpu.roll(xh, (GDN_CONV // 2 - t) % rows, 0) * cw_ref[t:t + 1, cols]
        acc = acc[halo:halo + tm, :]
        y = acc * jax.nn.sigmoid(acc)
        inv = lax.rsqrt(jnp.sum(y * y, axis=-1, keepdims=True) + EPS) * qk_mul
        o_ref[h] = (y * jnp.where(kind < 2, inv, 1.0)).astype(o_ref.dtype)


def matmul_conv_norm(a, w, col_off, conv_w, seq):
    m, k = a.shape
    tn = GDN_WIDTH
    tm = _tile(seq, MM_TM)
    assert col_off % tn == 0 and tm % CONV_HALO == 0
    joff = col_off // tn
    per_h = tm // CONV_HALO
    nblk_h = m // CONV_HALO
    return pl.pallas_call(
        functools.partial(_mm_conv_kernel, tm=tm, tiles_per_seq=seq // tm),
        out_shape=jax.ShapeDtypeStruct((3 * GDN_HEADS, m, HEAD_DIM), BF16),
        grid=(m // tm, 3),
        in_specs=[
            pl.BlockSpec((tm, k), lambda i, j: (i, 0)),
            pl.BlockSpec((CONV_HALO, k), lambda i, j: (jnp.maximum(i * per_h - 1, 0), 0)),
            pl.BlockSpec((CONV_HALO, k), lambda i, j: (jnp.minimum((i + 1) * per_h, nblk_h - 1), 0)),
            pl.BlockSpec((k, tn), lambda i, j: (0, joff + j)),
            pl.BlockSpec((GDN_CONV, tn), lambda i, j: (0, j)),
        ],
        out_specs=pl.BlockSpec((GDN_HEADS, tm, HEAD_DIM), lambda i, j: (j, i, 0)),
        scratch_shapes=[pltpu.VMEM((tm + 2 * CONV_HALO, tn), F32)],
        compiler_params=_cp("parallel", "arbitrary"),
        name="matmul_conv_norm",
    )(a, a, a, w, conv_w)


def _swa_kernel(q_ref, k_ref, v_ref, o_ref, acc_ref, m_ref, l_ref, *, seq):
    g = pl.program_id(2)

    @pl.when(g == 0)
    def _():
        acc_ref[...] = jnp.zeros_like(acc_ref)
        m_ref[...] = jnp.full_like(m_ref, NEG_INF)
        l_ref[...] = jnp.zeros_like(l_ref)

    for gi, (window, dil) in enumerate(SWA_GROUPS):
        radius = window // (2 * dil)
        assert radius <= SWA_BLOCK
        sub = seq // dil
        qb = min(128, sub)
        win = min(sub, qb + 2 * SWA_BLOCK)
        nqb = sub // qb

        @pl.when(g == gi)
        def _(dil=dil, radius=radius, sub=sub, qb=qb, win=win, nqb=nqb):
            def rows(start, size):
                return pl.ds(start, size) if dil == 1 else pl.ds(start, size, stride=dil)

            total = dil * nqb
            unroll = SWA_UNROLL if total % SWA_UNROLL == 0 else 1
            idx = range(unroll)

            def body(step, carry):
                its = [step * unroll + u for u in idx]
                qss = [(it % nqb) * qb for it in its]
                wss = [jnp.clip(qs - SWA_BLOCK, 0, sub - win) for qs in qss]
                q_rows = [rows(it // nqb + dil * qs, qb) for it, qs in zip(its, qss)]
                k_rows = [rows(it // nqb + dil * ws, win) for it, ws in zip(its, wss)]
                q = [q_ref[r, :].astype(BF16) for r in q_rows]
                k = [k_ref[r, :].astype(BF16) for r in k_rows]
                v = [v_ref[r, :].astype(BF16) for r in k_rows]
                s = [_nt(q[u], k[u]) for u in idx]
                off = lax.broadcasted_iota(jnp.int32, (qb, win), 0) - lax.broadcasted_iota(jnp.int32, (qb, win), 1)
                s = [jnp.where(jnp.abs(off + (qss[u] - wss[u])) <= radius, s[u], NEG_INF) for u in idx]
                m_old = [m_ref[r, :] for r in q_rows]
                m_new = [jnp.maximum(m_old[u], jnp.max(s[u], axis=1, keepdims=True)) for u in idx]
                p = [jnp.exp(s[u] - m_new[u]) for u in idx]
                corr = [jnp.exp(m_old[u] - m_new[u]) for u in idx]
                pv = [jnp.dot(p[u].astype(BF16), v[u], preferred_element_type=F32) for u in idx]
                for u in idx:
                    l_ref[q_rows[u], :] = l_ref[q_rows[u], :] * corr[u] + jnp.sum(p[u], axis=1, keepdims=True)
                    acc_ref[q_rows[u], :] = acc_ref[q_rows[u], :] * corr[u] + pv[u]
                    m_ref[q_rows[u], :] = m_new[u]
                return carry

            lax.fori_loop(0, total // unroll, body, 0)

    @pl.when(g == len(SWA_GROUPS) - 1)
    def _():
        o_ref[...] = (acc_ref[...] / l_ref[...]).astype(o_ref.dtype)


def dilated_swa(qkv, batch, seq):
    nh = SWA_HEADS_PER_GROUP
    qkv3 = qkv.reshape(batch, seq, 3 * SWA_WIDTH)

    def spec(off):
        return pl.BlockSpec((None, seq, HEAD_DIM), lambda b, h, g: (b, 0, off + g * nh + h))

    out = pl.pallas_call(
        functools.partial(_swa_kernel, seq=seq),
        out_shape=jax.ShapeDtypeStruct((batch, seq, nh * HEAD_DIM), BF16),
        grid=(batch, nh, len(SWA_GROUPS)),
        in_specs=[spec(0), spec(SWA_HEADS), spec(2 * SWA_HEADS)],
        out_specs=pl.BlockSpec((None, seq, HEAD_DIM), lambda b, h, g: (b, 0, h)),
        scratch_shapes=[pltpu.VMEM((seq, HEAD_DIM), F32), pltpu.VMEM((seq, 1), F32), pltpu.VMEM((seq, 1), F32)],
        compiler_params=_cp("parallel", "parallel", "arbitrary"),
        name="dilated_swa",
    )(qkv3, qkv3, qkv3)
    return out.reshape(batch * seq, nh * HEAD_DIM)


def _gdn_gate_kernel(ba_ref, alog_ref, dtb_ref, beta_ref, gc_ref, *, ts):
    x = ba_ref[...]
    nhd = 2 * GDN_HEADS
    lane = lax.broadcasted_iota(jnp.int32, (ts, LANES), 1)
    row = lax.broadcasted_iota(jnp.int32, (ts, LANES), 0) % GDN_CHUNK
    beta_ref[...] = jax.nn.sigmoid(x)
    z = x + dtb_ref[...]
    softplus = jnp.maximum(z, 0.0) + jnp.log(1.0 + jnp.exp(-jnp.abs(z)))
    g = jnp.where(jnp.logical_and(lane >= nhd, lane < 2 * nhd), -jnp.exp(alog_ref[...]) * softplus, 0.0)
    pre, suf = g, g
    s = 1
    while s < GDN_CHUNK:
        pre = pre + jnp.where(row >= s, pltpu.roll(pre, s, 0), 0.0)
        suf = suf + jnp.where(row < GDN_CHUNK - s, pltpu.roll(suf, ts - s, 0), 0.0)
        s *= 2
    gc_ref[...] = jnp.where(lane < nhd + GDN_HEADS, pre, suf)


def gdn_gates(ba, a_log, dt_bias):
    t = ba.shape[0]
    ts = _tile(t, 1024)
    nhd = 2 * GDN_HEADS
    pad = lambda v: jnp.zeros((1, LANES), F32).at[0, nhd:2 * nhd].set(v.reshape(-1).astype(F32))
    shp = jax.ShapeDtypeStruct((t, LANES), F32)
    vec = pl.BlockSpec((1, LANES), lambda i: (0, 0))
    blk = pl.BlockSpec((ts, LANES), lambda i: (i, 0))
    return pl.pallas_call(
        functools.partial(_gdn_gate_kernel, ts=ts),
        out_shape=(shp, shp),
        grid=(t // ts,),
        in_specs=[blk, vec, vec],
        out_specs=(blk, blk),
        compiler_params=_cp("parallel"),
        name="gdn_gates",
    )(ba, pad(a_log), pad(dt_bias))


def _nt(a, b):
    return lax.dot_general(a, b, (((1,), (1,)), ((), ())), preferred_element_type=F32)


def _tn(a, b):
    return lax.dot_general(a, b, (((0,), (0,)), ((), ())), preferred_element_type=F32)


def _mmb(a, b):
    return jnp.dot(a.astype(BF16), b.astype(BF16), preferred_element_type=F32)


def _gdn_chunks(qs, ks, vs, bcols, gcols, states, revs):
    c = GDN_CHUNK
    idx = range(len(qs))
    ri = lax.broadcasted_iota(jnp.int32, (c, c), 0)
    ci = lax.broadcasted_iota(jnp.int32, (c, c), 1)
    strict = {False: ri > ci, True: ri < ci}
    incl = {False: ri >= ci, True: ri <= ci}
    eye = jnp.where(ri == ci, 1.0, 0.0)
    lane = lax.broadcasted_iota(jnp.int32, (c, LANES), 1)
    ones3 = jnp.where(lane < 3, 1.0, 0.0)
    ones3_hi = jnp.where(jnp.logical_and(lane >= 3, lane < 6), 1.0, 0.0)
    g_last = [gcols[i][0:1, :] if revs[i] else gcols[i][c - 1:c, :] for i in idx]
    egc = [jnp.exp(gcols[i]) for i in idx]
    q_state = [_mmb(qs[i] * egc[i], states[i]) for i in idx]
    diff = []
    for i in idx:
        g1 = gcols[i].astype(BF16).astype(F32)
        g2 = (gcols[i] - g1).astype(BF16).astype(F32)
        g3 = (gcols[i] - g1 - g2).astype(BF16).astype(F32)
        pieces = jnp.where(lane == 0, g1, jnp.where(lane == 1, g2, jnp.where(lane == 2, g3, 0.0)))
        xm = pieces + ones3_hi
        ym = ones3 - pltpu.roll(pieces, 3, 1)
        diff.append(_nt(xm.astype(BF16), ym.astype(BF16)))
    decay = [jnp.where(incl[revs[i]], jnp.exp(jnp.where(incl[revs[i]], diff[i], 0.0)), 0.0) for i in idx]
    kb = [ks[i] * bcols[i] for i in idx]
    k16 = [ks[i].astype(BF16) for i in idx]
    lm = [jnp.where(strict[revs[i]], _nt(kb[i].astype(BF16), k16[i]) * decay[i], 0.0) for i in idx]
    attn = [_nt(qs[i].astype(BF16), k16[i]) * decay[i] for i in idx]
    tinv = [eye - lm[i] for i in idx]
    pw = [_mmb(lm[i], lm[i]) for i in idx]
    n = 4
    while n < c:
        both = [_mmb(jnp.concatenate([tinv[i], pw[i]], axis=0), pw[i]) for i in idx]
        tinv = [tinv[i] + both[i][:c] for i in idx]
        pw = [both[i][c:] for i in idx]
        n *= 2
    tinv = [tinv[i] + _mmb(tinv[i], pw[i]) for i in idx]
    u = [_mmb(tinv[i], jnp.concatenate([vs[i] * bcols[i], kb[i] * egc[i]], axis=1)) for i in idx]
    v_new = [u[i][:, :HEAD_DIM] - _mmb(u[i][:, HEAD_DIM:], states[i]) for i in idx]
    outs = [q_state[i] + _mmb(attn[i], v_new[i]) for i in idx]
    k_e = [ks[i] * jnp.exp(g_last[i] - gcols[i]) for i in idx]
    new_states = [states[i] * jnp.exp(g_last[i]) + _tn(k_e[i].astype(BF16), v_new[i].astype(BF16)) for i in idx]
    return outs, new_states


def _gdn_kernel(qf_ref, kf_ref, vf_ref, bf_ref, gf_ref, qb_ref, kb_ref, vb_ref, bb_ref, gb_ref,
                of_ref, ob_ref, state_ref, *, chunks):
    @pl.when(pl.program_id(1) == 0)
    def _():
        state_ref[...] = jnp.zeros_like(state_ref)

    nh = GDN_HEADS

    def body(it, carry):
        rows_f = pl.ds(pl.multiple_of(it * GDN_CHUNK, GDN_CHUNK), GDN_CHUNK)
        rows_b = pl.ds(pl.multiple_of((chunks - 1 - it) * GDN_CHUNK, GDN_CHUNK), GDN_CHUNK)
        tabs = ((qf_ref, kf_ref, vf_ref, bf_ref[rows_f, :], gf_ref[rows_f, :], rows_f, 0),
                (qb_ref, kb_ref, vb_ref, bb_ref[rows_b, :], gb_ref[rows_b, :], rows_b, nh))
        qs, ks, vs, bcols, gcols, states, revs = [], [], [], [], [], [], []
        for d, (q_ref, k_ref, v_ref, bt, gt, rows, lane0) in enumerate(tabs):
            for h in range(nh):
                qs.append(q_ref[h, rows, :].astype(F32))
                ks.append(k_ref[h, rows, :].astype(F32))
                vs.append(v_ref[h, rows, :].astype(F32))
                bcols.append(bt[:, lane0 + h:lane0 + h + 1])
                gcols.append(gt[:, 2 * nh + lane0 + h:2 * nh + lane0 + h + 1])
                states.append(state_ref[d * nh + h])
                revs.append(d == 1)
        outs, new_states = _gdn_chunks(qs, ks, vs, bcols, gcols, states, revs)
        for d, o_ref in enumerate((of_ref, ob_ref)):
            for h in range(nh):
                state_ref[d * nh + h] = new_states[d * nh + h]
                o_ref[h, tabs[d][5], :] = outs[d * nh + h].astype(o_ref.dtype)
        return carry

    lax.fori_loop(0, chunks, body, 0)


def gdn_scan(qkv_h, beta, gc, batch, seq):
    t = batch * seq
    rows = _tile(seq, 512)
    nblk = seq // rows
    fwd = lambda b, ib: b * nblk + ib
    bwd = lambda b, ib: b * nblk + nblk - 1 - ib

    def specs(blk):
        qspec = lambda part: pl.BlockSpec((GDN_HEADS, rows, HEAD_DIM), lambda b, ib: (part, blk(b, ib), 0))
        gspec = pl.BlockSpec((rows, LANES), lambda b, ib: (blk(b, ib), 0))
        return [qspec(0), qspec(1), qspec(2), gspec, gspec]

    ospec = lambda blk: pl.BlockSpec((GDN_HEADS, rows, HEAD_DIM), lambda b, ib: (0, blk(b, ib), 0))
    out = jax.ShapeDtypeStruct((GDN_HEADS, t, HEAD_DIM), BF16)
    return pl.pallas_call(
        functools.partial(_gdn_kernel, chunks=rows // GDN_CHUNK),
        out_shape=(out, out),
        grid=(batch, nblk),
        in_specs=specs(fwd) + specs(bwd),
        out_specs=(ospec(fwd), ospec(bwd)),
        scratch_shapes=[pltpu.VMEM((2 * GDN_HEADS, HEAD_DIM, HEAD_DIM), F32)],
        compiler_params=_cp("parallel", "arbitrary"),
        name="gdn_scan",
    )(qkv_h, qkv_h, qkv_h, beta, gc, qkv_h, qkv_h, qkv_h, beta, gc)


def _gdn_post_kernel(of_ref, ob_ref, z_ref, g_ref, o_ref):
    o = of_ref[...].astype(F32) + ob_ref[...].astype(F32)
    r = lax.rsqrt(jnp.mean(o * o, axis=-1, keepdims=True) + EPS)
    z = z_ref[...].astype(F32)
    o_ref[...] = (o * r * g_ref[...] * (z * jax.nn.sigmoid(z))).astype(o_ref.dtype)


def gdn_post(o_f, o_b, z, norm_g):
    nhd, t, _ = o_f.shape
    ts = _tile(t, 1024)
    hspec = pl.BlockSpec((None, ts, HEAD_DIM), lambda i, h: (h, i, 0))
    tspec = pl.BlockSpec((ts, HEAD_DIM), lambda i, h: (i, h))
    return pl.pallas_call(
        _gdn_post_kernel,
        out_shape=jax.ShapeDtypeStruct((t, nhd * HEAD_DIM), BF16),
        grid=(t // ts, nhd),
        in_specs=[hspec, hspec, tspec, pl.BlockSpec((1, HEAD_DIM), lambda i, h: (0, 0))],
        out_specs=tspec,
        compiler_params=_cp("parallel", "arbitrary"),
        name="gdn_post",
    )(o_f, o_b, z, norm_g.reshape(1, HEAD_DIM).astype(F32))


def _mem_attn_kernel(q_ref, k_ref, v_ref, o_ref):
    scale = MEM_HEAD_DIM ** -0.5
    for h in range(MEM_HEADS):
        cols = slice(h * MEM_HEAD_DIM, (h + 1) * MEM_HEAD_DIM)
        s = _nt(q_ref[:, cols], k_ref[:, cols]) * scale
        p = jnp.exp(s - jnp.max(s, axis=-1, keepdims=True))
        p = p / jnp.sum(p, axis=-1, keepdims=True)
        o_ref[:, cols] = jnp.dot(p.astype(BF16), v_ref[:, cols], preferred_element_type=F32).astype(o_ref.dtype)


def mem_attention(mq, kv, batch, seq, mem_tokens):
    tq = _tile(seq, 512)
    per = seq // tq
    return pl.pallas_call(
        _mem_attn_kernel,
        out_shape=jax.ShapeDtypeStruct(mq.shape, BF16),
        grid=(batch * per,),
        in_specs=[
            pl.BlockSpec((tq, MEM_WIDTH), lambda i: (i, 0)),
            pl.BlockSpec((mem_tokens, MEM_WIDTH), lambda i: (i // per, 0)),
            pl.BlockSpec((mem_tokens, MEM_WIDTH), lambda i: (i // per, 1)),
        ],
        out_specs=pl.BlockSpec((tq, MEM_WIDTH), lambda i: (i, 0)),
        compiler_params=_cp("parallel"),
        name="mem_attention",
    )(mq, kv, kv)


def _merge_kernel(oa_ref, ob_ref, om_ref, wa_ref, wb_ref, wm_ref, g0_ref, g1_ref, g2_ref, o_ref):
    ya = jnp.dot(oa_ref[...], wa_ref[...], preferred_element_type=F32)
    yb = jnp.dot(ob_ref[...], wb_ref[...], preferred_element_type=F32)
    ym = jnp.dot(om_ref[...], wm_ref[...], preferred_element_type=F32)
    mix = g0_ref[...].astype(F32) * ya + g1_ref[...].astype(F32) * yb + g2_ref[...].astype(F32) * ym
    o_ref[...] = mix.astype(o_ref.dtype)


def gated_merge(o_a, o_b, o_m, w_a, w_b, w_m, gates):
    t = o_a.shape[0]
    d = w_a.shape[1]
    tm, tn = _tile(t, 1024), _tile(d, 512)
    nj = d // tn
    act = lambda o: pl.BlockSpec((tm, o.shape[1]), lambda i, j: (i, 0))
    wsp = lambda w: pl.BlockSpec((w.shape[0], tn), lambda i, j: (0, j))
    gsp = lambda br: pl.BlockSpec((tm, tn), lambda i, j: (i, br * nj + j))
    return pl.pallas_call(
        _merge_kernel,
        out_shape=jax.ShapeDtypeStruct((t, d), BF16),
        grid=(t // tm, nj),
        in_specs=[act(o_a), act(o_b), act(o_m), wsp(w_a), wsp(w_b), wsp(w_m), gsp(0), gsp(1), gsp(2)],
        out_specs=pl.BlockSpec((tm, tn), lambda i, j: (i, j)),
        compiler_params=_cp("parallel", "arbitrary"),
        name="gated_merge",
    )(o_a, o_b, o_m, w_a, w_b, w_m, gates, gates, gates)


MOE_ROWS_PER_STEP = 128
ROUTE_ROWS = SUBLANES + N_EXPERTS


def _route_kernel(x_ref, g_ref, w_ref, b_ref, xn_ref, eid_ref, wt_ref):
    x = x_ref[...]
    r = lax.rsqrt(jnp.mean(x * x, axis=-1, keepdims=True) + EPS)
    xn = x * r * g_ref[...]
    xn_ref[...] = xn
    tm = x.shape[0]
    x_hi = xn.astype(BF16)
    x_lo = (xn - x_hi.astype(F32)).astype(BF16)
    w_hi, w_lo = w_ref[0], w_ref[1]
    lg = (jnp.dot(x_hi, w_hi, preferred_element_type=F32) + jnp.dot(x_lo, w_hi, preferred_element_type=F32)
          + jnp.dot(x_hi, w_lo, preferred_element_type=F32))
    lg = lg.T[:ROUTE_ROWS, :] + b_ref[...]
    gl = [lg[i:i + 1, :] for i in range(N_GROUPS)]
    gmax = functools.reduce(jnp.maximum, gl)
    grp = jnp.full((1, tm), N_GROUPS - 1, jnp.int32)
    for i in range(N_GROUPS - 2, -1, -1):
        grp = jnp.where(gl[i] == gmax, i, grp)
    p_grp = 1.0 / functools.reduce(jnp.add, [jnp.exp(v - gmax) for v in gl])
    sel = lg[SUBLANES + (N_GROUPS - 1) * EXPERTS_PER_GROUP:SUBLANES + N_GROUPS * EXPERTS_PER_GROUP, :]
    for i in range(N_GROUPS - 2, -1, -1):
        lo = SUBLANES + i * EXPERTS_PER_GROUP
        sel = jnp.where(grp == i, lg[lo:lo + EXPERTS_PER_GROUP, :], sel)
    rowi = lax.broadcasted_iota(jnp.int32, (EXPERTS_PER_GROUP, tm), 0)
    v1 = jnp.max(sel, axis=0, keepdims=True)
    i1 = jnp.min(jnp.where(sel == v1, rowi, EXPERTS_PER_GROUP), axis=0, keepdims=True)
    sel2 = jnp.where(rowi == i1, -jnp.inf, sel)
    v2 = jnp.max(sel2, axis=0, keepdims=True)
    i2 = jnp.min(jnp.where(sel2 == v2, rowi, EXPERTS_PER_GROUP), axis=0, keepdims=True)
    e2 = jnp.exp(v2 - v1)
    w1 = p_grp / (1.0 + e2)
    eid_ref[0:1, :] = grp * EXPERTS_PER_GROUP + i1
    eid_ref[1:2, :] = grp * EXPERTS_PER_GROUP + i2
    wt_ref[0:1, :] = w1
    wt_ref[1:2, :] = w1 * e2


def moe_route(x, g, w_grp, b_grp, w_exp):
    t, d = x.shape
    tm = _tile(t, 512)
    w = jnp.zeros((d, LANES), F32).at[:, :N_GROUPS].set(w_grp).at[:, SUBLANES:ROUTE_ROWS].set(w_exp)
    w_hi = w.astype(BF16)
    w = jnp.stack([w_hi, (w - w_hi.astype(F32)).astype(BF16)])
    b = jnp.zeros((ROUTE_ROWS, 1), F32).at[:N_GROUPS, 0].set(b_grp.astype(F32))
    return pl.pallas_call(
        _route_kernel,
        out_shape=(jax.ShapeDtypeStruct((t, d), F32), jax.ShapeDtypeStruct((EXPERT_TOPK, t), jnp.int32),
                   jax.ShapeDtypeStruct((EXPERT_TOPK, t), F32)),
        grid=(t // tm,),
        in_specs=[pl.BlockSpec((tm, d), lambda i: (i, 0)), pl.BlockSpec((1, d), lambda i: (0, 0)),
                  pl.BlockSpec((2, d, LANES), lambda i: (0, 0, 0)), pl.BlockSpec((ROUTE_ROWS, 1), lambda i: (0, 0))],
        out_specs=(pl.BlockSpec((tm, d), lambda i: (i, 0)), pl.BlockSpec((EXPERT_TOPK, tm), lambda i: (0, i)),
                   pl.BlockSpec((EXPERT_TOPK, tm), lambda i: (0, i))),
        compiler_params=_cp("parallel"),
        name="moe_route",
    )(x, g.reshape(1, d), w, b)


def _rank_kernel(e_ref, rank_ref, cnt_ref, *, tb):
    @pl.when(pl.program_id(0) == 0)
    def _():
        cnt_ref[...] = jnp.zeros_like(cnt_ref)

    e = e_ref[...]
    onehot = jnp.where(lax.broadcasted_iota(jnp.int32, (N_EXPERTS, tb), 0) == e, 1.0, 0.0)
    earlier = jnp.where(lax.broadcasted_iota(jnp.int32, (tb, tb), 0) < lax.broadcasted_iota(jnp.int32, (tb, tb), 1),
                        1.0, 0.0)
    before = _mmb(onehot, earlier) + cnt_ref[...]
    rank_ref[...] = jnp.sum(onehot * before, axis=0, keepdims=True).astype(jnp.int32)
    cnt_ref[...] = cnt_ref[...] + jnp.sum(onehot, axis=1, keepdims=True)


def moe_rank(e_flat):
    n = e_flat.shape[1]
    tb = _tile(n, 512)
    return pl.pallas_call(
        functools.partial(_rank_kernel, tb=tb),
        out_shape=(jax.ShapeDtypeStruct((1, n), jnp.int32), jax.ShapeDtypeStruct((N_EXPERTS, 1), F32)),
        grid=(n // tb,),
        in_specs=[pl.BlockSpec((1, tb), lambda i: (0, i))],
        out_specs=(pl.BlockSpec((1, tb), lambda i: (0, i)), pl.BlockSpec((N_EXPERTS, 1), lambda i: (0, 0))),
        compiler_params=_cp("arbitrary"),
        name="moe_rank",
    )(e_flat)


def _dispatch_kernel(dest_ref, cnt_ref, pstart_ref, xn_ref, xs_ref, xbuf_ref, zero_ref, load_sem, scat_sem, fill_sem,
                     *, tb, tokens, nblk):
    i, nsteps = pl.program_id(0), pl.num_programs(0)
    slot = i % 2

    def load(step, sl):
        return pltpu.make_async_copy(xn_ref.at[pl.ds(step * tb, tb)], xbuf_ref.at[sl], load_sem.at[sl])

    def scatters(step, sl):
        return [pltpu.make_async_copy(xbuf_ref.at[sl, pl.ds(j, 1)],
                                      xs_ref.at[pl.ds(dest_ref[kk * tokens + step * tb + j], 1)], scat_sem.at[sl])
                for j in range(tb) for kk in range(EXPERT_TOPK)]

    def zero_rows(dst_row, nrows):
        return pltpu.make_async_copy(zero_ref.at[pl.ds(0, nrows)], xs_ref.at[pl.ds(dst_row, nrows)], fill_sem)

    @pl.when(i == 0)
    def _():
        load(0, 0).start()
        zero_ref[...] = jnp.zeros_like(zero_ref)
        last = N_EXPERTS - 1
        n_used = (pstart_ref[last] + cnt_ref[last] + MOE_BLOCK - 1) // MOE_BLOCK

        def fill_block(blk, carry):
            zero_rows(blk * MOE_BLOCK, MOE_BLOCK).start()
            zero_rows(blk * MOE_BLOCK, MOE_BLOCK).wait()
            return carry

        lax.fori_loop(n_used, nblk, fill_block, 0)

        def per_expert(e, carry):
            npad = (-cnt_ref[e]) % MOE_BLOCK
            base = pstart_ref[e] + cnt_ref[e]

            def fill(r, c2):
                zero_rows(base + r, 1).start()
                return c2

            lax.fori_loop(0, npad, fill, 0)

            def drain(r, c2):
                zero_rows(base + r, 1).wait()
                return c2

            lax.fori_loop(0, npad, drain, 0)
            return carry

        lax.fori_loop(0, N_EXPERTS, per_expert, 0)

    @pl.when(i >= 1)
    def _():
        for cp in scatters(i - 1, 1 - slot):
            cp.wait()

    @pl.when(i + 1 < nsteps)
    def _():
        load(i + 1, 1 - slot).start()

    load(i, slot).wait()
    for cp in scatters(i, slot):
        cp.start()

    @pl.when(i == nsteps - 1)
    def _():
        for cp in scatters(i, slot):
            cp.wait()


def moe_dispatch(xn, dest_flat, counts, pad_start, n_slots):
    t, d = xn.shape
    tb = _tile(t, MOE_ROWS_PER_STEP)
    return pl.pallas_call(
        functools.partial(_dispatch_kernel, tb=tb, tokens=t, nblk=n_slots // MOE_BLOCK),
        out_shape=jax.ShapeDtypeStruct((n_slots, d), xn.dtype),
        grid_spec=pltpu.PrefetchScalarGridSpec(
            num_scalar_prefetch=3,
            grid=(t // tb,),
            in_specs=[pl.BlockSpec(memory_space=pl.ANY)],
            out_specs=pl.BlockSpec(memory_space=pl.ANY),
            scratch_shapes=[pltpu.VMEM((2, tb, d), xn.dtype), pltpu.VMEM((MOE_BLOCK, d), xn.dtype),
                            pltpu.SemaphoreType.DMA((2,)), pltpu.SemaphoreType.DMA((2,)), pltpu.SemaphoreType.DMA],
        ),
        compiler_params=pltpu.CompilerParams(dimension_semantics=("arbitrary",), has_side_effects=True,
                                             vmem_limit_bytes=VMEM_LIMIT),
        name="moe_dispatch",
    )(dest_flat, counts, pad_start, xn)


def _expert_kernel(be_ref, nused_ref, x_ref, wg_ref, wu_ref, wd_ref, y_ref, wg16_ref, wu16_ref, wd16_ref):
    i = pl.program_id(0)
    used = i < nused_ref[0]

    @pl.when(jnp.logical_and(used, jnp.logical_or(i == 0, be_ref[i] != be_ref[jnp.maximum(i - 1, 0)])))
    def _():
        wg16_ref[...] = wg_ref[...].astype(BF16)
        wu16_ref[...] = wu_ref[...].astype(BF16)
        wd16_ref[...] = wd_ref[...].astype(BF16)

    @pl.when(used)
    def _():
        x = x_ref[...].astype(BF16)
        hg = jnp.dot(x, wg16_ref[...], preferred_element_type=F32)
        hu = jnp.dot(x, wu16_ref[...], preferred_element_type=F32)
        hmid = (hg * jax.nn.sigmoid(hg) * hu).astype(BF16)
        y_ref[...] = jnp.dot(hmid, wd16_ref[...], preferred_element_type=F32).astype(y_ref.dtype)

    @pl.when(pl.program_id(0) >= nused_ref[0])
    def _():
        y_ref[...] = jnp.zeros_like(y_ref)


def moe_experts(xs, blk_expert, n_used, w_gate, w_up, w_down):
    p, d = xs.shape
    de = w_gate.shape[2]
    nblk = p // MOE_BLOCK
    row = lambda i, be, nu: (jnp.minimum(i, nu[0] - 1), 0)
    wsel = lambda i, be, nu: (be[jnp.minimum(i, nu[0] - 1)], 0, 0)
    return pl.pallas_call(
        _expert_kernel,
        out_shape=jax.ShapeDtypeStruct((p, d), F32),
        grid_spec=pltpu.PrefetchScalarGridSpec(
            num_scalar_prefetch=2,
            grid=(nblk,),
            in_specs=[pl.BlockSpec((MOE_BLOCK, d), row), pl.BlockSpec((None, d, de), wsel),
                      pl.BlockSpec((None, d, de), wsel), pl.BlockSpec((None, de, d), wsel)],
            out_specs=pl.BlockSpec((MOE_BLOCK, d), lambda i, be, nu: (i, 0)),
            scratch_shapes=[pltpu.VMEM((d, de), BF16), pltpu.VMEM((d, de), BF16), pltpu.VMEM((de, d), BF16)],
        ),
        compiler_params=_cp("arbitrary"),
        name="moe_experts",
    )(blk_expert, n_used, xs, w_gate, w_up, w_down)


def _combine_kernel(dest_ref, x_ref, wt_ref, g_ref, ys_ref, o_ref, y_ref, sems, *, tb, tokens):
    i, nsteps = pl.program_id(0), pl.num_programs(0)
    slot = i % 2

    def gathers(step, sl):
        return [pltpu.make_async_copy(ys_ref.at[pl.ds(dest_ref[kk * tokens + step * tb + j], 1)],
                                      y_ref.at[sl, kk, pl.ds(j, 1)], sems.at[sl])
                for j in range(tb) for kk in range(EXPERT_TOPK)]

    @pl.when(i == 0)
    def _():
        for cp in gathers(0, 0):
            cp.start()

    @pl.when(i + 1 < nsteps)
    def _():
        for cp in gathers(i + 1, 1 - slot):
            cp.start()

    for cp in gathers(i, slot):
        cp.wait()
    h = x_ref[...] + wt_ref[:, 0:1] * y_ref[slot, 0] + wt_ref[:, 1:2] * y_ref[slot, 1]
    r = lax.rsqrt(jnp.mean(h * h, axis=-1, keepdims=True) + EPS)
    o_ref[...] = h * r * g_ref[...]


def moe_combine(x, ys, dest_flat, wts_tok, g_final):
    t, d = x.shape
    tb = _tile(t, MOE_ROWS_PER_STEP)
    return pl.pallas_call(
        functools.partial(_combine_kernel, tb=tb, tokens=t),
        out_shape=jax.ShapeDtypeStruct((t, d), F32),
        grid_spec=pltpu.PrefetchScalarGridSpec(
            num_scalar_prefetch=1,
            grid=(t // tb,),
            in_specs=[pl.BlockSpec((tb, d), lambda i, dr: (i, 0)),
                      pl.BlockSpec((tb, EXPERT_TOPK), lambda i, dr: (i, 0)),
                      pl.BlockSpec((1, d), lambda i, dr: (0, 0)),
                      pl.BlockSpec(memory_space=pl.ANY)],
            out_specs=pl.BlockSpec((tb, d), lambda i, dr: (i, 0)),
            scratch_shapes=[pltpu.VMEM((2, EXPERT_TOPK, tb, d), F32), pltpu.SemaphoreType.DMA((2,))],
        ),
        compiler_params=_cp("arbitrary"),
        name="moe_combine",
    )(dest_flat, x, wts_tok, g_final.reshape(1, d), ys)


def hier_moe_final(x2, g_ffn, w_rg, b_rg, w_re, w_gate, w_up, w_down, g_final):
    t, d = x2.shape
    xn, eid, wts = moe_route(x2, g_ffn, w_rg, b_rg, w_re)
    n = EXPERT_TOPK * t
    e_flat = eid.reshape(1, n)
    rank, counts_f = moe_rank(e_flat)
    counts = counts_f.reshape(N_EXPERTS).astype(jnp.int32)
    padded = (counts + MOE_BLOCK - 1) // MOE_BLOCK * MOE_BLOCK
    pad_end = jnp.cumsum(padded)
    pad_start = pad_end - padded
    dest = (jnp.take(pad_start, e_flat[0]) + rank[0]).astype(jnp.int32)
    nblk = -(-n // MOE_BLOCK) + N_EXPERTS
    blk_start = jnp.arange(nblk, dtype=jnp.int32) * MOE_BLOCK
    blk_expert = jnp.minimum(jnp.sum(pad_end[None, :] <= blk_start[:, None], axis=1), N_EXPERTS - 1).astype(jnp.int32)
    n_used = (pad_end[-1:] // MOE_BLOCK).astype(jnp.int32)
    xs = moe_dispatch(xn, dest, counts, pad_start.astype(jnp.int32), nblk * MOE_BLOCK)
    ys = moe_experts(xs, blk_expert, n_used, w_gate, w_up, w_down)
    return moe_combine(x2, ys, dest, wts.T, g_final)


def kernel(x, mem, g_mix, w_in, b_gate, gdn_conv, gdn_a_log, gdn_dt_bias, gdn_norm_g, g_mem, w_mem_kv, w_o_swa, w_o_gdn, w_o_mem, w_out, g_ffn, w_route_group, b_route_group, w_route_expert, w_expert_gate, w_expert_up, w_expert_down, g_final):
    batch, seq, d = x.shape
    depth = w_in.shape[0]
    assert depth == 1, "the final RMSNorm is fused into the layer's MoE combine"
    t = batch * seq
    mem_tokens = mem.shape[1]
    h = x.reshape(t, d)
    o0 = 3 * SWA_WIDTH
    o1 = o0 + 3 * GDN_WIDTH
    o2 = o1 + GDN_WIDTH
    o3 = o2 + 4 * GDN_HEADS
    o4 = o3 + MEM_WIDTH
    tables = rope_tables(seq)
    for l in range(depth):
        w = w_in[l].astype(BF16)
        a = rmsnorm_rows(h, g_mix[l], BF16)
        qkv_a = matmul(a, w, F32, n=o0, rope=(tables, seq))
        o_a = dilated_swa(qkv_a, batch, seq)
        qkv_b = matmul_conv_norm(a, w, o0, gdn_conv[l], seq)
        z_b = matmul(a, w, BF16, n=GDN_WIDTH, col_off=o1)
        w_ba = jnp.zeros((d, LANES), BF16).at[:, :o3 - o2].set(w[:, o2:o3])
        beta, gc = gdn_gates(matmul(a, w_ba, F32), gdn_a_log[l], gdn_dt_bias[l])
        o_f, o_r = gdn_scan(qkv_b, beta, gc, batch, seq)
        o_b = gdn_post(o_f, o_r, z_b, gdn_norm_g[l])
        kv = matmul(rmsnorm_rows(mem.reshape(batch * mem_tokens, d), g_mem[l], BF16), w_mem_kv[l].astype(BF16), BF16)
        o_m = mem_attention(matmul(a, w[:, o3:o4], BF16), kv, batch, seq, mem_tokens)
        gates = matmul(a, w[:, o4:], BF16, bias=b_gate[l].reshape(-1))
        mix = gated_merge(o_a, o_b, o_m, w_o_swa[l].astype(BF16), w_o_gdn[l].astype(BF16),
                          w_o_mem[l].astype(BF16), gates)
        x2 = matmul(mix, w_out[l].astype(BF16), F32, residual=h)
        h = hier_moe_final(x2, g_ffn[l], w_route_group[l], b_route_group[l], w_route_expert[l],
                           w_expert_gate[l], w_expert_up[l], w_expert_down[l], g_final)
    return h.reshape(batch, seq, d)
```

```python
import functools

import jax
import jax.numpy as jnp
from jax import lax
from jax.experimental import pallas as pl
from jax.experimental.pallas import tpu as pltpu

F32 = jnp.float32
BF16 = jnp.bfloat16

HEAD_DIM = 128
SWA_GROUPS = ((128, 1), (512, 4), (2048, 16))
SWA_HEADS_PER_GROUP = 4
SWA_HEADS = SWA_HEADS_PER_GROUP * len(SWA_GROUPS)
SWA_WIDTH = SWA_HEADS * HEAD_DIM
SWA_BLOCK = 64
SWA_UNROLL = 4
ROPE_THETA = 500000.0
ROPE_DIMS = HEAD_DIM // 4
GDN_HEADS = 12
GDN_WIDTH = GDN_HEADS * HEAD_DIM
GDN_CONV = 5
GDN_CHUNK = 64
MEM_HEADS = 4
MEM_HEAD_DIM = 256
MEM_WIDTH = MEM_HEADS * MEM_HEAD_DIM
N_BRANCH = 3
N_GROUPS = 4
EXPERTS_PER_GROUP = 8
N_EXPERTS = N_GROUPS * EXPERTS_PER_GROUP
EXPERT_TOPK = 2
MOE_BLOCK = 256
EPS = 1e-6
NEG_INF = -1e30

LANES = 128
SUBLANES = 8
VMEM_LIMIT = 48 * 1024 * 1024


def _cp(*sem, vmem=VMEM_LIMIT):
    return pltpu.CompilerParams(dimension_semantics=sem, vmem_limit_bytes=vmem)


def _tile(n, pref):
    t = min(n, pref)
    while n % t:
        t //= 2
    return t


def _rmsnorm_kernel(x_ref, g_ref, o_ref):
    x = x_ref[...].astype(F32)
    r = lax.rsqrt(jnp.mean(x * x, axis=-1, keepdims=True) + EPS)
    o_ref[...] = (x * r * g_ref[...]).astype(o_ref.dtype)


def rmsnorm_rows(x, g, out_dtype):
    m, d = x.shape
    tm = _tile(m, 512)
    return pl.pallas_call(
        _rmsnorm_kernel,
        out_shape=jax.ShapeDtypeStruct((m, d), out_dtype),
        grid=(m // tm,),
        in_specs=[pl.BlockSpec((tm, d), lambda i: (i, 0)), pl.BlockSpec((1, d), lambda i: (0, 0))],
        out_specs=pl.BlockSpec((tm, d), lambda i: (i, 0)),
        compiler_params=_cp("parallel"),
        name="rmsnorm_rows",
    )(x, g.reshape(1, d))


def _mm_kernel(a_ref, w_ref, o_ref):
    o_ref[...] = jnp.dot(a_ref[...], w_ref[...], preferred_element_type=F32).astype(o_ref.dtype)


def _mm_sigmoid_kernel(a_ref, w_ref, b_ref, o_ref):
    z = jnp.dot(a_ref[...], w_ref[...], preferred_element_type=F32) + b_ref[...]
    o_ref[...] = jax.nn.sigmoid(z).astype(o_ref.dtype)


def _mm_residual_kernel(a_ref, w_ref, r_ref, o_ref):
    o_ref[...] = r_ref[...] + jnp.dot(a_ref[...], w_ref[...], preferred_element_type=F32)


EPI_COLS = 2 * HEAD_DIM


def _mm_rope_kernel(a_ref, w_ref, c_ref, s1_ref, s2_ref, o_ref):
    c, s1, s2 = c_ref[...], s1_ref[...], s2_ref[...]
    half = ROPE_DIMS // 2
    for c0 in range(0, o_ref.shape[1], EPI_COLS):
        acc = jnp.dot(a_ref[...], w_ref[:, c0:c0 + EPI_COLS], preferred_element_type=F32)
        for h0 in range(0, EPI_COLS, HEAD_DIM):
            xh = acc[:, h0:h0 + HEAD_DIM]
            o_ref[:, c0 + h0:c0 + h0 + HEAD_DIM] = (
                xh * c + pltpu.roll(xh, HEAD_DIM - half, 1) * s1 + pltpu.roll(xh, half, 1) * s2)


MM_TM = 1024
MM_TN = (1536, 1024, 512)


def _col_tile(n):
    return next((t for t in MM_TN if n % t == 0), n)


def matmul(a, w, out_dtype, *, n=None, col_off=0, bias=None, residual=None, rope=None):
    m, k = a.shape
    n = w.shape[1] if n is None else n
    tm, tn = _tile(m, MM_TM), _col_tile(n)
    assert col_off % tn == 0
    joff = col_off // tn
    a_spec = pl.BlockSpec((tm, k), lambda i, j: (i, 0))
    w_spec = pl.BlockSpec((k, tn), lambda i, j: (0, joff + j))
    o_spec = pl.BlockSpec((tm, tn), lambda i, j: (i, j))
    if bias is not None:
        kern, extra, extra_specs = _mm_sigmoid_kernel, (bias.reshape(1, n),), [pl.BlockSpec((1, tn), lambda i, j: (0, j))]
    elif residual is not None:
        kern, extra, extra_specs = _mm_residual_kernel, (residual,), [o_spec]
    elif rope is not None:
        tables, seq = rope
        assert tn == SWA_WIDTH and n == 3 * SWA_WIDTH
        per = seq // tm
        t_spec = pl.BlockSpec((None, tm, HEAD_DIM), lambda i, j: (j, i % per, 0))
        kern = _mm_rope_kernel
        extra, extra_specs = tuple(tables), [t_spec] * 3
    else:
        kern, extra, extra_specs = _mm_kernel, (), []
    return pl.pallas_call(
        kern,
        out_shape=jax.ShapeDtypeStruct((m, n), out_dtype),
        grid=(m // tm, n // tn),
        in_specs=[a_spec, w_spec] + extra_specs,
        out_specs=o_spec,
        compiler_params=_cp("parallel", "arbitrary"),
        name="matmul",
    )(a, w, *extra)


def rope_tables(seq):
    half = ROPE_DIMS // 2
    inv = ROPE_THETA ** (-jnp.arange(half, dtype=F32) / half)
    ang = jnp.arange(seq, dtype=F32)[:, None] * inv[None, :]
    cos, sin = jnp.cos(ang), jnp.sin(ang)
    zeros = jnp.zeros((seq, HEAD_DIM - ROPE_DIMS), F32)
    zh = jnp.zeros((seq, half), F32)
    c = jnp.concatenate([cos, cos, zeros + 1.0], axis=1)
    s1 = jnp.concatenate([-sin, zh, zeros], axis=1)
    s2 = jnp.concatenate([zh, sin, zeros], axis=1)
    ident = (jnp.ones_like(c), jnp.zeros_like(c), jnp.zeros_like(c))
    return tuple(jnp.stack([tab * (HEAD_DIM ** -0.5), tab, idt]) for tab, idt in zip((c, s1, s2), ident))


CONV_HALO = 16


def _mm_conv_kernel(a_ref, ap_ref, an_ref, w_ref, cw_ref, o_ref, buf_ref, *, tm, tiles_per_seq):
    i, kind = pl.program_id(0), pl.program_id(1)
    halo = CONV_HALO
    rows = tm + 2 * halo
    keep_prev = jnp.where((i % tiles_per_seq) == 0, 0.0, 1.0)
    keep_next = jnp.where((i % tiles_per_seq) == tiles_per_seq - 1, 0.0, 1.0)
    qk_mul = jnp.where(kind == 0, HEAD_DIM ** -0.5, 1.0)
    w = w_ref[...]
    buf_ref[0:halo, :] = jnp.dot(ap_ref[...], w, preferred_element_type=F32) * keep_prev
    buf_ref[halo:halo + tm, :] = jnp.dot(a_ref[...], w, preferred_element_type=F32)
    buf_ref[halo + tm:, :] = jnp.dot(an_ref[...], w, preferred_element_type=F32) * keep_next
    for h in range(GDN_HEADS):
        cols = slice(h * HEAD_DIM, (h + 1) * HEAD_DIM)
        xh = buf_ref[:, cols]
        acc = xh * cw_ref[GDN_CONV // 2:GDN_CONV // 2 + 1, cols]
        for t in range(GDN_CONV):
            if t != GDN_CONV // 2:
                acc = acc + pltpu.roll(xh, (GDN_CONV // 2 - t) % rows, 0) * cw_ref[t:t + 1, cols]
        acc = acc[halo:halo + tm, :]
        y = acc * jax.nn.sigmoid(acc)
        inv = lax.rsqrt(jnp.sum(y * y, axis=-1, keepdims=True) + EPS) * qk_mul
        o_ref[h] = (y * jnp.where(kind < 2, inv, 1.0)).astype(o_ref.dtype)


def matmul_conv_norm(a, w, col_off, conv_w, seq):
    m, k = a.shape
    tn = GDN_WIDTH
    tm = _tile(seq, MM_TM)
    assert col_off % tn == 0 and tm % CONV_HALO == 0
    joff = col_off // tn
    per_h = tm // CONV_HALO
    nblk_h = m // CONV_HALO
    return pl.pallas_call(
        functools.partial(_mm_conv_kernel, tm=tm, tiles_per_seq=seq // tm),
        out_shape=jax.ShapeDtypeStruct((3 * GDN_HEADS, m, HEAD_DIM), BF16),
        grid=(m // tm, 3),
        in_specs=[
            pl.BlockSpec((tm, k), lambda i, j: (i, 0)),
            pl.BlockSpec((CONV_HALO, k), lambda i, j: (jnp.maximum(i * per_h - 1, 0), 0)),
            pl.BlockSpec((CONV_HALO, k), lambda i, j: (jnp.minimum((i + 1) * per_h, nblk_h - 1), 0)),
            pl.BlockSpec((k, tn), lambda i, j: (0, joff + j)),
            pl.BlockSpec((GDN_CONV, tn), lambda i, j: (0, j)),
        ],
        out_specs=pl.BlockSpec((GDN_HEADS, tm, HEAD_DIM), lambda i, j: (j, i, 0)),
        scratch_shapes=[pltpu.VMEM((tm + 2 * CONV_HALO, tn), F32)],
        compiler_params=_cp("parallel", "arbitrary"),
        name="matmul_conv_norm",
    )(a, a, a, w, conv_w)


def _swa_kernel(q_ref, k_ref, v_ref, o_ref, acc_ref, m_ref, l_ref, *, seq):
    g = pl.program_id(2)

    @pl.when(g == 0)
    def _():
        acc_ref[...] = jnp.zeros_like(acc_ref)
        m_ref[...] = jnp.full_like(m_ref, NEG_INF)
        l_ref[...] = jnp.zeros_like(l_ref)

    for gi, (window, dil) in enumerate(SWA_GROUPS):
        radius = window // (2 * dil)
        assert radius <= SWA_BLOCK
        sub = seq // dil
        qb = min(128, sub)
        win = min(sub, qb + 2 * SWA_BLOCK)
        nqb = sub // qb

        @pl.when(g == gi)
        def _(dil=dil, radius=radius, sub=sub, qb=qb, win=win, nqb=nqb):
            def rows(start, size):
                return pl.ds(start, size) if dil == 1 else pl.ds(start, size, stride=dil)

            total = dil * nqb
            unroll = SWA_UNROLL if total % SWA_UNROLL == 0 else 1
            idx = range(unroll)

            def body(step, carry):
                its = [step * unroll + u for u in idx]
                qss = [(it % nqb) * qb for it in its]
                wss = [jnp.clip(qs - SWA_BLOCK, 0, sub - win) for qs in qss]
                q_rows = [rows(it // nqb + dil * qs, qb) for it, qs in zip(its, qss)]
                k_rows = [rows(it // nqb + dil * ws, win) for it, ws in zip(its, wss)]
                q = [q_ref[r, :].astype(BF16) for r in q_rows]
                k = [k_ref[r, :].astype(BF16) for r in k_rows]
                v = [v_ref[r, :].astype(BF16) for r in k_rows]
                s = [_nt(q[u], k[u]) for u in idx]
                off = lax.broadcasted_iota(jnp.int32, (qb, win), 0) - lax.broadcasted_iota(jnp.int32, (qb, win), 1)
                s = [jnp.where(jnp.abs(off + (qss[u] - wss[u])) <= radius, s[u], NEG_INF) for u in idx]
                m_old = [m_ref[r, :] for r in q_rows]
                m_new = [jnp.maximum(m_old[u], jnp.max(s[u], axis=1, keepdims=True)) for u in idx]
                p = [jnp.exp(s[u] - m_new[u]) for u in idx]
                corr = [jnp.exp(m_old[u] - m_new[u]) for u in idx]
                pv = [jnp.dot(p[u].astype(BF16), v[u], preferred_element_type=F32) for u in idx]
                for u in idx:
                    l_ref[q_rows[u], :] = l_ref[q_rows[u], :] * corr[u] + jnp.sum(p[u], axis=1, keepdims=True)
                    acc_ref[q_rows[u], :] = acc_ref[q_rows[u], :] * corr[u] + pv[u]
                    m_ref[q_rows[u], :] = m_new[u]
                return carry

            lax.fori_loop(0, total // unroll, body, 0)

    @pl.when(g == len(SWA_GROUPS) - 1)
    def _():
        o_ref[...] = (acc_ref[...] / l_ref[...]).astype(o_ref.dtype)


def dilated_swa(qkv, batch, seq):
    nh = SWA_HEADS_PER_GROUP
    qkv3 = qkv.reshape(batch, seq, 3 * SWA_WIDTH)

    def spec(off):
        return pl.BlockSpec((None, seq, HEAD_DIM), lambda b, h, g: (b, 0, off + g * nh + h))

    out = pl.pallas_call(
        functools.partial(_swa_kernel, seq=seq),
        out_shape=jax.ShapeDtypeStruct((batch, seq, nh * HEAD_DIM), BF16),
        grid=(batch, nh, len(SWA_GROUPS)),
        in_specs=[spec(0), spec(SWA_HEADS), spec(2 * SWA_HEADS)],
        out_specs=pl.BlockSpec((None, seq, HEAD_DIM), lambda b, h, g: (b, 0, h)),
        scratch_shapes=[pltpu.VMEM((seq, HEAD_DIM), F32), pltpu.VMEM((seq, 1), F32), pltpu.VMEM((seq, 1), F32)],
        compiler_params=_cp("parallel", "parallel", "arbitrary"),
        name="dilated_swa",
    )(qkv3, qkv3, qkv3)
    return out.reshape(batch * seq, nh * HEAD_DIM)


def _gdn_gate_kernel(ba_ref, alog_ref, dtb_ref, beta_ref, gc_ref, *, ts):
    x = ba_ref[...]
    nhd = 2 * GDN_HEADS
    lane = lax.broadcasted_iota(jnp.int32, (ts, LANES), 1)
    row = lax.broadcasted_iota(jnp.int32, (ts, LANES), 0) % GDN_CHUNK
    beta_ref[...] = jax.nn.sigmoid(x)
    z = x + dtb_ref[...]
    softplus = jnp.maximum(z, 0.0) + jnp.log(1.0 + jnp.exp(-jnp.abs(z)))
    g = jnp.where(jnp.logical_and(lane >= nhd, lane < 2 * nhd), -jnp.exp(alog_ref[...]) * softplus, 0.0)
    pre, suf = g, g
    s = 1
    while s < GDN_CHUNK:
        pre = pre + jnp.where(row >= s, pltpu.roll(pre, s, 0), 0.0)
        suf = suf + jnp.where(row < GDN_CHUNK - s, pltpu.roll(suf, ts - s, 0), 0.0)
        s *= 2
    gc_ref[...] = jnp.where(lane < nhd + GDN_HEADS, pre, suf)


def gdn_gates(ba, a_log, dt_bias):
    t = ba.shape[0]
    ts = _tile(t, 1024)
    nhd = 2 * GDN_HEADS
    pad = lambda v: jnp.zeros((1, LANES), F32).at[0, nhd:2 * nhd].set(v.reshape(-1).astype(F32))
    shp = jax.ShapeDtypeStruct((t, LANES), F32)
    vec = pl.BlockSpec((1, LANES), lambda i: (0, 0))
    blk = pl.BlockSpec((ts, LANES), lambda i: (i, 0))
    return pl.pallas_call(
        functools.partial(_gdn_gate_kernel, ts=ts),
        out_shape=(shp, shp),
        grid=(t // ts,),
        in_specs=[blk, vec, vec],
        out_specs=(blk, blk),
        compiler_params=_cp("parallel"),
        name="gdn_gates",
    )(ba, pad(a_log), pad(dt_bias))


def _nt(a, b):
    return lax.dot_general(a, b, (((1,), (1,)), ((), ())), preferred_element_type=F32)


def _tn(a, b):
    return lax.dot_general(a, b, (((0,), (0,)), ((), ())), preferred_element_type=F32)


def _mmb(a, b):
    return jnp.dot(a.astype(BF16), b.astype(BF16), preferred_element_type=F32)


def _gdn_chunks(qs, ks, vs, bcols, gcols, states, revs):
    c = GDN_CHUNK
    idx = range(len(qs))
    ri = lax.broadcasted_iota(jnp.int32, (c, c), 0)
    ci = lax.broadcasted_iota(jnp.int32, (c, c), 1)
    strict = {False: ri > ci, True: ri < ci}
    incl = {False: ri >= ci, True: ri <= ci}
    eye = jnp.where(ri == ci, 1.0, 0.0)
    lane = lax.broadcasted_iota(jnp.int32, (c, LANES), 1)
    ones3 = jnp.where(lane < 3, 1.0, 0.0)
    ones3_hi = jnp.where(jnp.logical_and(lane >= 3, lane < 6), 1.0, 0.0)
    g_last = [gcols[i][0:1, :] if revs[i] else gcols[i][c - 1:c, :] for i in idx]
    egc = [jnp.exp(gcols[i]) for i in idx]
    q_state = [_mmb(qs[i] * egc[i], states[i]) for i in idx]
    diff = []
    for i in idx:
        g1 = gcols[i].astype(BF16).astype(F32)
        g2 = (gcols[i] - g1).astype(BF16).astype(F32)
        g3 = (gcols[i] - g1 - g2).astype(BF16).astype(F32)
        pieces = jnp.where(lane == 0, g1, jnp.where(lane == 1, g2, jnp.where(lane == 2, g3, 0.0)))
        xm = pieces + ones3_hi
        ym = ones3 - pltpu.roll(pieces, 3, 1)
        diff.append(_nt(xm.astype(BF16), ym.astype(BF16)))
    decay = [jnp.where(incl[revs[i]], jnp.exp(jnp.where(incl[revs[i]], diff[i], 0.0)), 0.0) for i in idx]
    kb = [ks[i] * bcols[i] for i in idx]
    k16 = [ks[i].astype(BF16) for i in idx]
    lm = [jnp.where(strict[revs[i]], _nt(kb[i].astype(BF16), k16[i]) * decay[i], 0.0) for i in idx]
    attn = [_nt(qs[i].astype(BF16), k16[i]) * decay[i] for i in idx]
    tinv = [eye - lm[i] for i in idx]
    pw = [_mmb(lm[i], lm[i]) for i in idx]
    n = 4
    while n < c:
        both = [_mmb(jnp.concatenate([tinv[i], pw[i]], axis=0), pw[i]) for i in idx]
        tinv = [tinv[i] + both[i][:c] for i in idx]
        pw = [both[i][c:] for i in idx]
        n *= 2
    tinv = [tinv[i] + _mmb(tinv[i], pw[i]) for i in idx]
    u = [_mmb(tinv[i], jnp.concatenate([vs[i] * bcols[i], kb[i] * egc[i]], axis=1)) for i in idx]
    v_new = [u[i][:, :HEAD_DIM] - _mmb(u[i][:, HEAD_DIM:], states[i]) for i in idx]
    outs = [q_state[i] + _mmb(attn[i], v_new[i]) for i in idx]
    k_e = [ks[i] * jnp.exp(g_last[i] - gcols[i]) for i in idx]
    new_states = [states[i] * jnp.exp(g_last[i]) + _tn(k_e[i].astype(BF16), v_new[i].astype(BF16)) for i in idx]
    return outs, new_states


def _gdn_kernel(qf_ref, kf_ref, vf_ref, bf_ref, gf_ref, qb_ref, kb_ref, vb_ref, bb_ref, gb_ref,
                of_ref, ob_ref, state_ref, *, chunks):
    @pl.when(pl.program_id(1) == 0)
    def _():
        state_ref[...] = jnp.zeros_like(state_ref)

    nh = GDN_HEADS

    def body(it, carry):
        rows_f = pl.ds(pl.multiple_of(it * GDN_CHUNK, GDN_CHUNK), GDN_CHUNK)
        rows_b = pl.ds(pl.multiple_of((chunks - 1 - it) * GDN_CHUNK, GDN_CHUNK), GDN_CHUNK)
        tabs = ((qf_ref, kf_ref, vf_ref, bf_ref[rows_f, :], gf_ref[rows_f, :], rows_f, 0),
                (qb_ref, kb_ref, vb_ref, bb_ref[rows_b, :], gb_ref[rows_b, :], rows_b, nh))
        qs, ks, vs, bcols, gcols, states, revs = [], [], [], [], [], [], []
        for d, (q_ref, k_ref, v_ref, bt, gt, rows, lane0) in enumerate(tabs):
            for h in range(nh):
                qs.append(q_ref[h, rows, :].astype(F32))
                ks.append(k_ref[h, rows, :].astype(F32))
                vs.append(v_ref[h, rows, :].astype(F32))
                bcols.append(bt[:, lane0 + h:lane0 + h + 1])
                gcols.append(gt[:, 2 * nh + lane0 + h:2 * nh + lane0 + h + 1])
                states.append(state_ref[d * nh + h])
                revs.append(d == 1)
        outs, new_states = _gdn_chunks(qs, ks, vs, bcols, gcols, states, revs)
        for d, o_ref in enumerate((of_ref, ob_ref)):
            for h in range(nh):
                state_ref[d * nh + h] = new_states[d * nh + h]
                o_ref[h, tabs[d][5], :] = outs[d * nh + h].astype(o_ref.dtype)
        return carry

    lax.fori_loop(0, chunks, body, 0)


def gdn_scan(qkv_h, beta, gc, batch, seq):
    t = batch * seq
    rows = _tile(seq, 512)
    nblk = seq // rows
    fwd = lambda b, ib: b * nblk + ib
    bwd = lambda b, ib: b * nblk + nblk - 1 - ib

    def specs(blk):
        qspec = lambda part: pl.BlockSpec((GDN_HEADS, rows, HEAD_DIM), lambda b, ib: (part, blk(b, ib), 0))
        gspec = pl.BlockSpec((rows, LANES), lambda b, ib: (blk(b, ib), 0))
        return [qspec(0), qspec(1), qspec(2), gspec, gspec]

    ospec = lambda blk: pl.BlockSpec((GDN_HEADS, rows, HEAD_DIM), lambda b, ib: (0, blk(b, ib), 0))
    out = jax.ShapeDtypeStruct((GDN_HEADS, t, HEAD_DIM), BF16)
    return pl.pallas_call(
        functools.partial(_gdn_kernel, chunks=rows // GDN_CHUNK),
        out_shape=(out, out),
        grid=(batch, nblk),
        in_specs=specs(fwd) + specs(bwd),
        out_specs=(ospec(fwd), ospec(bwd)),
        scratch_shapes=[pltpu.VMEM((2 * GDN_HEADS, HEAD_DIM, HEAD_DIM), F32)],
        compiler_params=_cp("parallel", "arbitrary"),
        name="gdn_scan",
    )(qkv_h, qkv_h, qkv_h, beta, gc, qkv_h, qkv_h, qkv_h, beta, gc)


def _gdn_post_kernel(of_ref, ob_ref, z_ref, g_ref, o_ref):
    o = of_ref[...].astype(F32) + ob_ref[...].astype(F32)
    r = lax.rsqrt(jnp.mean(o * o, axis=-1, keepdims=True) + EPS)
    z = z_ref[...].astype(F32)
    o_ref[...] = (o * r * g_ref[...] * (z * jax.nn.sigmoid(z))).astype(o_ref.dtype)


def gdn_post(o_f, o_b, z, norm_g):
    nhd, t, _ = o_f.shape
    ts = _tile(t, 1024)
    hspec = pl.BlockSpec((None, ts, HEAD_DIM), lambda i, h: (h, i, 0))
    tspec = pl.BlockSpec((ts, HEAD_DIM), lambda i, h: (i, h))
    return pl.pallas_call(
        _gdn_post_kernel,
        out_shape=jax.ShapeDtypeStruct((t, nhd * HEAD_DIM), BF16),
        grid=(t // ts, nhd),
        in_specs=[hspec, hspec, tspec, pl.BlockSpec((1, HEAD_DIM), lambda i, h: (0, 0))],
        out_specs=tspec,
        compiler_params=_cp("parallel", "arbitrary"),
        name="gdn_post",
    )(o_f, o_b, z, norm_g.reshape(1, HEAD_DIM).astype(F32))


def _mem_attn_kernel(q_ref, k_ref, v_ref, o_ref):
    scale = MEM_HEAD_DIM ** -0.5
    for h in range(MEM_HEADS):
        cols = slice(h * MEM_HEAD_DIM, (h + 1) * MEM_HEAD_DIM)
        s = _nt(q_ref[:, cols], k_ref[:, cols]) * scale
        p = jnp.exp(s - jnp.max(s, axis=-1, keepdims=True))
        p = p / jnp.sum(p, axis=-1, keepdims=True)
        o_ref[:, cols] = jnp.dot(p.astype(BF16), v_ref[:, cols], preferred_element_type=F32).astype(o_ref.dtype)


def mem_attention(mq, kv, batch, seq, mem_tokens):
    tq = _tile(seq, 512)
    per = seq // tq
    return pl.pallas_call(
        _mem_attn_kernel,
        out_shape=jax.ShapeDtypeStruct(mq.shape, BF16),
        grid=(batch * per,),
        in_specs=[
            pl.BlockSpec((tq, MEM_WIDTH), lambda i: (i, 0)),
            pl.BlockSpec((mem_tokens, MEM_WIDTH), lambda i: (i // per, 0)),
            pl.BlockSpec((mem_tokens, MEM_WIDTH), lambda i: (i // per, 1)),
        ],
        out_specs=pl.BlockSpec((tq, MEM_WIDTH), lambda i: (i, 0)),
        compiler_params=_cp("parallel"),
        name="mem_attention",
    )(mq, kv, kv)


def _merge_kernel(oa_ref, ob_ref, om_ref, wa_ref, wb_ref, wm_ref, g0_ref, g1_ref, g2_ref, o_ref):
    ya = jnp.dot(oa_ref[...], wa_ref[...], preferred_element_type=F32)
    yb = jnp.dot(ob_ref[...], wb_ref[...], preferred_element_type=F32)
    ym = jnp.dot(om_ref[...], wm_ref[...], preferred_element_type=F32)
    mix = g0_ref[...].astype(F32) * ya + g1_ref[...].astype(F32) * yb + g2_ref[...].astype(F32) * ym
    o_ref[...] = mix.astype(o_ref.dtype)


def gated_merge(o_a, o_b, o_m, w_a, w_b, w_m, gates):
    t = o_a.shape[0]
    d = w_a.shape[1]
    tm, tn = _tile(t, 1024), _tile(d, 512)
    nj = d // tn
    act = lambda o: pl.BlockSpec((tm, o.shape[1]), lambda i, j: (i, 0))
    wsp = lambda w: pl.BlockSpec((w.shape[0], tn), lambda i, j: (0, j))
    gsp = lambda br: pl.BlockSpec((tm, tn), lambda i, j: (i, br * nj + j))
    return pl.pallas_call(
        _merge_kernel,
        out_shape=jax.ShapeDtypeStruct((t, d), BF16),
        grid=(t // tm, nj),
        in_specs=[act(o_a), act(o_b), act(o_m), wsp(w_a), wsp(w_b), wsp(w_m), gsp(0), gsp(1), gsp(2)],
        out_specs=pl.BlockSpec((tm, tn), lambda i, j: (i, j)),
        compiler_params=_cp("parallel", "arbitrary"),
        name="gated_merge",
    )(o_a, o_b, o_m, w_a, w_b, w_m, gates, gates, gates)


def _pack_bf16_pairs(x):
    w = x.shape[1] // 2
    lo = lax.bitcast_convert_type(x[:, :w].astype(BF16).astype(F32), jnp.uint32)
    hi = lax.bitcast_convert_type(x[:, w:].astype(BF16).astype(F32), jnp.uint32)
    return (hi & jnp.uint32(0xFFFF0000)) | (lo >> 16)


def _unpack_bf16_pairs(p):
    lo = lax.bitcast_convert_type(p << 16, F32)
    hi = lax.bitcast_convert_type(p & jnp.uint32(0xFFFF0000), F32)
    return lo, hi


MOE_ROWS_PER_STEP = 128
ROUTE_ROWS = SUBLANES + N_EXPERTS


def _route_kernel(x_ref, g_ref, w_ref, b_ref, xn_ref, eid_ref, wt_ref):
    x = x_ref[...]
    r = lax.rsqrt(jnp.mean(x * x, axis=-1, keepdims=True) + EPS)
    xn = x * r * g_ref[...]
    xn_ref[...] = _pack_bf16_pairs(xn)
    tm = x.shape[0]
    x_hi = xn.astype(BF16)
    x_lo = (xn - x_hi.astype(F32)).astype(BF16)
    w_hi, w_lo = w_ref[0], w_ref[1]
    lg = (jnp.dot(x_hi, w_hi, preferred_element_type=F32) + jnp.dot(x_lo, w_hi, preferred_element_type=F32)
          + jnp.dot(x_hi, w_lo, preferred_element_type=F32))
    lg = lg.T[:ROUTE_ROWS, :] + b_ref[...]
    gl = [lg[i:i + 1, :] for i in range(N_GROUPS)]
    gmax = functools.reduce(jnp.maximum, gl)
    grp = jnp.full((1, tm), N_GROUPS - 1, jnp.int32)
    for i in range(N_GROUPS - 2, -1, -1):
        grp = jnp.where(gl[i] == gmax, i, grp)
    p_grp = 1.0 / functools.reduce(jnp.add, [jnp.exp(v - gmax) for v in gl])
    sel = lg[SUBLANES + (N_GROUPS - 1) * EXPERTS_PER_GROUP:SUBLANES + N_GROUPS * EXPERTS_PER_GROUP, :]
    for i in range(N_GROUPS - 2, -1, -1):
        lo = SUBLANES + i * EXPERTS_PER_GROUP
        sel = jnp.where(grp == i, lg[lo:lo + EXPERTS_PER_GROUP, :], sel)
    rowi = lax.broadcasted_iota(jnp.int32, (EXPERTS_PER_GROUP, tm), 0)
    v1 = jnp.max(sel, axis=0, keepdims=True)
    i1 = jnp.min(jnp.where(sel == v1, rowi, EXPERTS_PER_GROUP), axis=0, keepdims=True)
    sel2 = jnp.where(rowi == i1, -jnp.inf, sel)
    v2 = jnp.max(sel2, axis=0, keepdims=True)
    i2 = jnp.min(jnp.where(sel2 == v2, rowi, EXPERTS_PER_GROUP), axis=0, keepdims=True)
    e2 = jnp.exp(v2 - v1)
    w1 = p_grp / (1.0 + e2)
    eid_ref[0:1, :] = grp * EXPERTS_PER_GROUP + i1
    eid_ref[1:2, :] = grp * EXPERTS_PER_GROUP + i2
    wt_ref[0:1, :] = w1
    wt_ref[1:2, :] = w1 * e2


def moe_route(x, g, w_grp, b_grp, w_exp):
    t, d = x.shape
    tm = _tile(t, 512)
    w = jnp.zeros((d, LANES), F32).at[:, :N_GROUPS].set(w_grp).at[:, SUBLANES:ROUTE_ROWS].set(w_exp)
    w_hi = w.astype(BF16)
    w = jnp.stack([w_hi, (w - w_hi.astype(F32)).astype(BF16)])
    b = jnp.zeros((ROUTE_ROWS, 1), F32).at[:N_GROUPS, 0].set(b_grp.astype(F32))
    return pl.pallas_call(
        _route_kernel,
        out_shape=(jax.ShapeDtypeStruct((t, d // 2), jnp.uint32), jax.ShapeDtypeStruct((EXPERT_TOPK, t), jnp.int32),
                   jax.ShapeDtypeStruct((EXPERT_TOPK, t), F32)),
        grid=(t // tm,),
        in_specs=[pl.BlockSpec((tm, d), lambda i: (i, 0)), pl.BlockSpec((1, d), lambda i: (0, 0)),
                  pl.BlockSpec((2, d, LANES), lambda i: (0, 0, 0)), pl.BlockSpec((ROUTE_ROWS, 1), lambda i: (0, 0))],
        out_specs=(pl.BlockSpec((tm, d // 2), lambda i: (i, 0)), pl.BlockSpec((EXPERT_TOPK, tm), lambda i: (0, i)),
                   pl.BlockSpec((EXPERT_TOPK, tm), lambda i: (0, i))),
        compiler_params=_cp("parallel"),
        name="moe_route",
    )(x, g.reshape(1, d), w, b)


def _rank_kernel(e_ref, rank_ref, cnt_ref, *, tb):
    @pl.when(pl.program_id(0) == 0)
    def _():
        cnt_ref[...] = jnp.zeros_like(cnt_ref)

    e = e_ref[...]
    onehot = jnp.where(lax.broadcasted_iota(jnp.int32, (N_EXPERTS, tb), 0) == e, 1.0, 0.0)
    earlier = jnp.where(lax.broadcasted_iota(jnp.int32, (tb, tb), 0) < lax.broadcasted_iota(jnp.int32, (tb, tb), 1),
                        1.0, 0.0)
    before = _mmb(onehot, earlier) + cnt_ref[...]
    rank_ref[...] = jnp.sum(onehot * before, axis=0, keepdims=True).astype(jnp.int32)
    cnt_ref[...] = cnt_ref[...] + jnp.sum(onehot, axis=1, keepdims=True)


def moe_rank(e_flat):
    n = e_flat.shape[1]
    tb = _tile(n, 512)
    return pl.pallas_call(
        functools.partial(_rank_kernel, tb=tb),
        out_shape=(jax.ShapeDtypeStruct((1, n), jnp.int32), jax.ShapeDtypeStruct((N_EXPERTS, 1), F32)),
        grid=(n // tb,),
        in_specs=[pl.BlockSpec((1, tb), lambda i: (0, i))],
        out_specs=(pl.BlockSpec((1, tb), lambda i: (0, i)), pl.BlockSpec((N_EXPERTS, 1), lambda i: (0, 0))),
        compiler_params=_cp("arbitrary"),
        name="moe_rank",
    )(e_flat)


def _dispatch_kernel(dest_ref, cnt_ref, pstart_ref, xn_ref, xs_ref, xbuf_ref, zero_ref, load_sem, scat_sem, fill_sem,
                     *, tb, tokens, nblk):
    i, nsteps = pl.program_id(0), pl.num_programs(0)
    slot = i % 2

    def load(step, sl):
        return pltpu.make_async_copy(xn_ref.at[pl.ds(step * tb, tb)], xbuf_ref.at[sl], load_sem.at[sl])

    def scatters(step, sl):
        return [pltpu.make_async_copy(xbuf_ref.at[sl, pl.ds(j, 1)],
                                      xs_ref.at[pl.ds(dest_ref[kk * tokens + step * tb + j], 1)], scat_sem.at[sl])
                for j in range(tb) for kk in range(EXPERT_TOPK)]

    def zero_rows(dst_row, nrows):
        return pltpu.make_async_copy(zero_ref.at[pl.ds(0, nrows)], xs_ref.at[pl.ds(dst_row, nrows)], fill_sem)

    @pl.when(i == 0)
    def _():
        load(0, 0).start()
        zero_ref[...] = jnp.zeros_like(zero_ref)
        last = N_EXPERTS - 1
        n_used = (pstart_ref[last] + cnt_ref[last] + MOE_BLOCK - 1) // MOE_BLOCK

        def fill_block(blk, carry):
            zero_rows(blk * MOE_BLOCK, MOE_BLOCK).start()
            zero_rows(blk * MOE_BLOCK, MOE_BLOCK).wait()
            return carry

        lax.fori_loop(n_used, nblk, fill_block, 0)

        def per_expert(e, carry):
            npad = (-cnt_ref[e]) % MOE_BLOCK
            base = pstart_ref[e] + cnt_ref[e]

            def fill(r, c2):
                zero_rows(base + r, 1).start()
                return c2

            lax.fori_loop(0, npad, fill, 0)

            def drain(r, c2):
                zero_rows(base + r, 1).wait()
                return c2

            lax.fori_loop(0, npad, drain, 0)
            return carry

        lax.fori_loop(0, N_EXPERTS, per_expert, 0)

    @pl.when(i >= 1)
    def _():
        for cp in scatters(i - 1, 1 - slot):
            cp.wait()

    @pl.when(i + 1 < nsteps)
    def _():
        load(i + 1, 1 - slot).start()

    load(i, slot).wait()
    for cp in scatters(i, slot):
        cp.start()

    @pl.when(i == nsteps - 1)
    def _():
        for cp in scatters(i, slot):
            cp.wait()


def moe_dispatch(xn, dest_flat, counts, pad_start, n_slots):
    t, d = xn.shape
    tb = _tile(t, MOE_ROWS_PER_STEP)
    return pl.pallas_call(
        functools.partial(_dispatch_kernel, tb=tb, tokens=t, nblk=n_slots // MOE_BLOCK),
        out_shape=jax.ShapeDtypeStruct((n_slots, d), xn.dtype),
        grid_spec=pltpu.PrefetchScalarGridSpec(
            num_scalar_prefetch=3,
            grid=(t // tb,),
            in_specs=[pl.BlockSpec(memory_space=pl.ANY)],
            out_specs=pl.BlockSpec(memory_space=pl.ANY),
            scratch_shapes=[pltpu.VMEM((2, tb, d), xn.dtype), pltpu.VMEM((MOE_BLOCK, d), xn.dtype),
                            pltpu.SemaphoreType.DMA((2,)), pltpu.SemaphoreType.DMA((2,)), pltpu.SemaphoreType.DMA],
        ),
        compiler_params=pltpu.CompilerParams(dimension_semantics=("arbitrary",), has_side_effects=True,
                                             vmem_limit_bytes=VMEM_LIMIT),
        name="moe_dispatch",
    )(dest_flat, counts, pad_start, xn)


def _expert_kernel(be_ref, nused_ref, x_ref, wg_ref, wu_ref, wd_ref, y_ref, wg16_ref, wu16_ref, wd16_ref):
    i = pl.program_id(0)
    used = i < nused_ref[0]

    @pl.when(jnp.logical_and(used, jnp.logical_or(i == 0, be_ref[i] != be_ref[jnp.maximum(i - 1, 0)])))
    def _():
        wg16_ref[...] = wg_ref[...].astype(BF16)
        wu16_ref[...] = wu_ref[...].astype(BF16)
        wd16_ref[...] = wd_ref[...].astype(BF16)

    @pl.when(used)
    def _():
        x_lo, x_hi = (v.astype(BF16) for v in _unpack_bf16_pairs(x_ref[...]))
        half = x_lo.shape[1]

        def proj(w_ref):
            return (jnp.dot(x_lo, w_ref[:half, :], preferred_element_type=F32)
                    + jnp.dot(x_hi, w_ref[half:, :], preferred_element_type=F32))

        hg, hu = proj(wg16_ref), proj(wu16_ref)
        hmid = (hg * jax.nn.sigmoid(hg) * hu).astype(BF16)
        y_ref[...] = _pack_bf16_pairs(jnp.dot(hmid, wd16_ref[...], preferred_element_type=F32))

    @pl.when(pl.program_id(0) >= nused_ref[0])
    def _():
        y_ref[...] = jnp.zeros_like(y_ref)


def moe_experts(xs, blk_expert, n_used, w_gate, w_up, w_down):
    p, dp = xs.shape
    d, de = w_gate.shape[1:]
    nblk = p // MOE_BLOCK
    row = lambda i, be, nu: (jnp.minimum(i, nu[0] - 1), 0)
    wsel = lambda i, be, nu: (be[jnp.minimum(i, nu[0] - 1)], 0, 0)
    return pl.pallas_call(
        _expert_kernel,
        out_shape=jax.ShapeDtypeStruct((p, dp), jnp.uint32),
        grid_spec=pltpu.PrefetchScalarGridSpec(
            num_scalar_prefetch=2,
            grid=(nblk,),
            in_specs=[pl.BlockSpec((MOE_BLOCK, dp), row), pl.BlockSpec((None, d, de), wsel),
                      pl.BlockSpec((None, d, de), wsel), pl.BlockSpec((None, de, d), wsel)],
            out_specs=pl.BlockSpec((MOE_BLOCK, dp), lambda i, be, nu: (i, 0)),
            scratch_shapes=[pltpu.VMEM((d, de), BF16), pltpu.VMEM((d, de), BF16), pltpu.VMEM((de, d), BF16)],
        ),
        compiler_params=_cp("arbitrary"),
        name="moe_experts",
    )(blk_expert, n_used, xs, w_gate, w_up, w_down)


def _combine_kernel(dest_ref, x_ref, wt_ref, g_ref, ys_ref, o_ref, y_ref, sems, *, tb, tokens):
    i, nsteps = pl.program_id(0), pl.num_programs(0)
    slot = i % 2

    def gathers(step, sl):
        return [pltpu.make_async_copy(ys_ref.at[pl.ds(dest_ref[kk * tokens + step * tb + j], 1)],
                                      y_ref.at[sl, kk, pl.ds(j, 1)], sems.at[sl])
                for j in range(tb) for kk in range(EXPERT_TOPK)]

    @pl.when(i == 0)
    def _():
        for cp in gathers(0, 0):
            cp.start()

    @pl.when(i + 1 < nsteps)
    def _():
        for cp in gathers(i + 1, 1 - slot):
            cp.start()

    for cp in gathers(i, slot):
        cp.wait()
    y0_lo, y0_hi = _unpack_bf16_pairs(y_ref[slot, 0])
    y1_lo, y1_hi = _unpack_bf16_pairs(y_ref[slot, 1])
    w0, w1 = wt_ref[:, 0:1], wt_ref[:, 1:2]
    h = x_ref[...] + jnp.concatenate([w0 * y0_lo + w1 * y1_lo, w0 * y0_hi + w1 * y1_hi], axis=1)
    r = lax.rsqrt(jnp.mean(h * h, axis=-1, keepdims=True) + EPS)
    o_ref[...] = h * r * g_ref[...]


def moe_combine(x, ys, dest_flat, wts_tok, g_final):
    t, d = x.shape
    tb = _tile(t, MOE_ROWS_PER_STEP)
    return pl.pallas_call(
        functools.partial(_combine_kernel, tb=tb, tokens=t),
        out_shape=jax.ShapeDtypeStruct((t, d), F32),
        grid_spec=pltpu.PrefetchScalarGridSpec(
            num_scalar_prefetch=1,
            grid=(t // tb,),
            in_specs=[pl.BlockSpec((tb, d), lambda i, dr: (i, 0)),
                      pl.BlockSpec((tb, EXPERT_TOPK), lambda i, dr: (i, 0)),
                      pl.BlockSpec((1, d), lambda i, dr: (0, 0)),
                      pl.BlockSpec(memory_space=pl.ANY)],
            out_specs=pl.BlockSpec((tb, d), lambda i, dr: (i, 0)),
            scratch_shapes=[pltpu.VMEM((2, EXPERT_TOPK, tb, d // 2), jnp.uint32), pltpu.SemaphoreType.DMA((2,))],
        ),
        compiler_params=_cp("arbitrary"),
        name="moe_combine",
    )(dest_flat, x, wts_tok, g_final.reshape(1, d), ys)


def hier_moe_final(x2, g_ffn, w_rg, b_rg, w_re, w_gate, w_up, w_down, g_final):
    t, d = x2.shape
    xn, eid, wts = moe_route(x2, g_ffn, w_rg, b_rg, w_re)
    n = EXPERT_TOPK * t
    e_flat = eid.reshape(1, n)
    rank, counts_f = moe_rank(e_flat)
    counts = counts_f.reshape(N_EXPERTS).astype(jnp.int32)
    padded = (counts + MOE_BLOCK - 1) // MOE_BLOCK * MOE_BLOCK
    pad_end = jnp.cumsum(padded)
    pad_start = pad_end - padded
    dest = (jnp.take(pad_start, e_flat[0]) + rank[0]).astype(jnp.int32)
    nblk = -(-n // MOE_BLOCK) + N_EXPERTS
    blk_start = jnp.arange(nblk, dtype=jnp.int32) * MOE_BLOCK
    blk_expert = jnp.minimum(jnp.sum(pad_end[None, :] <= blk_start[:, None], axis=1), N_EXPERTS - 1).astype(jnp.int32)
    n_used = (pad_end[-1:] // MOE_BLOCK).astype(jnp.int32)
    xs = moe_dispatch(xn, dest, counts, pad_start.astype(jnp.int32), nblk * MOE_BLOCK)
    ys = moe_experts(xs, blk_expert, n_used, w_gate, w_up, w_down)
    return moe_combine(x2, ys, dest, wts.T, g_final)


def kernel(x, mem, g_mix, w_in, b_gate, gdn_conv, gdn_a_log, gdn_dt_bias, gdn_norm_g, g_mem, w_mem_kv, w_o_swa, w_o_gdn, w_o_mem, w_out, g_ffn, w_route_group, b_route_group, w_route_expert, w_expert_gate, w_expert_up, w_expert_down, g_final):
    batch, seq, d = x.shape
    depth = w_in.shape[0]
    assert depth == 1, "the final RMSNorm is fused into the layer's MoE combine"
    t = batch * seq
    mem_tokens = mem.shape[1]
    h = x.reshape(t, d)
    o0 = 3 * SWA_WIDTH
    o1 = o0 + 3 * GDN_WIDTH
    o2 = o1 + GDN_WIDTH
    o3 = o2 + 4 * GDN_HEADS
    o4 = o3 + MEM_WIDTH
    tables = rope_tables(seq)
    for l in range(depth):
        w = w_in[l].astype(BF16)
        a = rmsnorm_rows(h, g_mix[l], BF16)
        qkv_a = matmul(a, w, F32, n=o0, rope=(tables, seq))
        o_a = dilated_swa(qkv_a, batch, seq)
        qkv_b = matmul_conv_norm(a, w, o0, gdn_conv[l], seq)
        z_b = matmul(a, w, BF16, n=GDN_WIDTH, col_off=o1)
        w_ba = jnp.zeros((d, LANES), BF16).at[:, :o3 - o2].set(w[:, o2:o3])
        beta, gc = gdn_gates(matmul(a, w_ba, F32), gdn_a_log[l], gdn_dt_bias[l])
        o_f, o_r = gdn_scan(qkv_b, beta, gc, batch, seq)
        o_b = gdn_post(o_f, o_r, z_b, gdn_norm_g[l])
        kv = matmul(rmsnorm_rows(mem.reshape(batch * mem_tokens, d), g_mem[l], BF16), w_mem_kv[l].astype(BF16), BF16)
        o_m = mem_attention(matmul(a, w[:, o3:o4], BF16), kv, batch, seq, mem_tokens)
        gates = matmul(a, w[:, o4:], BF16, bias=b_gate[l].reshape(-1))
        mix = gated_merge(o_a, o_b, o_m, w_o_swa[l].astype(BF16), w_o_gdn[l].astype(BF16),
                          w_o_mem[l].astype(BF16), gates)
        x2 = matmul(mix, w_out[l].astype(BF16), F32, residual=h)
        h = hier_moe_final(x2, g_ffn[l], w_route_group[l], b_route_group[l], w_route_expert[l],
                           w_expert_gate[l], w_expert_up[l], w_expert_down[l], g_final)
    return h.reshape(batch, seq, d)
```

```python
import functools

import jax
import jax.numpy as jnp
from jax import lax
from jax.experimental import pallas as pl
from jax.experimental.pallas import tpu as pltpu

F32 = jnp.float32
BF16 = jnp.bfloat16

HEAD_DIM = 128
SWA_GROUPS = ((128, 1), (512, 4), (2048, 16))
SWA_HEADS_PER_GROUP = 4
SWA_HEADS = SWA_HEADS_PER_GROUP * len(SWA_GROUPS)
SWA_WIDTH = SWA_HEADS * HEAD_DIM
SWA_BLOCK = 64
SWA_UNROLL = 4
ROPE_THETA = 500000.0
ROPE_DIMS = HEAD_DIM // 4
GDN_HEADS = 12
GDN_WIDTH = GDN_HEADS * HEAD_DIM
GDN_CONV = 5
GDN_CHUNK = 64
MEM_HEADS = 4
MEM_HEAD_DIM = 256
MEM_WIDTH = MEM_HEADS * MEM_HEAD_DIM
N_BRANCH = 3
N_GROUPS = 4
EXPERTS_PER_GROUP = 8
N_EXPERTS = N_GROUPS * EXPERTS_PER_GROUP
EXPERT_TOPK = 2
MOE_BLOCK = 512
EPS = 1e-6
NEG_INF = -1e30

LANES = 128
SUBLANES = 8
VMEM_LIMIT = 48 * 1024 * 1024


def _cp(*sem, vmem=VMEM_LIMIT):
    return pltpu.CompilerParams(dimension_semantics=sem, vmem_limit_bytes=vmem)


def _tile(n, pref):
    t = min(n, pref)
    while n % t:
        t //= 2
    return t


def _rmsnorm_kernel(x_ref, g_ref, o_ref):
    x = x_ref[...].astype(F32)
    r = lax.rsqrt(jnp.mean(x * x, axis=-1, keepdims=True) + EPS)
    o_ref[...] = (x * r * g_ref[...]).astype(o_ref.dtype)


def rmsnorm_rows(x, g, out_dtype):
    m, d = x.shape
    tm = _tile(m, 512)
    return pl.pallas_call(
        _rmsnorm_kernel,
        out_shape=jax.ShapeDtypeStruct((m, d), out_dtype),
        grid=(m // tm,),
        in_specs=[pl.BlockSpec((tm, d), lambda i: (i, 0)), pl.BlockSpec((1, d), lambda i: (0, 0))],
        out_specs=pl.BlockSpec((tm, d), lambda i: (i, 0)),
        compiler_params=_cp("parallel"),
        name="rmsnorm_rows",
    )(x, g.reshape(1, d))


def _mm_kernel(a_ref, w_ref, o_ref):
    o_ref[...] = jnp.dot(a_ref[...], w_ref[...], preferred_element_type=F32).astype(o_ref.dtype)


def _mm_sigmoid_kernel(a_ref, w_ref, b_ref, o_ref):
    z = jnp.dot(a_ref[...], w_ref[...], preferred_element_type=F32) + b_ref[...]
    o_ref[...] = jax.nn.sigmoid(z).astype(o_ref.dtype)


def _mm_residual_kernel(a_ref, w_ref, r_ref, o_ref):
    o_ref[...] = r_ref[...] + jnp.dot(a_ref[...], w_ref[...], preferred_element_type=F32)


EPI_COLS = 2 * HEAD_DIM


def _mm_rope_kernel(a_ref, w_ref, c_ref, s1_ref, s2_ref, o_ref):
    c, s1, s2 = c_ref[...], s1_ref[...], s2_ref[...]
    half = ROPE_DIMS // 2
    for c0 in range(0, o_ref.shape[1], EPI_COLS):
        acc = jnp.dot(a_ref[...], w_ref[:, c0:c0 + EPI_COLS], preferred_element_type=F32)
        for h0 in range(0, EPI_COLS, HEAD_DIM):
            xh = acc[:, h0:h0 + HEAD_DIM]
            o_ref[:, c0 + h0:c0 + h0 + HEAD_DIM] = (
                xh * c + pltpu.roll(xh, HEAD_DIM - half, 1) * s1 + pltpu.roll(xh, half, 1) * s2)


MM_TM = 1024
MM_TN = (1536, 1024, 512)


def _col_tile(n):
    return next((t for t in MM_TN if n % t == 0), n)


def matmul(a, w, out_dtype, *, n=None, col_off=0, bias=None, residual=None, rope=None):
    m, k = a.shape
    n = w.shape[1] if n is None else n
    tm, tn = _tile(m, MM_TM), _col_tile(n)
    assert col_off % tn == 0
    joff = col_off // tn
    a_spec = pl.BlockSpec((tm, k), lambda i, j: (i, 0))
    w_spec = pl.BlockSpec((k, tn), lambda i, j: (0, joff + j))
    o_spec = pl.BlockSpec((tm, tn), lambda i, j: (i, j))
    if bias is not None:
        kern, extra, extra_specs = _mm_sigmoid_kernel, (bias.reshape(1, n),), [pl.BlockSpec((1, tn), lambda i, j: (0, j))]
    elif residual is not None:
        kern, extra, extra_specs = _mm_residual_kernel, (residual,), [o_spec]
    elif rope is not None:
        tables, seq = rope
        assert tn == SWA_WIDTH and n == 3 * SWA_WIDTH
        per = seq // tm
        t_spec = pl.BlockSpec((None, tm, HEAD_DIM), lambda i, j: (j, i % per, 0))
        kern = _mm_rope_kernel
        extra, extra_specs = tuple(tables), [t_spec] * 3
    else:
        kern, extra, extra_specs = _mm_kernel, (), []
    return pl.pallas_call(
        kern,
        out_shape=jax.ShapeDtypeStruct((m, n), out_dtype),
        grid=(m // tm, n // tn),
        in_specs=[a_spec, w_spec] + extra_specs,
        out_specs=o_spec,
        compiler_params=_cp("parallel", "arbitrary"),
        name="matmul",
    )(a, w, *extra)


def rope_tables(seq):
    half = ROPE_DIMS // 2
    inv = ROPE_THETA ** (-jnp.arange(half, dtype=F32) / half)
    ang = jnp.arange(seq, dtype=F32)[:, None] * inv[None, :]
    cos, sin = jnp.cos(ang), jnp.sin(ang)
    zeros = jnp.zeros((seq, HEAD_DIM - ROPE_DIMS), F32)
    zh = jnp.zeros((seq, half), F32)
    c = jnp.concatenate([cos, cos, zeros + 1.0], axis=1)
    s1 = jnp.concatenate([-sin, zh, zeros], axis=1)
    s2 = jnp.concatenate([zh, sin, zeros], axis=1)
    ident = (jnp.ones_like(c), jnp.zeros_like(c), jnp.zeros_like(c))
    return tuple(jnp.stack([tab * (HEAD_DIM ** -0.5), tab, idt]) for tab, idt in zip((c, s1, s2), ident))


CONV_HALO = 16


def _mm_conv_kernel(a_ref, ap_ref, an_ref, w_ref, cw_ref, o_ref, buf_ref, *, tm, tiles_per_seq):
    i, kind = pl.program_id(0), pl.program_id(1)
    halo = CONV_HALO
    rows = tm + 2 * halo
    keep_prev = jnp.where((i % tiles_per_seq) == 0, 0.0, 1.0)
    keep_next = jnp.where((i % tiles_per_seq) == tiles_per_seq - 1, 0.0, 1.0)
    qk_mul = jnp.where(kind == 0, HEAD_DIM ** -0.5, 1.0)
    w = w_ref[...]
    buf_ref[0:halo, :] = jnp.dot(ap_ref[...], w, preferred_element_type=F32) * keep_prev
    buf_ref[halo:halo + tm, :] = jnp.dot(a_ref[...], w, preferred_element_type=F32)
    buf_ref[halo + tm:, :] = jnp.dot(an_ref[...], w, preferred_element_type=F32) * keep_next
    for h in range(GDN_HEADS):
        cols = slice(h * HEAD_DIM, (h + 1) * HEAD_DIM)
        xh = buf_ref[:, cols]
        acc = xh * cw_ref[GDN_CONV // 2:GDN_CONV // 2 + 1, cols]
        for t in range(GDN_CONV):
            if t != GDN_CONV // 2:
                acc = acc + pltpu.roll(xh, (GDN_CONV // 2 - t) % rows, 0) * cw_ref[t:t + 1, cols]
        acc = acc[halo:halo + tm, :]
        y = acc * jax.nn.sigmoid(acc)
        inv = lax.rsqrt(jnp.sum(y * y, axis=-1, keepdims=True) + EPS) * qk_mul
        o_ref[h] = (y * jnp.where(kind < 2, inv, 1.0)).astype(o_ref.dtype)


def matmul_conv_norm(a, w, col_off, conv_w, seq):
    m, k = a.shape
    tn = GDN_WIDTH
    tm = _tile(seq, MM_TM)
    assert col_off % tn == 0 and tm % CONV_HALO == 0
    joff = col_off // tn
    per_h = tm // CONV_HALO
    nblk_h = m // CONV_HALO
    return pl.pallas_call(
        functools.partial(_mm_conv_kernel, tm=tm, tiles_per_seq=seq // tm),
        out_shape=jax.ShapeDtypeStruct((3 * GDN_HEADS, m, HEAD_DIM), BF16),
        grid=(m // tm, 3),
        in_specs=[
            pl.BlockSpec((tm, k), lambda i, j: (i, 0)),
            pl.BlockSpec((CONV_HALO, k), lambda i, j: (jnp.maximum(i * per_h - 1, 0), 0)),
            pl.BlockSpec((CONV_HALO, k), lambda i, j: (jnp.minimum((i + 1) * per_h, nblk_h - 1), 0)),
            pl.BlockSpec((k, tn), lambda i, j: (0, joff + j)),
            pl.BlockSpec((GDN_CONV, tn), lambda i, j: (0, j)),
        ],
        out_specs=pl.BlockSpec((GDN_HEADS, tm, HEAD_DIM), lambda i, j: (j, i, 0)),
        scratch_shapes=[pltpu.VMEM((tm + 2 * CONV_HALO, tn), F32)],
        compiler_params=_cp("parallel", "arbitrary"),
        name="matmul_conv_norm",
    )(a, a, a, w, conv_w)


def _swa_kernel(q_ref, k_ref, v_ref, o_ref, acc_ref, m_ref, l_ref, *, seq):
    g = pl.program_id(2)

    @pl.when(g == 0)
    def _():
        acc_ref[...] = jnp.zeros_like(acc_ref)
        m_ref[...] = jnp.full_like(m_ref, NEG_INF)
        l_ref[...] = jnp.zeros_like(l_ref)

    for gi, (window, dil) in enumerate(SWA_GROUPS):
        radius = window // (2 * dil)
        assert radius <= SWA_BLOCK
        sub = seq // dil
        qb = min(128, sub)
        win = min(sub, qb + 2 * SWA_BLOCK)
        nqb = sub // qb

        @pl.when(g == gi)
        def _(dil=dil, radius=radius, sub=sub, qb=qb, win=win, nqb=nqb):
            def rows(start, size):
                return pl.ds(start, size) if dil == 1 else pl.ds(start, size, stride=dil)

            total = dil * nqb
            unroll = SWA_UNROLL if total % SWA_UNROLL == 0 else 1
            idx = range(unroll)

            def body(step, carry):
                its = [step * unroll + u for u in idx]
                qss = [(it % nqb) * qb for it in its]
                wss = [jnp.clip(qs - SWA_BLOCK, 0, sub - win) for qs in qss]
                q_rows = [rows(it // nqb + dil * qs, qb) for it, qs in zip(its, qss)]
                k_rows = [rows(it // nqb + dil * ws, win) for it, ws in zip(its, wss)]
                q = [q_ref[r, :].astype(BF16) for r in q_rows]
                k = [k_ref[r, :].astype(BF16) for r in k_rows]
                v = [v_ref[r, :].astype(BF16) for r in k_rows]
                s = [_nt(q[u], k[u]) for u in idx]
                off = lax.broadcasted_iota(jnp.int32, (qb, win), 0) - lax.broadcasted_iota(jnp.int32, (qb, win), 1)
                s = [jnp.where(jnp.abs(off + (qss[u] - wss[u])) <= radius, s[u], NEG_INF) for u in idx]
                m_old = [m_ref[r, :] for r in q_rows]
                m_new = [jnp.maximum(m_old[u], jnp.max(s[u], axis=1, keepdims=True)) for u in idx]
                p = [jnp.exp(s[u] - m_new[u]) for u in idx]
                corr = [jnp.exp(m_old[u] - m_new[u]) for u in idx]
                pv = [jnp.dot(p[u].astype(BF16), v[u], preferred_element_type=F32) for u in idx]
                for u in idx:
                    l_ref[q_rows[u], :] = l_ref[q_rows[u], :] * corr[u] + jnp.sum(p[u], axis=1, keepdims=True)
                    acc_ref[q_rows[u], :] = acc_ref[q_rows[u], :] * corr[u] + pv[u]
                    m_ref[q_rows[u], :] = m_new[u]
                return carry

            lax.fori_loop(0, total // unroll, body, 0)

    @pl.when(g == len(SWA_GROUPS) - 1)
    def _():
        o_ref[...] = (acc_ref[...] / l_ref[...]).astype(o_ref.dtype)


def dilated_swa(qkv, batch, seq):
    nh = SWA_HEADS_PER_GROUP
    qkv3 = qkv.reshape(batch, seq, 3 * SWA_WIDTH)

    def spec(off):
        return pl.BlockSpec((None, seq, HEAD_DIM), lambda b, h, g: (b, 0, off + g * nh + h))

    out = pl.pallas_call(
        functools.partial(_swa_kernel, seq=seq),
        out_shape=jax.ShapeDtypeStruct((batch, seq, nh * HEAD_DIM), BF16),
        grid=(batch, nh, len(SWA_GROUPS)),
        in_specs=[spec(0), spec(SWA_HEADS), spec(2 * SWA_HEADS)],
        out_specs=pl.BlockSpec((None, seq, HEAD_DIM), lambda b, h, g: (b, 0, h)),
        scratch_shapes=[pltpu.VMEM((seq, HEAD_DIM), F32), pltpu.VMEM((seq, 1), F32), pltpu.VMEM((seq, 1), F32)],
        compiler_params=_cp("parallel", "parallel", "arbitrary"),
        name="dilated_swa",
    )(qkv3, qkv3, qkv3)
    return out.reshape(batch * seq, nh * HEAD_DIM)


def _gdn_gate_kernel(ba_ref, alog_ref, dtb_ref, beta_ref, gc_ref, *, ts):
    x = ba_ref[...]
    nhd = 2 * GDN_HEADS
    lane = lax.broadcasted_iota(jnp.int32, (ts, LANES), 1)
    row = lax.broadcasted_iota(jnp.int32, (ts, LANES), 0) % GDN_CHUNK
    beta_ref[...] = jax.nn.sigmoid(x)
    z = x + dtb_ref[...]
    softplus = jnp.maximum(z, 0.0) + jnp.log(1.0 + jnp.exp(-jnp.abs(z)))
    g = jnp.where(jnp.logical_and(lane >= nhd, lane < 2 * nhd), -jnp.exp(alog_ref[...]) * softplus, 0.0)
    pre, suf = g, g
    s = 1
    while s < GDN_CHUNK:
        pre = pre + jnp.where(row >= s, pltpu.roll(pre, s, 0), 0.0)
        suf = suf + jnp.where(row < GDN_CHUNK - s, pltpu.roll(suf, ts - s, 0), 0.0)
        s *= 2
    gc_ref[...] = jnp.where(lane < nhd + GDN_HEADS, pre, suf)


def gdn_gates(ba, a_log, dt_bias):
    t = ba.shape[0]
    ts = _tile(t, 1024)
    nhd = 2 * GDN_HEADS
    pad = lambda v: jnp.zeros((1, LANES), F32).at[0, nhd:2 * nhd].set(v.reshape(-1).astype(F32))
    shp = jax.ShapeDtypeStruct((t, LANES), F32)
    vec = pl.BlockSpec((1, LANES), lambda i: (0, 0))
    blk = pl.BlockSpec((ts, LANES), lambda i: (i, 0))
    return pl.pallas_call(
        functools.partial(_gdn_gate_kernel, ts=ts),
        out_shape=(shp, shp),
        grid=(t // ts,),
        in_specs=[blk, vec, vec],
        out_specs=(blk, blk),
        compiler_params=_cp("parallel"),
        name="gdn_gates",
    )(ba, pad(a_log), pad(dt_bias))


def _nt(a, b):
    return lax.dot_general(a, b, (((1,), (1,)), ((), ())), preferred_element_type=F32)


def _tn(a, b):
    return lax.dot_general(a, b, (((0,), (0,)), ((), ())), preferred_element_type=F32)


def _mmb(a, b):
    return jnp.dot(a.astype(BF16), b.astype(BF16), preferred_element_type=F32)


def _gdn_chunks(qs, ks, vs, bcols, gcols, states, revs):
    c = GDN_CHUNK
    idx = range(len(qs))
    ri = lax.broadcasted_iota(jnp.int32, (c, c), 0)
    ci = lax.broadcasted_iota(jnp.int32, (c, c), 1)
    strict = {False: ri > ci, True: ri < ci}
    incl = {False: ri >= ci, True: ri <= ci}
    eye = jnp.where(ri == ci, 1.0, 0.0)
    lane = lax.broadcasted_iota(jnp.int32, (c, LANES), 1)
    ones3 = jnp.where(lane < 3, 1.0, 0.0)
    ones3_hi = jnp.where(jnp.logical_and(lane >= 3, lane < 6), 1.0, 0.0)
    g_last = [gcols[i][0:1, :] if revs[i] else gcols[i][c - 1:c, :] for i in idx]
    egc = [jnp.exp(gcols[i]) for i in idx]
    q_state = [_mmb(qs[i] * egc[i], states[i]) for i in idx]
    diff = []
    for i in idx:
        g1 = gcols[i].astype(BF16).astype(F32)
        g2 = (gcols[i] - g1).astype(BF16).astype(F32)
        g3 = (gcols[i] - g1 - g2).astype(BF16).astype(F32)
        pieces = jnp.where(lane == 0, g1, jnp.where(lane == 1, g2, jnp.where(lane == 2, g3, 0.0)))
        xm = pieces + ones3_hi
        ym = ones3 - pltpu.roll(pieces, 3, 1)
        diff.append(_nt(xm.astype(BF16), ym.astype(BF16)))
    decay = [jnp.where(incl[revs[i]], jnp.exp(jnp.where(incl[revs[i]], diff[i], 0.0)), 0.0) for i in idx]
    kb = [ks[i] * bcols[i] for i in idx]
    k16 = [ks[i].astype(BF16) for i in idx]
    lm = [jnp.where(strict[revs[i]], _nt(kb[i].astype(BF16), k16[i]) * decay[i], 0.0) for i in idx]
    attn = [_nt(qs[i].astype(BF16), k16[i]) * decay[i] for i in idx]
    tinv = [eye - lm[i] for i in idx]
    pw = [_mmb(lm[i], lm[i]) for i in idx]
    n = 4
    while n < c:
        both = [_mmb(jnp.concatenate([tinv[i], pw[i]], axis=0), pw[i]) for i in idx]
        tinv = [tinv[i] + both[i][:c] for i in idx]
        pw = [both[i][c:] for i in idx]
        n *= 2
    tinv = [tinv[i] + _mmb(tinv[i], pw[i]) for i in idx]
    u = [_mmb(tinv[i], jnp.concatenate([vs[i] * bcols[i], kb[i] * egc[i]], axis=1)) for i in idx]
    v_new = [u[i][:, :HEAD_DIM] - _mmb(u[i][:, HEAD_DIM:], states[i]) for i in idx]
    outs = [q_state[i] + _mmb(attn[i], v_new[i]) for i in idx]
    k_e = [ks[i] * jnp.exp(g_last[i] - gcols[i]) for i in idx]
    new_states = [states[i] * jnp.exp(g_last[i]) + _tn(k_e[i].astype(BF16), v_new[i].astype(BF16)) for i in idx]
    return outs, new_states


def _gdn_kernel(qf_ref, kf_ref, vf_ref, bf_ref, gf_ref, qb_ref, kb_ref, vb_ref, bb_ref, gb_ref,
                of_ref, ob_ref, state_ref, *, chunks):
    @pl.when(pl.program_id(1) == 0)
    def _():
        state_ref[...] = jnp.zeros_like(state_ref)

    nh = GDN_HEADS

    def body(it, carry):
        rows_f = pl.ds(pl.multiple_of(it * GDN_CHUNK, GDN_CHUNK), GDN_CHUNK)
        rows_b = pl.ds(pl.multiple_of((chunks - 1 - it) * GDN_CHUNK, GDN_CHUNK), GDN_CHUNK)
        tabs = ((qf_ref, kf_ref, vf_ref, bf_ref[rows_f, :], gf_ref[rows_f, :], rows_f, 0),
                (qb_ref, kb_ref, vb_ref, bb_ref[rows_b, :], gb_ref[rows_b, :], rows_b, nh))
        qs, ks, vs, bcols, gcols, states, revs = [], [], [], [], [], [], []
        for d, (q_ref, k_ref, v_ref, bt, gt, rows, lane0) in enumerate(tabs):
            for h in range(nh):
                qs.append(q_ref[h, rows, :].astype(F32))
                ks.append(k_ref[h, rows, :].astype(F32))
                vs.append(v_ref[h, rows, :].astype(F32))
                bcols.append(bt[:, lane0 + h:lane0 + h + 1])
                gcols.append(gt[:, 2 * nh + lane0 + h:2 * nh + lane0 + h + 1])
                states.append(state_ref[d * nh + h])
                revs.append(d == 1)
        outs, new_states = _gdn_chunks(qs, ks, vs, bcols, gcols, states, revs)
        for d, o_ref in enumerate((of_ref, ob_ref)):
            for h in range(nh):
                state_ref[d * nh + h] = new_states[d * nh + h]
                o_ref[h, tabs[d][5], :] = outs[d * nh + h].astype(o_ref.dtype)
        return carry

    lax.fori_loop(0, chunks, body, 0)


def gdn_scan(qkv_h, beta, gc, batch, seq):
    t = batch * seq
    rows = _tile(seq, 512)
    nblk = seq // rows
    fwd = lambda b, ib: b * nblk + ib
    bwd = lambda b, ib: b * nblk + nblk - 1 - ib

    def specs(blk):
        qspec = lambda part: pl.BlockSpec((GDN_HEADS, rows, HEAD_DIM), lambda b, ib: (part, blk(b, ib), 0))
        gspec = pl.BlockSpec((rows, LANES), lambda b, ib: (blk(b, ib), 0))
        return [qspec(0), qspec(1), qspec(2), gspec, gspec]

    ospec = lambda blk: pl.BlockSpec((GDN_HEADS, rows, HEAD_DIM), lambda b, ib: (0, blk(b, ib), 0))
    out = jax.ShapeDtypeStruct((GDN_HEADS, t, HEAD_DIM), BF16)
    return pl.pallas_call(
        functools.partial(_gdn_kernel, chunks=rows // GDN_CHUNK),
        out_shape=(out, out),
        grid=(batch, nblk),
        in_specs=specs(fwd) + specs(bwd),
        out_specs=(ospec(fwd), ospec(bwd)),
        scratch_shapes=[pltpu.VMEM((2 * GDN_HEADS, HEAD_DIM, HEAD_DIM), F32)],
        compiler_params=_cp("parallel", "arbitrary"),
        name="gdn_scan",
    )(qkv_h, qkv_h, qkv_h, beta, gc, qkv_h, qkv_h, qkv_h, beta, gc)


def _gdn_post_kernel(of_ref, ob_ref, z_ref, g_ref, o_ref):
    for h in range(of_ref.shape[0]):
        cols = slice(h * HEAD_DIM, (h + 1) * HEAD_DIM)
        o = of_ref[h].astype(F32) + ob_ref[h].astype(F32)
        r = lax.rsqrt(jnp.mean(o * o, axis=-1, keepdims=True) + EPS)
        z = z_ref[:, cols].astype(F32)
        o_ref[:, cols] = (o * r * g_ref[...] * (z * jax.nn.sigmoid(z))).astype(o_ref.dtype)


def gdn_post(o_f, o_b, z, norm_g):
    nhd, t, _ = o_f.shape
    ts = _tile(t, 512)
    hspec = pl.BlockSpec((nhd, ts, HEAD_DIM), lambda i: (0, i, 0))
    tspec = pl.BlockSpec((ts, nhd * HEAD_DIM), lambda i: (i, 0))
    return pl.pallas_call(
        _gdn_post_kernel,
        out_shape=jax.ShapeDtypeStruct((t, nhd * HEAD_DIM), BF16),
        grid=(t // ts,),
        in_specs=[hspec, hspec, tspec, pl.BlockSpec((1, HEAD_DIM), lambda i: (0, 0))],
        out_specs=tspec,
        compiler_params=_cp("parallel"),
        name="gdn_post",
    )(o_f, o_b, z, norm_g.reshape(1, HEAD_DIM).astype(F32))


def _mem_attn_kernel(q_ref, k_ref, v_ref, o_ref):
    scale = MEM_HEAD_DIM ** -0.5
    for h in range(MEM_HEADS):
        cols = slice(h * MEM_HEAD_DIM, (h + 1) * MEM_HEAD_DIM)
        s = _nt(q_ref[:, cols], k_ref[:, cols]) * scale
        p = jnp.exp(s - jnp.max(s, axis=-1, keepdims=True))
        p = p / jnp.sum(p, axis=-1, keepdims=True)
        o_ref[:, cols] = jnp.dot(p.astype(BF16), v_ref[:, cols], preferred_element_type=F32).astype(o_ref.dtype)


def mem_attention(mq, kv, batch, seq, mem_tokens):
    tq = _tile(seq, 512)
    per = seq // tq
    return pl.pallas_call(
        _mem_attn_kernel,
        out_shape=jax.ShapeDtypeStruct(mq.shape, BF16),
        grid=(batch * per,),
        in_specs=[
            pl.BlockSpec((tq, MEM_WIDTH), lambda i: (i, 0)),
            pl.BlockSpec((mem_tokens, MEM_WIDTH), lambda i: (i // per, 0)),
            pl.BlockSpec((mem_tokens, MEM_WIDTH), lambda i: (i // per, 1)),
        ],
        out_specs=pl.BlockSpec((tq, MEM_WIDTH), lambda i: (i, 0)),
        compiler_params=_cp("parallel"),
        name="mem_attention",
    )(mq, kv, kv)


def _merge_kernel(oa_ref, ob_ref, om_ref, wa_ref, wb_ref, wm_ref, g0_ref, g1_ref, g2_ref, o_ref):
    ya = jnp.dot(oa_ref[...], wa_ref[...], preferred_element_type=F32)
    yb = jnp.dot(ob_ref[...], wb_ref[...], preferred_element_type=F32)
    ym = jnp.dot(om_ref[...], wm_ref[...], preferred_element_type=F32)
    mix = g0_ref[...].astype(F32) * ya + g1_ref[...].astype(F32) * yb + g2_ref[...].astype(F32) * ym
    o_ref[...] = mix.astype(o_ref.dtype)


def gated_merge(o_a, o_b, o_m, w_a, w_b, w_m, gates):
    t = o_a.shape[0]
    d = w_a.shape[1]
    tm, tn = _tile(t, 1024), _tile(d, 512)
    nj = d // tn
    act = lambda o: pl.BlockSpec((tm, o.shape[1]), lambda i, j: (i, 0))
    wsp = lambda w: pl.BlockSpec((w.shape[0], tn), lambda i, j: (0, j))
    gsp = lambda br: pl.BlockSpec((tm, tn), lambda i, j: (i, br * nj + j))
    return pl.pallas_call(
        _merge_kernel,
        out_shape=jax.ShapeDtypeStruct((t, d), BF16),
        grid=(t // tm, nj),
        in_specs=[act(o_a), act(o_b), act(o_m), wsp(w_a), wsp(w_b), wsp(w_m), gsp(0), gsp(1), gsp(2)],
        out_specs=pl.BlockSpec((tm, tn), lambda i, j: (i, j)),
        compiler_params=_cp("parallel", "arbitrary"),
        name="gated_merge",
    )(o_a, o_b, o_m, w_a, w_b, w_m, gates, gates, gates)


def _pack_bf16_pairs(x):
    w = x.shape[1] // 2
    lo = lax.bitcast_convert_type(x[:, :w].astype(BF16).astype(F32), jnp.uint32)
    hi = lax.bitcast_convert_type(x[:, w:].astype(BF16).astype(F32), jnp.uint32)
    return (hi & jnp.uint32(0xFFFF0000)) | (lo >> 16)


def _unpack_bf16_pairs(p):
    lo = lax.bitcast_convert_type(p << 16, F32)
    hi = lax.bitcast_convert_type(p & jnp.uint32(0xFFFF0000), F32)
    return lo, hi


MOE_ROWS_PER_STEP = 128
ROUTE_ROWS = SUBLANES + N_EXPERTS


def _route_kernel(x_ref, g_ref, w_ref, b_ref, xn_ref, eid_ref, wt_ref):
    x = x_ref[...]
    r = lax.rsqrt(jnp.mean(x * x, axis=-1, keepdims=True) + EPS)
    xn = x * r * g_ref[...]
    xn_ref[...] = _pack_bf16_pairs(xn)
    tm = x.shape[0]
    x_hi = xn.astype(BF16)
    x_lo = (xn - x_hi.astype(F32)).astype(BF16)
    w_hi, w_lo = w_ref[0], w_ref[1]
    lg = (jnp.dot(x_hi, w_hi, preferred_element_type=F32) + jnp.dot(x_lo, w_hi, preferred_element_type=F32)
          + jnp.dot(x_hi, w_lo, preferred_element_type=F32))
    lg = lg.T[:ROUTE_ROWS, :] + b_ref[...]
    gl = [lg[i:i + 1, :] for i in range(N_GROUPS)]
    gmax = functools.reduce(jnp.maximum, gl)
    grp = jnp.full((1, tm), N_GROUPS - 1, jnp.int32)
    for i in range(N_GROUPS - 2, -1, -1):
        grp = jnp.where(gl[i] == gmax, i, grp)
    p_grp = 1.0 / functools.reduce(jnp.add, [jnp.exp(v - gmax) for v in gl])
    sel = lg[SUBLANES + (N_GROUPS - 1) * EXPERTS_PER_GROUP:SUBLANES + N_GROUPS * EXPERTS_PER_GROUP, :]
    for i in range(N_GROUPS - 2, -1, -1):
        lo = SUBLANES + i * EXPERTS_PER_GROUP
        sel = jnp.where(grp == i, lg[lo:lo + EXPERTS_PER_GROUP, :], sel)
    rowi = lax.broadcasted_iota(jnp.int32, (EXPERTS_PER_GROUP, tm), 0)
    v1 = jnp.max(sel, axis=0, keepdims=True)
    i1 = jnp.min(jnp.where(sel == v1, rowi, EXPERTS_PER_GROUP), axis=0, keepdims=True)
    sel2 = jnp.where(rowi == i1, -jnp.inf, sel)
    v2 = jnp.max(sel2, axis=0, keepdims=True)
    i2 = jnp.min(jnp.where(sel2 == v2, rowi, EXPERTS_PER_GROUP), axis=0, keepdims=True)
    e2 = jnp.exp(v2 - v1)
    w1 = p_grp / (1.0 + e2)
    eid_ref[0:1, :] = grp * EXPERTS_PER_GROUP + i1
    eid_ref[1:2, :] = grp * EXPERTS_PER_GROUP + i2
    wt_ref[0:1, :] = w1
    wt_ref[1:2, :] = w1 * e2


def moe_route(x, g, w_grp, b_grp, w_exp):
    t, d = x.shape
    tm = _tile(t, 512)
    w = jnp.zeros((d, LANES), F32).at[:, :N_GROUPS].set(w_grp).at[:, SUBLANES:ROUTE_ROWS].set(w_exp)
    w_hi = w.astype(BF16)
    w = jnp.stack([w_hi, (w - w_hi.astype(F32)).astype(BF16)])
    b = jnp.zeros((ROUTE_ROWS, 1), F32).at[:N_GROUPS, 0].set(b_grp.astype(F32))
    return pl.pallas_call(
        _route_kernel,
        out_shape=(jax.ShapeDtypeStruct((t, d // 2), jnp.uint32), jax.ShapeDtypeStruct((EXPERT_TOPK, t), jnp.int32),
                   jax.ShapeDtypeStruct((EXPERT_TOPK, t), F32)),
        grid=(t // tm,),
        in_specs=[pl.BlockSpec((tm, d), lambda i: (i, 0)), pl.BlockSpec((1, d), lambda i: (0, 0)),
                  pl.BlockSpec((2, d, LANES), lambda i: (0, 0, 0)), pl.BlockSpec((ROUTE_ROWS, 1), lambda i: (0, 0))],
        out_specs=(pl.BlockSpec((tm, d // 2), lambda i: (i, 0)), pl.BlockSpec((EXPERT_TOPK, tm), lambda i: (0, i)),
                   pl.BlockSpec((EXPERT_TOPK, tm), lambda i: (0, i))),
        compiler_params=_cp("parallel"),
        name="moe_route",
    )(x, g.reshape(1, d), w, b)


def _rank_kernel(e_ref, rank_ref, cnt_ref, *, tb):
    @pl.when(pl.program_id(0) == 0)
    def _():
        cnt_ref[...] = jnp.zeros_like(cnt_ref)

    e = e_ref[...]
    onehot = jnp.where(lax.broadcasted_iota(jnp.int32, (N_EXPERTS, tb), 0) == e, 1.0, 0.0)
    earlier = jnp.where(lax.broadcasted_iota(jnp.int32, (tb, tb), 0) < lax.broadcasted_iota(jnp.int32, (tb, tb), 1),
                        1.0, 0.0)
    before = _mmb(onehot, earlier) + cnt_ref[...]
    rank_ref[...] = jnp.sum(onehot * before, axis=0, keepdims=True).astype(jnp.int32)
    cnt_ref[...] = cnt_ref[...] + jnp.sum(onehot, axis=1, keepdims=True)


def moe_rank(e_flat):
    n = e_flat.shape[1]
    tb = _tile(n, 512)
    return pl.pallas_call(
        functools.partial(_rank_kernel, tb=tb),
        out_shape=(jax.ShapeDtypeStruct((1, n), jnp.int32), jax.ShapeDtypeStruct((N_EXPERTS, 1), F32)),
        grid=(n // tb,),
        in_specs=[pl.BlockSpec((1, tb), lambda i: (0, i))],
        out_specs=(pl.BlockSpec((1, tb), lambda i: (0, i)), pl.BlockSpec((N_EXPERTS, 1), lambda i: (0, 0))),
        compiler_params=_cp("arbitrary"),
        name="moe_rank",
    )(e_flat)


def _dispatch_kernel(dest_ref, cnt_ref, pstart_ref, xn_ref, xs_ref, xbuf_ref, zero_ref, load_sem, scat_sem, fill_sem,
                     *, tb, tokens, nblk):
    i, nsteps = pl.program_id(0), pl.num_programs(0)
    slot = i % 2

    def load(step, sl):
        return pltpu.make_async_copy(xn_ref.at[pl.ds(step * tb, tb)], xbuf_ref.at[sl], load_sem.at[sl])

    def scatters(step, sl):
        return [pltpu.make_async_copy(xbuf_ref.at[sl, pl.ds(j, 1)],
                                      xs_ref.at[pl.ds(dest_ref[kk * tokens + step * tb + j], 1)], scat_sem.at[sl])
                for j in range(tb) for kk in range(EXPERT_TOPK)]

    def zero_rows(dst_row, nrows):
        return pltpu.make_async_copy(zero_ref.at[pl.ds(0, nrows)], xs_ref.at[pl.ds(dst_row, nrows)], fill_sem)

    @pl.when(i == 0)
    def _():
        load(0, 0).start()
        zero_ref[...] = jnp.zeros_like(zero_ref)
        last = N_EXPERTS - 1
        n_used = (pstart_ref[last] + cnt_ref[last] + MOE_BLOCK - 1) // MOE_BLOCK

        def fill_block(blk, carry):
            zero_rows(blk * MOE_BLOCK, MOE_BLOCK).start()
            zero_rows(blk * MOE_BLOCK, MOE_BLOCK).wait()
            return carry

        lax.fori_loop(n_used, nblk, fill_block, 0)

        def per_expert(e, carry):
            npad = (-cnt_ref[e]) % MOE_BLOCK
            base = pstart_ref[e] + cnt_ref[e]

            def fill(r, c2):
                zero_rows(base + r, 1).start()
                return c2

            lax.fori_loop(0, npad, fill, 0)

            def drain(r, c2):
                zero_rows(base + r, 1).wait()
                return c2

            lax.fori_loop(0, npad, drain, 0)
            return carry

        lax.fori_loop(0, N_EXPERTS, per_expert, 0)

    @pl.when(i >= 1)
    def _():
        for cp in scatters(i - 1, 1 - slot):
            cp.wait()

    @pl.when(i + 1 < nsteps)
    def _():
        load(i + 1, 1 - slot).start()

    load(i, slot).wait()
    for cp in scatters(i, slot):
        cp.start()

    @pl.when(i == nsteps - 1)
    def _():
        for cp in scatters(i, slot):
            cp.wait()


def moe_dispatch(xn, dest_flat, counts, pad_start, n_slots):
    t, d = xn.shape
    tb = _tile(t, MOE_ROWS_PER_STEP)
    return pl.pallas_call(
        functools.partial(_dispatch_kernel, tb=tb, tokens=t, nblk=n_slots // MOE_BLOCK),
        out_shape=jax.ShapeDtypeStruct((n_slots, d), xn.dtype),
        grid_spec=pltpu.PrefetchScalarGridSpec(
            num_scalar_prefetch=3,
            grid=(t // tb,),
            in_specs=[pl.BlockSpec(memory_space=pl.ANY)],
            out_specs=pl.BlockSpec(memory_space=pl.ANY),
            scratch_shapes=[pltpu.VMEM((2, tb, d), xn.dtype), pltpu.VMEM((MOE_BLOCK, d), xn.dtype),
                            pltpu.SemaphoreType.DMA((2,)), pltpu.SemaphoreType.DMA((2,)), pltpu.SemaphoreType.DMA],
        ),
        compiler_params=pltpu.CompilerParams(dimension_semantics=("arbitrary",), has_side_effects=True,
                                             vmem_limit_bytes=VMEM_LIMIT),
        name="moe_dispatch",
    )(dest_flat, counts, pad_start, xn)


def _expert_kernel(be_ref, nused_ref, x_ref, wg_ref, wu_ref, wd_ref, y_ref, wg16_ref, wu16_ref, wd16_ref):
    i = pl.program_id(0)
    used = i < nused_ref[0]

    @pl.when(jnp.logical_and(used, jnp.logical_or(i == 0, be_ref[i] != be_ref[jnp.maximum(i - 1, 0)])))
    def _():
        wg16_ref[...] = wg_ref[...].astype(BF16)
        wu16_ref[...] = wu_ref[...].astype(BF16)
        wd16_ref[...] = wd_ref[...].astype(BF16)

    @pl.when(used)
    def _():
        x_lo, x_hi = (v.astype(BF16) for v in _unpack_bf16_pairs(x_ref[...]))
        half = x_lo.shape[1]

        def proj(w_ref):
            return (jnp.dot(x_lo, w_ref[:half, :], preferred_element_type=F32)
                    + jnp.dot(x_hi, w_ref[half:, :], preferred_element_type=F32))

        hg, hu = proj(wg16_ref), proj(wu16_ref)
        hmid = (hg * jax.nn.sigmoid(hg) * hu).astype(BF16)
        y_ref[...] = _pack_bf16_pairs(jnp.dot(hmid, wd16_ref[...], preferred_element_type=F32))

    @pl.when(pl.program_id(0) >= nused_ref[0])
    def _():
        y_ref[...] = jnp.zeros_like(y_ref)


def moe_experts(xs, blk_expert, n_used, w_gate, w_up, w_down):
    p, dp = xs.shape
    d, de = w_gate.shape[1:]
    nblk = p // MOE_BLOCK
    row = lambda i, be, nu: (jnp.minimum(i, nu[0] - 1), 0)
    wsel = lambda i, be, nu: (be[jnp.minimum(i, nu[0] - 1)], 0, 0)
    return pl.pallas_call(
        _expert_kernel,
        out_shape=jax.ShapeDtypeStruct((p, dp), jnp.uint32),
        grid_spec=pltpu.PrefetchScalarGridSpec(
            num_scalar_prefetch=2,
            grid=(nblk,),
            in_specs=[pl.BlockSpec((MOE_BLOCK, dp), row), pl.BlockSpec((None, d, de), wsel),
                      pl.BlockSpec((None, d, de), wsel), pl.BlockSpec((None, de, d), wsel)],
            out_specs=pl.BlockSpec((MOE_BLOCK, dp), lambda i, be, nu: (i, 0)),
            scratch_shapes=[pltpu.VMEM((d, de), BF16), pltpu.VMEM((d, de), BF16), pltpu.VMEM((de, d), BF16)],
        ),
        compiler_params=_cp("arbitrary"),
        name="moe_experts",
    )(blk_expert, n_used, xs, w_gate, w_up, w_down)


def _combine_kernel(dest_ref, x_ref, wt_ref, g_ref, ys_ref, o_ref, y_ref, sems, *, tb, tokens):
    i, nsteps = pl.program_id(0), pl.num_programs(0)
    slot = i % 2

    def gathers(step, sl):
        return [pltpu.make_async_copy(ys_ref.at[pl.ds(dest_ref[kk * tokens + step * tb + j], 1)],
                                      y_ref.at[sl, kk, pl.ds(j, 1)], sems.at[sl])
                for j in range(tb) for kk in range(EXPERT_TOPK)]

    @pl.when(i == 0)
    def _():
        for cp in gathers(0, 0):
            cp.start()

    @pl.when(i + 1 < nsteps)
    def _():
        for cp in gathers(i + 1, 1 - slot):
            cp.start()

    for cp in gathers(i, slot):
        cp.wait()
    y0_lo, y0_hi = _unpack_bf16_pairs(y_ref[slot, 0])
    y1_lo, y1_hi = _unpack_bf16_pairs(y_ref[slot, 1])
    w0, w1 = wt_ref[:, 0:1], wt_ref[:, 1:2]
    h = x_ref[...] + jnp.concatenate([w0 * y0_lo + w1 * y1_lo, w0 * y0_hi + w1 * y1_hi], axis=1)
    r = lax.rsqrt(jnp.mean(h * h, axis=-1, keepdims=True) + EPS)
    o_ref[...] = h * r * g_ref[...]


def moe_combine(x, ys, dest_flat, wts_tok, g_final):
    t, d = x.shape
    tb = _tile(t, MOE_ROWS_PER_STEP)
    return pl.pallas_call(
        functools.partial(_combine_kernel, tb=tb, tokens=t),
        out_shape=jax.ShapeDtypeStruct((t, d), F32),
        grid_spec=pltpu.PrefetchScalarGridSpec(
            num_scalar_prefetch=1,
            grid=(t // tb,),
            in_specs=[pl.BlockSpec((tb, d), lambda i, dr: (i, 0)),
                      pl.BlockSpec((tb, EXPERT_TOPK), lambda i, dr: (i, 0)),
                      pl.BlockSpec((1, d), lambda i, dr: (0, 0)),
                      pl.BlockSpec(memory_space=pl.ANY)],
            out_specs=pl.BlockSpec((tb, d), lambda i, dr: (i, 0)),
            scratch_shapes=[pltpu.VMEM((2, EXPERT_TOPK, tb, d // 2), jnp.uint32), pltpu.SemaphoreType.DMA((2,))],
        ),
        compiler_params=_cp("arbitrary"),
        name="moe_combine",
    )(dest_flat, x, wts_tok, g_final.reshape(1, d), ys)


def hier_moe_final(x2, g_ffn, w_rg, b_rg, w_re, w_gate, w_up, w_down, g_final):
    t, d = x2.shape
    xn, eid, wts = moe_route(x2, g_ffn, w_rg, b_rg, w_re)
    n = EXPERT_TOPK * t
    e_flat = eid.reshape(1, n)
    rank, counts_f = moe_rank(e_flat)
    counts = counts_f.reshape(N_EXPERTS).astype(jnp.int32)
    padded = (counts + MOE_BLOCK - 1) // MOE_BLOCK * MOE_BLOCK
    pad_end = jnp.cumsum(padded)
    pad_start = pad_end - padded
    dest = (jnp.take(pad_start, e_flat[0]) + rank[0]).astype(jnp.int32)
    nblk = -(-n // MOE_BLOCK) + N_EXPERTS
    blk_start = jnp.arange(nblk, dtype=jnp.int32) * MOE_BLOCK
    blk_expert = jnp.minimum(jnp.sum(pad_end[None, :] <= blk_start[:, None], axis=1), N_EXPERTS - 1).astype(jnp.int32)
    n_used = (pad_end[-1:] // MOE_BLOCK).astype(jnp.int32)
    xs = moe_dispatch(xn, dest, counts, pad_start.astype(jnp.int32), nblk * MOE_BLOCK)
    ys = moe_experts(xs, blk_expert, n_used, w_gate, w_up, w_down)
    return moe_combine(x2, ys, dest, wts.T, g_final)


def kernel(x, mem, g_mix, w_in, b_gate, gdn_conv, gdn_a_log, gdn_dt_bias, gdn_norm_g, g_mem, w_mem_kv, w_o_swa, w_o_gdn, w_o_mem, w_out, g_ffn, w_route_group, b_route_group, w_route_expert, w_expert_gate, w_expert_up, w_expert_down, g_final):
    batch, seq, d = x.shape
    depth = w_in.shape[0]
    assert depth == 1, "the final RMSNorm is fused into the layer's MoE combine"
    t = batch * seq
    mem_tokens = mem.shape[1]
    h = x.reshape(t, d)
    o0 = 3 * SWA_WIDTH
    o1 = o0 + 3 * GDN_WIDTH
    o2 = o1 + GDN_WIDTH
    o3 = o2 + 4 * GDN_HEADS
    o4 = o3 + MEM_WIDTH
    tables = rope_tables(seq)
    for l in range(depth):
        w = w_in[l].astype(BF16)
        a = rmsnorm_rows(h, g_mix[l], BF16)
        qkv_a = matmul(a, w, F32, n=o0, rope=(tables, seq))
        o_a = dilated_swa(qkv_a, batch, seq)
        qkv_b = matmul_conv_norm(a, w, o0, gdn_conv[l], seq)
        z_b = matmul(a, w, BF16, n=GDN_WIDTH, col_off=o1)
        w_ba = jnp.zeros((d, LANES), BF16).at[:, :o3 - o2].set(w[:, o2:o3])
        beta, gc = gdn_gates(matmul(a, w_ba, F32), gdn_a_log[l], gdn_dt_bias[l])
        o_f, o_r = gdn_scan(qkv_b, beta, gc, batch, seq)
        o_b = gdn_post(o_f, o_r, z_b, gdn_norm_g[l])
        kv = matmul(rmsnorm_rows(mem.reshape(batch * mem_tokens, d), g_mem[l], BF16), w_mem_kv[l].astype(BF16), BF16)
        o_m = mem_attention(matmul(a, w[:, o3:o4], BF16), kv, batch, seq, mem_tokens)
        gates = matmul(a, w[:, o4:], BF16, bias=b_gate[l].reshape(-1))
        mix = gated_merge(o_a, o_b, o_m, w_o_swa[l].astype(BF16), w_o_gdn[l].astype(BF16),
                          w_o_mem[l].astype(BF16), gates)
        x2 = matmul(mix, w_out[l].astype(BF16), F32, residual=h)
        h = hier_moe_final(x2, g_ffn[l], w_route_group[l], b_route_group[l], w_route_expert[l],
                           w_expert_gate[l], w_expert_up[l], w_expert_down[l], g_final)
    return h.reshape(batch, seq, d)
```

```python
import functools

import jax
import jax.numpy as jnp
from jax import lax
from jax.experimental import pallas as pl
from jax.experimental.pallas import tpu as pltpu

F32 = jnp.float32
BF16 = jnp.bfloat16

HEAD_DIM = 128
SWA_GROUPS = ((128, 1), (512, 4), (2048, 16))
SWA_HEADS_PER_GROUP = 4
SWA_HEADS = SWA_HEADS_PER_GROUP * len(SWA_GROUPS)
SWA_WIDTH = SWA_HEADS * HEAD_DIM
SWA_BLOCK = 64
SWA_UNROLL = 4
ROPE_THETA = 500000.0
ROPE_DIMS = HEAD_DIM // 4
GDN_HEADS = 12
GDN_WIDTH = GDN_HEADS * HEAD_DIM
GDN_CONV = 5
GDN_CHUNK = 64
MEM_HEADS = 4
MEM_HEAD_DIM = 256
MEM_WIDTH = MEM_HEADS * MEM_HEAD_DIM
N_BRANCH = 3
N_GROUPS = 4
EXPERTS_PER_GROUP = 8
N_EXPERTS = N_GROUPS * EXPERTS_PER_GROUP
EXPERT_TOPK = 2
MOE_BLOCK = 512
EPS = 1e-6
NEG_INF = -1e30

LANES = 128
SUBLANES = 8
VMEM_LIMIT = 48 * 1024 * 1024


def _cp(*sem, vmem=VMEM_LIMIT):
    return pltpu.CompilerParams(dimension_semantics=sem, vmem_limit_bytes=vmem)


def _tile(n, pref):
    t = min(n, pref)
    while n % t:
        t //= 2
    return t


def _rmsnorm_kernel(x_ref, g_ref, o_ref):
    x = x_ref[...].astype(F32)
    r = lax.rsqrt(jnp.mean(x * x, axis=-1, keepdims=True) + EPS)
    o_ref[...] = (x * r * g_ref[...]).astype(o_ref.dtype)


def rmsnorm_rows(x, g, out_dtype):
    m, d = x.shape
    tm = _tile(m, 512)
    return pl.pallas_call(
        _rmsnorm_kernel,
        out_shape=jax.ShapeDtypeStruct((m, d), out_dtype),
        grid=(m // tm,),
        in_specs=[pl.BlockSpec((tm, d), lambda i: (i, 0)), pl.BlockSpec((1, d), lambda i: (0, 0))],
        out_specs=pl.BlockSpec((tm, d), lambda i: (i, 0)),
        compiler_params=_cp("parallel"),
        name="rmsnorm_rows",
    )(x, g.reshape(1, d))


def _mm_kernel(a_ref, w_ref, o_ref):
    o_ref[...] = jnp.dot(a_ref[...], w_ref[...], preferred_element_type=F32).astype(o_ref.dtype)


def _mm_sigmoid_kernel(a_ref, w_ref, b_ref, o_ref):
    z = jnp.dot(a_ref[...], w_ref[...], preferred_element_type=F32) + b_ref[...]
    o_ref[...] = jax.nn.sigmoid(z).astype(o_ref.dtype)


def _mm_residual_kernel(a_ref, w_ref, r_ref, o_ref):
    o_ref[...] = r_ref[...] + jnp.dot(a_ref[...], w_ref[...], preferred_element_type=F32)


EPI_COLS = 2 * HEAD_DIM


def _mm_rope_kernel(a_ref, w_ref, c_ref, s1_ref, s2_ref, o_ref):
    c, s1, s2 = c_ref[...], s1_ref[...], s2_ref[...]
    half = ROPE_DIMS // 2
    for c0 in range(0, o_ref.shape[1], EPI_COLS):
        acc = jnp.dot(a_ref[...], w_ref[:, c0:c0 + EPI_COLS], preferred_element_type=F32)
        for h0 in range(0, EPI_COLS, HEAD_DIM):
            xh = acc[:, h0:h0 + HEAD_DIM]
            o_ref[:, c0 + h0:c0 + h0 + HEAD_DIM] = (
                xh * c + pltpu.roll(xh, HEAD_DIM - half, 1) * s1 + pltpu.roll(xh, half, 1) * s2)


MM_TM = 1024
MM_TN = (1536, 1024, 512)


def _col_tile(n):
    return next((t for t in MM_TN if n % t == 0), n)


def matmul(a, w, out_dtype, *, n=None, col_off=0, bias=None, residual=None, rope=None):
    m, k = a.shape
    n = w.shape[1] if n is None else n
    tm, tn = _tile(m, MM_TM), _col_tile(n)
    assert col_off % tn == 0
    joff = col_off // tn
    a_spec = pl.BlockSpec((tm, k), lambda i, j: (i, 0))
    w_spec = pl.BlockSpec((k, tn), lambda i, j: (0, joff + j))
    o_spec = pl.BlockSpec((tm, tn), lambda i, j: (i, j))
    if bias is not None:
        kern, extra, extra_specs = _mm_sigmoid_kernel, (bias.reshape(1, n),), [pl.BlockSpec((1, tn), lambda i, j: (0, j))]
    elif residual is not None:
        kern, extra, extra_specs = _mm_residual_kernel, (residual,), [o_spec]
    elif rope is not None:
        tables, seq = rope
        assert tn == SWA_WIDTH and n == 3 * SWA_WIDTH
        per = seq // tm
        t_spec = pl.BlockSpec((None, tm, HEAD_DIM), lambda i, j: (j, i % per, 0))
        kern = _mm_rope_kernel
        extra, extra_specs = tuple(tables), [t_spec] * 3
    else:
        kern, extra, extra_specs = _mm_kernel, (), []
    return pl.pallas_call(
        kern,
        out_shape=jax.ShapeDtypeStruct((m, n), out_dtype),
        grid=(m // tm, n // tn),
        in_specs=[a_spec, w_spec] + extra_specs,
        out_specs=o_spec,
        compiler_params=_cp("parallel", "arbitrary"),
        name="matmul",
    )(a, w, *extra)


def rope_tables(seq):
    half = ROPE_DIMS // 2
    inv = ROPE_THETA ** (-jnp.arange(half, dtype=F32) / half)
    ang = jnp.arange(seq, dtype=F32)[:, None] * inv[None, :]
    cos, sin = jnp.cos(ang), jnp.sin(ang)
    zeros = jnp.zeros((seq, HEAD_DIM - ROPE_DIMS), F32)
    zh = jnp.zeros((seq, half), F32)
    c = jnp.concatenate([cos, cos, zeros + 1.0], axis=1)
    s1 = jnp.concatenate([-sin, zh, zeros], axis=1)
    s2 = jnp.concatenate([zh, sin, zeros], axis=1)
    ident = (jnp.ones_like(c), jnp.zeros_like(c), jnp.zeros_like(c))
    return tuple(jnp.stack([tab * (HEAD_DIM ** -0.5), tab, idt]) for tab, idt in zip((c, s1, s2), ident))


CONV_HALO = 16


def _mm_conv_kernel(a_ref, ap_ref, an_ref, w_ref, cw_ref, o_ref, buf_ref, *, tm, tiles_per_seq):
    i, kind = pl.program_id(0), pl.program_id(1)
    halo = CONV_HALO
    rows = tm + 2 * halo
    keep_prev = jnp.where((i % tiles_per_seq) == 0, 0.0, 1.0)
    keep_next = jnp.where((i % tiles_per_seq) == tiles_per_seq - 1, 0.0, 1.0)
    qk_mul = jnp.where(kind == 0, HEAD_DIM ** -0.5, 1.0)
    w = w_ref[...]
    buf_ref[0:halo, :] = jnp.dot(ap_ref[...], w, preferred_element_type=F32) * keep_prev
    buf_ref[halo:halo + tm, :] = jnp.dot(a_ref[...], w, preferred_element_type=F32)
    buf_ref[halo + tm:, :] = jnp.dot(an_ref[...], w, preferred_element_type=F32) * keep_next
    for h in range(GDN_HEADS):
        cols = slice(h * HEAD_DIM, (h + 1) * HEAD_DIM)
        xh = buf_ref[:, cols]
        acc = xh * cw_ref[GDN_CONV // 2:GDN_CONV // 2 + 1, cols]
        for t in range(GDN_CONV):
            if t != GDN_CONV // 2:
                acc = acc + pltpu.roll(xh, (GDN_CONV // 2 - t) % rows, 0) * cw_ref[t:t + 1, cols]
        acc = acc[halo:halo + tm, :]
        y = acc * jax.nn.sigmoid(acc)
        inv = lax.rsqrt(jnp.sum(y * y, axis=-1, keepdims=True) + EPS) * qk_mul
        o_ref[h] = (y * jnp.where(kind < 2, inv, 1.0)).astype(o_ref.dtype)


def matmul_conv_norm(a, w, col_off, conv_w, seq):
    m, k = a.shape
    tn = GDN_WIDTH
    tm = _tile(seq, MM_TM)
    assert col_off % tn == 0 and tm % CONV_HALO == 0
    joff = col_off // tn
    per_h = tm // CONV_HALO
    nblk_h = m // CONV_HALO
    return pl.pallas_call(
        functools.partial(_mm_conv_kernel, tm=tm, tiles_per_seq=seq // tm),
        out_shape=jax.ShapeDtypeStruct((3 * GDN_HEADS, m, HEAD_DIM), BF16),
        grid=(m // tm, 3),
        in_specs=[
            pl.BlockSpec((tm, k), lambda i, j: (i, 0)),
            pl.BlockSpec((CONV_HALO, k), lambda i, j: (jnp.maximum(i * per_h - 1, 0), 0)),
            pl.BlockSpec((CONV_HALO, k), lambda i, j: (jnp.minimum((i + 1) * per_h, nblk_h - 1), 0)),
            pl.BlockSpec((k, tn), lambda i, j: (0, joff + j)),
            pl.BlockSpec((GDN_CONV, tn), lambda i, j: (0, j)),
        ],
        out_specs=pl.BlockSpec((GDN_HEADS, tm, HEAD_DIM), lambda i, j: (j, i, 0)),
        scratch_shapes=[pltpu.VMEM((tm + 2 * CONV_HALO, tn), F32)],
        compiler_params=_cp("parallel", "arbitrary"),
        name="matmul_conv_norm",
    )(a, a, a, w, conv_w)


def _swa_kernel(q_ref, k_ref, v_ref, o_ref, acc_ref, m_ref, l_ref, *, seq):
    g = pl.program_id(2)

    @pl.when(g == 0)
    def _():
        acc_ref[...] = jnp.zeros_like(acc_ref)
        m_ref[...] = jnp.full_like(m_ref, NEG_INF)
        l_ref[...] = jnp.zeros_like(l_ref)

    for gi, (window, dil) in enumerate(SWA_GROUPS):
        radius = window // (2 * dil)
        assert radius <= SWA_BLOCK
        sub = seq // dil
        qb = min(128, sub)
        win = min(sub, qb + 2 * SWA_BLOCK)
        nqb = sub // qb

        @pl.when(g == gi)
        def _(dil=dil, radius=radius, sub=sub, qb=qb, win=win, nqb=nqb):
            def rows(start, size):
                return pl.ds(start, size) if dil == 1 else pl.ds(start, size, stride=dil)

            total = dil * nqb
            unroll = SWA_UNROLL if total % SWA_UNROLL == 0 else 1
            idx = range(unroll)

            def body(step, carry):
                its = [step * unroll + u for u in idx]
                qss = [(it % nqb) * qb for it in its]
                wss = [jnp.clip(qs - SWA_BLOCK, 0, sub - win) for qs in qss]
                q_rows = [rows(it // nqb + dil * qs, qb) for it, qs in zip(its, qss)]
                k_rows = [rows(it // nqb + dil * ws, win) for it, ws in zip(its, wss)]
                q = [q_ref[r, :].astype(BF16) for r in q_rows]
                k = [k_ref[r, :].astype(BF16) for r in k_rows]
                v = [v_ref[r, :].astype(BF16) for r in k_rows]
                s = [_nt(q[u], k[u]) for u in idx]
                off = lax.broadcasted_iota(jnp.int32, (qb, win), 0) - lax.broadcasted_iota(jnp.int32, (qb, win), 1)
                s = [jnp.where(jnp.abs(off + (qss[u] - wss[u])) <= radius, s[u], NEG_INF) for u in idx]
                m_old = [m_ref[r, :] for r in q_rows]
                m_new = [jnp.maximum(m_old[u], jnp.max(s[u], axis=1, keepdims=True)) for u in idx]
                p = [jnp.exp(s[u] - m_new[u]) for u in idx]
                corr = [jnp.exp(m_old[u] - m_new[u]) for u in idx]
                pv = [jnp.dot(p[u].astype(BF16), v[u], preferred_element_type=F32) for u in idx]
                for u in idx:
                    l_ref[q_rows[u], :] = l_ref[q_rows[u], :] * corr[u] + jnp.sum(p[u], axis=1, keepdims=True)
                    acc_ref[q_rows[u], :] = acc_ref[q_rows[u], :] * corr[u] + pv[u]
                    m_ref[q_rows[u], :] = m_new[u]
                return carry

            lax.fori_loop(0, total // unroll, body, 0)

    @pl.when(g == len(SWA_GROUPS) - 1)
    def _():
        o_ref[...] = (acc_ref[...] / l_ref[...]).astype(o_ref.dtype)


def dilated_swa(qkv, batch, seq):
    nh = SWA_HEADS_PER_GROUP
    qkv3 = qkv.reshape(batch, seq, 3 * SWA_WIDTH)

    def spec(off):
        return pl.BlockSpec((None, seq, HEAD_DIM), lambda b, h, g: (b, 0, off + g * nh + h))

    out = pl.pallas_call(
        functools.partial(_swa_kernel, seq=seq),
        out_shape=jax.ShapeDtypeStruct((batch, seq, nh * HEAD_DIM), BF16),
        grid=(batch, nh, len(SWA_GROUPS)),
        in_specs=[spec(0), spec(SWA_HEADS), spec(2 * SWA_HEADS)],
        out_specs=pl.BlockSpec((None, seq, HEAD_DIM), lambda b, h, g: (b, 0, h)),
        scratch_shapes=[pltpu.VMEM((seq, HEAD_DIM), F32), pltpu.VMEM((seq, 1), F32), pltpu.VMEM((seq, 1), F32)],
        compiler_params=_cp("parallel", "parallel", "arbitrary"),
        name="dilated_swa",
    )(qkv3, qkv3, qkv3)
    return out.reshape(batch * seq, nh * HEAD_DIM)


def _gdn_gate_kernel(ba_ref, alog_ref, dtb_ref, beta_ref, gc_ref, *, ts):
    x = ba_ref[...]
    nhd = 2 * GDN_HEADS
    lane = lax.broadcasted_iota(jnp.int32, (ts, LANES), 1)
    row = lax.broadcasted_iota(jnp.int32, (ts, LANES), 0) % GDN_CHUNK
    beta_ref[...] = jax.nn.sigmoid(x)
    z = x + dtb_ref[...]
    softplus = jnp.maximum(z, 0.0) + jnp.log(1.0 + jnp.exp(-jnp.abs(z)))
    g = jnp.where(jnp.logical_and(lane >= nhd, lane < 2 * nhd), -jnp.exp(alog_ref[...]) * softplus, 0.0)
    pre, suf = g, g
    s = 1
    while s < GDN_CHUNK:
        pre = pre + jnp.where(row >= s, pltpu.roll(pre, s, 0), 0.0)
        suf = suf + jnp.where(row < GDN_CHUNK - s, pltpu.roll(suf, ts - s, 0), 0.0)
        s *= 2
    gc_ref[...] = jnp.where(lane < nhd + GDN_HEADS, pre, suf)


def gdn_gates(ba, a_log, dt_bias):
    t = ba.shape[0]
    ts = _tile(t, 1024)
    nhd = 2 * GDN_HEADS
    pad = lambda v: jnp.zeros((1, LANES), F32).at[0, nhd:2 * nhd].set(v.reshape(-1).astype(F32))
    shp = jax.ShapeDtypeStruct((t, LANES), F32)
    vec = pl.BlockSpec((1, LANES), lambda i: (0, 0))
    blk = pl.BlockSpec((ts, LANES), lambda i: (i, 0))
    return pl.pallas_call(
        functools.partial(_gdn_gate_kernel, ts=ts),
        out_shape=(shp, shp),
        grid=(t // ts,),
        in_specs=[blk, vec, vec],
        out_specs=(blk, blk),
        compiler_params=_cp("parallel"),
        name="gdn_gates",
    )(ba, pad(a_log), pad(dt_bias))


def _nt(a, b):
    return lax.dot_general(a, b, (((1,), (1,)), ((), ())), preferred_element_type=F32)


def _tn(a, b):
    return lax.dot_general(a, b, (((0,), (0,)), ((), ())), preferred_element_type=F32)


def _mmb(a, b):
    return jnp.dot(a.astype(BF16), b.astype(BF16), preferred_element_type=F32)


def _gdn_chunks(qs, ks, vs, bcols, gcols, states, revs):
    c = GDN_CHUNK
    idx = range(len(qs))
    ri = lax.broadcasted_iota(jnp.int32, (c, c), 0)
    ci = lax.broadcasted_iota(jnp.int32, (c, c), 1)
    strict = {False: ri > ci, True: ri < ci}
    incl = {False: ri >= ci, True: ri <= ci}
    eye = jnp.where(ri == ci, 1.0, 0.0)
    lane = lax.broadcasted_iota(jnp.int32, (c, LANES), 1)
    ones3 = jnp.where(lane < 3, 1.0, 0.0)
    ones3_hi = jnp.where(jnp.logical_and(lane >= 3, lane < 6), 1.0, 0.0)
    g_last = [gcols[i][0:1, :] if revs[i] else gcols[i][c - 1:c, :] for i in idx]
    egc = [jnp.exp(gcols[i]) for i in idx]
    q_state = [_mmb(qs[i] * egc[i], states[i]) for i in idx]
    diff = []
    for i in idx:
        g1 = gcols[i].astype(BF16).astype(F32)
        g2 = (gcols[i] - g1).astype(BF16).astype(F32)
        g3 = (gcols[i] - g1 - g2).astype(BF16).astype(F32)
        pieces = jnp.where(lane == 0, g1, jnp.where(lane == 1, g2, jnp.where(lane == 2, g3, 0.0)))
        xm = pieces + ones3_hi
        ym = ones3 - pltpu.roll(pieces, 3, 1)
        diff.append(_nt(xm.astype(BF16), ym.astype(BF16)))
    decay = [jnp.where(incl[revs[i]], jnp.exp(jnp.where(incl[revs[i]], diff[i], 0.0)), 0.0) for i in idx]
    kb = [ks[i] * bcols[i] for i in idx]
    k16 = [ks[i].astype(BF16) for i in idx]
    lm = [jnp.where(strict[revs[i]], _nt(kb[i].astype(BF16), k16[i]) * decay[i], 0.0) for i in idx]
    attn = [_nt(qs[i].astype(BF16), k16[i]) * decay[i] for i in idx]
    tinv = [eye - lm[i] for i in idx]
    pw = [_mmb(lm[i], lm[i]) for i in idx]
    n = 4
    while n < c:
        both = [_mmb(jnp.concatenate([tinv[i], pw[i]], axis=0), pw[i]) for i in idx]
        tinv = [tinv[i] + both[i][:c] for i in idx]
        pw = [both[i][c:] for i in idx]
        n *= 2
    tinv = [tinv[i] + _mmb(tinv[i], pw[i]) for i in idx]
    u = [_mmb(tinv[i], jnp.concatenate([vs[i] * bcols[i], kb[i] * egc[i]], axis=1)) for i in idx]
    v_new = [u[i][:, :HEAD_DIM] - _mmb(u[i][:, HEAD_DIM:], states[i]) for i in idx]
    outs = [q_state[i] + _mmb(attn[i], v_new[i]) for i in idx]
    k_e = [ks[i] * jnp.exp(g_last[i] - gcols[i]) for i in idx]
    new_states = [states[i] * jnp.exp(g_last[i]) + _tn(k_e[i].astype(BF16), v_new[i].astype(BF16)) for i in idx]
    return outs, new_states


def _gdn_kernel(qf_ref, kf_ref, vf_ref, bf_ref, gf_ref, qb_ref, kb_ref, vb_ref, bb_ref, gb_ref,
                of_ref, ob_ref, state_ref, *, chunks):
    @pl.when(pl.program_id(1) == 0)
    def _():
        state_ref[...] = jnp.zeros_like(state_ref)

    nh = GDN_HEADS

    def body(it, carry):
        rows_f = pl.ds(pl.multiple_of(it * GDN_CHUNK, GDN_CHUNK), GDN_CHUNK)
        rows_b = pl.ds(pl.multiple_of((chunks - 1 - it) * GDN_CHUNK, GDN_CHUNK), GDN_CHUNK)
        tabs = ((qf_ref, kf_ref, vf_ref, bf_ref[rows_f, :], gf_ref[rows_f, :], rows_f, 0),
                (qb_ref, kb_ref, vb_ref, bb_ref[rows_b, :], gb_ref[rows_b, :], rows_b, nh))
        qs, ks, vs, bcols, gcols, states, revs = [], [], [], [], [], [], []
        for d, (q_ref, k_ref, v_ref, bt, gt, rows, lane0) in enumerate(tabs):
            for h in range(nh):
                qs.append(q_ref[h, rows, :].astype(F32))
                ks.append(k_ref[h, rows, :].astype(F32))
                vs.append(v_ref[h, rows, :].astype(F32))
                bcols.append(bt[:, lane0 + h:lane0 + h + 1])
                gcols.append(gt[:, 2 * nh + lane0 + h:2 * nh + lane0 + h + 1])
                states.append(state_ref[d * nh + h])
                revs.append(d == 1)
        outs, new_states = _gdn_chunks(qs, ks, vs, bcols, gcols, states, revs)
        for d, o_ref in enumerate((of_ref, ob_ref)):
            for h in range(nh):
                state_ref[d * nh + h] = new_states[d * nh + h]
                o_ref[h, tabs[d][5], :] = outs[d * nh + h].astype(o_ref.dtype)
        return carry

    lax.fori_loop(0, chunks, body, 0)


def gdn_scan(qkv_h, beta, gc, batch, seq):
    t = batch * seq
    rows = _tile(seq, 512)
    nblk = seq // rows
    fwd = lambda b, ib: b * nblk + ib
    bwd = lambda b, ib: b * nblk + nblk - 1 - ib

    def specs(blk):
        qspec = lambda part: pl.BlockSpec((GDN_HEADS, rows, HEAD_DIM), lambda b, ib: (part, blk(b, ib), 0))
        gspec = pl.BlockSpec((rows, LANES), lambda b, ib: (blk(b, ib), 0))
        return [qspec(0), qspec(1), qspec(2), gspec, gspec]

    ospec = lambda blk: pl.BlockSpec((GDN_HEADS, rows, HEAD_DIM), lambda b, ib: (0, blk(b, ib), 0))
    out = jax.ShapeDtypeStruct((GDN_HEADS, t, HEAD_DIM), BF16)
    return pl.pallas_call(
        functools.partial(_gdn_kernel, chunks=rows // GDN_CHUNK),
        out_shape=(out, out),
        grid=(batch, nblk),
        in_specs=specs(fwd) + specs(bwd),
        out_specs=(ospec(fwd), ospec(bwd)),
        scratch_shapes=[pltpu.VMEM((2 * GDN_HEADS, HEAD_DIM, HEAD_DIM), F32)],
        compiler_params=_cp("parallel", "arbitrary"),
        name="gdn_scan",
    )(qkv_h, qkv_h, qkv_h, beta, gc, qkv_h, qkv_h, qkv_h, beta, gc)


def _gdn_post_kernel(of_ref, ob_ref, z_ref, g_ref, o_ref):
    for h in range(of_ref.shape[0]):
        cols = slice(h * HEAD_DIM, (h + 1) * HEAD_DIM)
        o = of_ref[h].astype(F32) + ob_ref[h].astype(F32)
        r = lax.rsqrt(jnp.mean(o * o, axis=-1, keepdims=True) + EPS)
        z = z_ref[:, cols].astype(F32)
        o_ref[:, cols] = (o * r * g_ref[...] * (z * jax.nn.sigmoid(z))).astype(o_ref.dtype)


def gdn_post(o_f, o_b, z, norm_g):
    nhd, t, _ = o_f.shape
    ts = _tile(t, 512)
    hspec = pl.BlockSpec((nhd, ts, HEAD_DIM), lambda i: (0, i, 0))
    tspec = pl.BlockSpec((ts, nhd * HEAD_DIM), lambda i: (i, 0))
    return pl.pallas_call(
        _gdn_post_kernel,
        out_shape=jax.ShapeDtypeStruct((t, nhd * HEAD_DIM), BF16),
        grid=(t // ts,),
        in_specs=[hspec, hspec, tspec, pl.BlockSpec((1, HEAD_DIM), lambda i: (0, 0))],
        out_specs=tspec,
        compiler_params=_cp("parallel"),
        name="gdn_post",
    )(o_f, o_b, z, norm_g.reshape(1, HEAD_DIM).astype(F32))


def _mem_attn_kernel(q_ref, k_ref, v_ref, o_ref):
    scale = MEM_HEAD_DIM ** -0.5
    for h in range(MEM_HEADS):
        cols = slice(h * MEM_HEAD_DIM, (h + 1) * MEM_HEAD_DIM)
        s = _nt(q_ref[:, cols], k_ref[:, cols]) * scale
        p = jnp.exp(s - jnp.max(s, axis=-1, keepdims=True))
        p = p / jnp.sum(p, axis=-1, keepdims=True)
        o_ref[:, cols] = jnp.dot(p.astype(BF16), v_ref[:, cols], preferred_element_type=F32).astype(o_ref.dtype)


def mem_attention(mq, kv, batch, seq, mem_tokens):
    tq = _tile(seq, 512)
    per = seq // tq
    return pl.pallas_call(
        _mem_attn_kernel,
        out_shape=jax.ShapeDtypeStruct(mq.shape, BF16),
        grid=(batch * per,),
        in_specs=[
            pl.BlockSpec((tq, MEM_WIDTH), lambda i: (i, 0)),
            pl.BlockSpec((mem_tokens, MEM_WIDTH), lambda i: (i // per, 0)),
            pl.BlockSpec((mem_tokens, MEM_WIDTH), lambda i: (i // per, 1)),
        ],
        out_specs=pl.BlockSpec((tq, MEM_WIDTH), lambda i: (i, 0)),
        compiler_params=_cp("parallel"),
        name="mem_attention",
    )(mq, kv, kv)


def _merge_kernel(oa_ref, ob_ref, om_ref, wa_ref, wb_ref, wm_ref, g0_ref, g1_ref, g2_ref, o_ref):
    ya = jnp.dot(oa_ref[...], wa_ref[...], preferred_element_type=F32)
    yb = jnp.dot(ob_ref[...], wb_ref[...], preferred_element_type=F32)
    ym = jnp.dot(om_ref[...], wm_ref[...], preferred_element_type=F32)
    mix = g0_ref[...].astype(F32) * ya + g1_ref[...].astype(F32) * yb + g2_ref[...].astype(F32) * ym
    o_ref[...] = mix.astype(o_ref.dtype)


def gated_merge(o_a, o_b, o_m, w_a, w_b, w_m, gates):
    t = o_a.shape[0]
    d = w_a.shape[1]
    tm, tn = _tile(t, 1024), _tile(d, 512)
    nj = d // tn
    act = lambda o: pl.BlockSpec((tm, o.shape[1]), lambda i, j: (i, 0))
    wsp = lambda w: pl.BlockSpec((w.shape[0], tn), lambda i, j: (0, j))
    gsp = lambda br: pl.BlockSpec((tm, tn), lambda i, j: (i, br * nj + j))
    return pl.pallas_call(
        _merge_kernel,
        out_shape=jax.ShapeDtypeStruct((t, d), BF16),
        grid=(t // tm, nj),
        in_specs=[act(o_a), act(o_b), act(o_m), wsp(w_a), wsp(w_b), wsp(w_m), gsp(0), gsp(1), gsp(2)],
        out_specs=pl.BlockSpec((tm, tn), lambda i, j: (i, j)),
        compiler_params=_cp("parallel", "arbitrary"),
        name="gated_merge",
    )(o_a, o_b, o_m, w_a, w_b, w_m, gates, gates, gates)


def _pack_bf16_pairs(x):
    w = x.shape[1] // 2
    lo = lax.bitcast_convert_type(x[:, :w].astype(BF16).astype(F32), jnp.uint32)
    hi = lax.bitcast_convert_type(x[:, w:].astype(BF16).astype(F32), jnp.uint32)
    return (hi & jnp.uint32(0xFFFF0000)) | (lo >> 16)


def _unpack_bf16_pairs(p):
    lo = lax.bitcast_convert_type(p << 16, F32)
    hi = lax.bitcast_convert_type(p & jnp.uint32(0xFFFF0000), F32)
    return lo, hi


MOE_ROWS_PER_STEP = 128
ROUTE_ROWS = SUBLANES + N_EXPERTS


def _route_kernel(x_ref, g_ref, w_ref, b_ref, xn_ref, eid_ref, wt_ref):
    x = x_ref[...]
    r = lax.rsqrt(jnp.mean(x * x, axis=-1, keepdims=True) + EPS)
    xn = x * r * g_ref[...]
    xn_ref[...] = _pack_bf16_pairs(xn)
    tm = x.shape[0]
    x_hi = xn.astype(BF16)
    x_lo = (xn - x_hi.astype(F32)).astype(BF16)
    w_hi, w_lo = w_ref[0], w_ref[1]
    lg = (jnp.dot(x_hi, w_hi, preferred_element_type=F32) + jnp.dot(x_lo, w_hi, preferred_element_type=F32)
          + jnp.dot(x_hi, w_lo, preferred_element_type=F32))
    lg = lg.T[:ROUTE_ROWS, :] + b_ref[...]
    gl = [lg[i:i + 1, :] for i in range(N_GROUPS)]
    gmax = functools.reduce(jnp.maximum, gl)
    grp = jnp.full((1, tm), N_GROUPS - 1, jnp.int32)
    for i in range(N_GROUPS - 2, -1, -1):
        grp = jnp.where(gl[i] == gmax, i, grp)
    p_grp = 1.0 / functools.reduce(jnp.add, [jnp.exp(v - gmax) for v in gl])
    sel = lg[SUBLANES + (N_GROUPS - 1) * EXPERTS_PER_GROUP:SUBLANES + N_GROUPS * EXPERTS_PER_GROUP, :]
    for i in range(N_GROUPS - 2, -1, -1):
        lo = SUBLANES + i * EXPERTS_PER_GROUP
        sel = jnp.where(grp == i, lg[lo:lo + EXPERTS_PER_GROUP, :], sel)
    rowi = lax.broadcasted_iota(jnp.int32, (EXPERTS_PER_GROUP, tm), 0)
    v1 = jnp.max(sel, axis=0, keepdims=True)
    i1 = jnp.min(jnp.where(sel == v1, rowi, EXPERTS_PER_GROUP), axis=0, keepdims=True)
    sel2 = jnp.where(rowi == i1, -jnp.inf, sel)
    v2 = jnp.max(sel2, axis=0, keepdims=True)
    i2 = jnp.min(jnp.where(sel2 == v2, rowi, EXPERTS_PER_GROUP), axis=0, keepdims=True)
    e2 = jnp.exp(v2 - v1)
    w1 = p_grp / (1.0 + e2)
    eid_ref[0:1, :] = grp * EXPERTS_PER_GROUP + i1
    eid_ref[1:2, :] = grp * EXPERTS_PER_GROUP + i2
    wt_ref[0:1, :] = w1
    wt_ref[1:2, :] = w1 * e2


def moe_route(x, g, w_grp, b_grp, w_exp):
    t, d = x.shape
    tm = _tile(t, 512)
    w = jnp.zeros((d, LANES), F32).at[:, :N_GROUPS].set(w_grp).at[:, SUBLANES:ROUTE_ROWS].set(w_exp)
    w_hi = w.astype(BF16)
    w = jnp.stack([w_hi, (w - w_hi.astype(F32)).astype(BF16)])
    b = jnp.zeros((ROUTE_ROWS, 1), F32).at[:N_GROUPS, 0].set(b_grp.astype(F32))
    return pl.pallas_call(
        _route_kernel,
        out_shape=(jax.ShapeDtypeStruct((t, d // 2), jnp.uint32), jax.ShapeDtypeStruct((EXPERT_TOPK, t), jnp.int32),
                   jax.ShapeDtypeStruct((EXPERT_TOPK, t), F32)),
        grid=(t // tm,),
        in_specs=[pl.BlockSpec((tm, d), lambda i: (i, 0)), pl.BlockSpec((1, d), lambda i: (0, 0)),
                  pl.BlockSpec((2, d, LANES), lambda i: (0, 0, 0)), pl.BlockSpec((ROUTE_ROWS, 1), lambda i: (0, 0))],
        out_specs=(pl.BlockSpec((tm, d // 2), lambda i: (i, 0)), pl.BlockSpec((EXPERT_TOPK, tm), lambda i: (0, i)),
                   pl.BlockSpec((EXPERT_TOPK, tm), lambda i: (0, i))),
        compiler_params=_cp("parallel"),
        name="moe_route",
    )(x, g.reshape(1, d), w, b)


def _rank_kernel(e_ref, rank_ref, cnt_ref, *, tb):
    @pl.when(pl.program_id(0) == 0)
    def _():
        cnt_ref[...] = jnp.zeros_like(cnt_ref)

    e = e_ref[...]
    onehot = jnp.where(lax.broadcasted_iota(jnp.int32, (N_EXPERTS, tb), 0) == e, 1.0, 0.0)
    earlier = jnp.where(lax.broadcasted_iota(jnp.int32, (tb, tb), 0) < lax.broadcasted_iota(jnp.int32, (tb, tb), 1),
                        1.0, 0.0)
    before = _mmb(onehot, earlier) + cnt_ref[...]
    rank_ref[...] = jnp.sum(onehot * before, axis=0, keepdims=True).astype(jnp.int32)
    cnt_ref[...] = cnt_ref[...] + jnp.sum(onehot, axis=1, keepdims=True)


def moe_rank(e_flat):
    n = e_flat.shape[1]
    tb = _tile(n, 512)
    return pl.pallas_call(
        functools.partial(_rank_kernel, tb=tb),
        out_shape=(jax.ShapeDtypeStruct((1, n), jnp.int32), jax.ShapeDtypeStruct((N_EXPERTS, 1), F32)),
        grid=(n // tb,),
        in_specs=[pl.BlockSpec((1, tb), lambda i: (0, i))],
        out_specs=(pl.BlockSpec((1, tb), lambda i: (0, i)), pl.BlockSpec((N_EXPERTS, 1), lambda i: (0, 0))),
        compiler_params=_cp("arbitrary"),
        name="moe_rank",
    )(e_flat)


def _dispatch_kernel(dest_ref, cnt_ref, pstart_ref, xn_ref, xs_ref, xbuf_ref, zero_ref, load_sem, scat_sem, fill_sem,
                     *, tb, tokens, nblk):
    i, nsteps = pl.program_id(0), pl.num_programs(0)
    slot = i % 2

    def load(step, sl):
        return pltpu.make_async_copy(xn_ref.at[pl.ds(step * tb, tb)], xbuf_ref.at[sl], load_sem.at[sl])

    def scatters(step, sl):
        return [pltpu.make_async_copy(xbuf_ref.at[sl, pl.ds(j, 1)],
                                      xs_ref.at[pl.ds(dest_ref[kk * tokens + step * tb + j], 1)], scat_sem.at[sl])
                for j in range(tb) for kk in range(EXPERT_TOPK)]

    def zero_rows(dst_row, nrows):
        return pltpu.make_async_copy(zero_ref.at[pl.ds(0, nrows)], xs_ref.at[pl.ds(dst_row, nrows)], fill_sem)

    @pl.when(i == 0)
    def _():
        load(0, 0).start()
        zero_ref[...] = jnp.zeros_like(zero_ref)
        last = N_EXPERTS - 1
        n_used = (pstart_ref[last] + cnt_ref[last] + MOE_BLOCK - 1) // MOE_BLOCK

        def fill_block(blk, carry):
            zero_rows(blk * MOE_BLOCK, MOE_BLOCK).start()
            zero_rows(blk * MOE_BLOCK, MOE_BLOCK).wait()
            return carry

        lax.fori_loop(n_used, nblk, fill_block, 0)

        def per_expert(e, carry):
            @pl.when(cnt_ref[e] % MOE_BLOCK != 0)
            def _():
                blk = (pstart_ref[e] + cnt_ref[e]) // MOE_BLOCK
                zero_rows(blk * MOE_BLOCK, MOE_BLOCK).start()
                zero_rows(blk * MOE_BLOCK, MOE_BLOCK).wait()

            return carry

        lax.fori_loop(0, N_EXPERTS, per_expert, 0)

    @pl.when(i >= 1)
    def _():
        for cp in scatters(i - 1, 1 - slot):
            cp.wait()

    @pl.when(i + 1 < nsteps)
    def _():
        load(i + 1, 1 - slot).start()

    load(i, slot).wait()
    for cp in scatters(i, slot):
        cp.start()

    @pl.when(i == nsteps - 1)
    def _():
        for cp in scatters(i, slot):
            cp.wait()


def moe_dispatch(xn, dest_flat, counts, pad_start, n_slots):
    t, d = xn.shape
    tb = _tile(t, MOE_ROWS_PER_STEP)
    return pl.pallas_call(
        functools.partial(_dispatch_kernel, tb=tb, tokens=t, nblk=n_slots // MOE_BLOCK),
        out_shape=jax.ShapeDtypeStruct((n_slots, d), xn.dtype),
        grid_spec=pltpu.PrefetchScalarGridSpec(
            num_scalar_prefetch=3,
            grid=(t // tb,),
            in_specs=[pl.BlockSpec(memory_space=pl.ANY)],
            out_specs=pl.BlockSpec(memory_space=pl.ANY),
            scratch_shapes=[pltpu.VMEM((2, tb, d), xn.dtype), pltpu.VMEM((MOE_BLOCK, d), xn.dtype),
                            pltpu.SemaphoreType.DMA((2,)), pltpu.SemaphoreType.DMA((2,)), pltpu.SemaphoreType.DMA],
        ),
        compiler_params=pltpu.CompilerParams(dimension_semantics=("arbitrary",), has_side_effects=True,
                                             vmem_limit_bytes=VMEM_LIMIT),
        name="moe_dispatch",
    )(dest_flat, counts, pad_start, xn)


def _expert_kernel(be_ref, nused_ref, x_ref, wg_ref, wu_ref, wd_ref, y_ref, wg16_ref, wu16_ref, wd16_ref):
    i = pl.program_id(0)
    used = i < nused_ref[0]

    @pl.when(jnp.logical_and(used, jnp.logical_or(i == 0, be_ref[i] != be_ref[jnp.maximum(i - 1, 0)])))
    def _():
        wg16_ref[...] = wg_ref[...].astype(BF16)
        wu16_ref[...] = wu_ref[...].astype(BF16)
        wd16_ref[...] = wd_ref[...].astype(BF16)

    @pl.when(used)
    def _():
        x_lo, x_hi = (v.astype(BF16) for v in _unpack_bf16_pairs(x_ref[...]))
        half = x_lo.shape[1]

        def proj(w_ref):
            return (jnp.dot(x_lo, w_ref[:half, :], preferred_element_type=F32)
                    + jnp.dot(x_hi, w_ref[half:, :], preferred_element_type=F32))

        hg, hu = proj(wg16_ref), proj(wu16_ref)
        hmid = (hg * jax.nn.sigmoid(hg) * hu).astype(BF16)
        y_ref[...] = _pack_bf16_pairs(jnp.dot(hmid, wd16_ref[...], preferred_element_type=F32))

    @pl.when(pl.program_id(0) >= nused_ref[0])
    def _():
        y_ref[...] = jnp.zeros_like(y_ref)


def moe_experts(xs, blk_expert, n_used, w_gate, w_up, w_down):
    p, dp = xs.shape
    d, de = w_gate.shape[1:]
    nblk = p // MOE_BLOCK
    row = lambda i, be, nu: (jnp.minimum(i, nu[0] - 1), 0)
    wsel = lambda i, be, nu: (be[jnp.minimum(i, nu[0] - 1)], 0, 0)
    return pl.pallas_call(
        _expert_kernel,
        out_shape=jax.ShapeDtypeStruct((p, dp), jnp.uint32),
        grid_spec=pltpu.PrefetchScalarGridSpec(
            num_scalar_prefetch=2,
            grid=(nblk,),
            in_specs=[pl.BlockSpec((MOE_BLOCK, dp), row), pl.BlockSpec((None, d, de), wsel),
                      pl.BlockSpec((None, d, de), wsel), pl.BlockSpec((None, de, d), wsel)],
            out_specs=pl.BlockSpec((MOE_BLOCK, dp), lambda i, be, nu: (i, 0)),
            scratch_shapes=[pltpu.VMEM((d, de), BF16), pltpu.VMEM((d, de), BF16), pltpu.VMEM((de, d), BF16)],
        ),
        compiler_params=_cp("arbitrary"),
        name="moe_experts",
    )(blk_expert, n_used, xs, w_gate, w_up, w_down)


def _combine_kernel(dest_ref, x_ref, wt_ref, g_ref, ys_ref, o_ref, y_ref, sems, *, tb, tokens):
    i, nsteps = pl.program_id(0), pl.num_programs(0)
    slot = i % 2

    def gathers(step, sl):
        return [pltpu.make_async_copy(ys_ref.at[pl.ds(dest_ref[kk * tokens + step * tb + j], 1)],
                                      y_ref.at[sl, kk, pl.ds(j, 1)], sems.at[sl])
                for j in range(tb) for kk in range(EXPERT_TOPK)]

    @pl.when(i == 0)
    def _():
        for cp in gathers(0, 0):
            cp.start()

    @pl.when(i + 1 < nsteps)
    def _():
        for cp in gathers(i + 1, 1 - slot):
            cp.start()

    for cp in gathers(i, slot):
        cp.wait()
    y0_lo, y0_hi = _unpack_bf16_pairs(y_ref[slot, 0])
    y1_lo, y1_hi = _unpack_bf16_pairs(y_ref[slot, 1])
    w0, w1 = wt_ref[:, 0:1], wt_ref[:, 1:2]
    h = x_ref[...] + jnp.concatenate([w0 * y0_lo + w1 * y1_lo, w0 * y0_hi + w1 * y1_hi], axis=1)
    r = lax.rsqrt(jnp.mean(h * h, axis=-1, keepdims=True) + EPS)
    o_ref[...] = h * r * g_ref[...]


def moe_combine(x, ys, dest_flat, wts_tok, g_final):
    t, d = x.shape
    tb = _tile(t, MOE_ROWS_PER_STEP)
    return pl.pallas_call(
        functools.partial(_combine_kernel, tb=tb, tokens=t),
        out_shape=jax.ShapeDtypeStruct((t, d), F32),
        grid_spec=pltpu.PrefetchScalarGridSpec(
            num_scalar_prefetch=1,
            grid=(t // tb,),
            in_specs=[pl.BlockSpec((tb, d), lambda i, dr: (i, 0)),
                      pl.BlockSpec((tb, EXPERT_TOPK), lambda i, dr: (i, 0)),
                      pl.BlockSpec((1, d), lambda i, dr: (0, 0)),
                      pl.BlockSpec(memory_space=pl.ANY)],
            out_specs=pl.BlockSpec((tb, d), lambda i, dr: (i, 0)),
            scratch_shapes=[pltpu.VMEM((2, EXPERT_TOPK, tb, d // 2), jnp.uint32), pltpu.SemaphoreType.DMA((2,))],
        ),
        compiler_params=_cp("arbitrary"),
        name="moe_combine",
    )(dest_flat, x, wts_tok, g_final.reshape(1, d), ys)


def hier_moe_final(x2, g_ffn, w_rg, b_rg, w_re, w_gate, w_up, w_down, g_final):
    t, d = x2.shape
    xn, eid, wts = moe_route(x2, g_ffn, w_rg, b_rg, w_re)
    n = EXPERT_TOPK * t
    e_flat = eid.reshape(1, n)
    rank, counts_f = moe_rank(e_flat)
    counts = counts_f.reshape(N_EXPERTS).astype(jnp.int32)
    padded = (counts + MOE_BLOCK - 1) // MOE_BLOCK * MOE_BLOCK
    pad_end = jnp.cumsum(padded)
    pad_start = pad_end - padded
    dest = (jnp.take(pad_start, e_flat[0]) + rank[0]).astype(jnp.int32)
    nblk = -(-n // MOE_BLOCK) + N_EXPERTS
    blk_start = jnp.arange(nblk, dtype=jnp.int32) * MOE_BLOCK
    blk_expert = jnp.minimum(jnp.sum(pad_end[None, :] <= blk_start[:, None], axis=1), N_EXPERTS - 1).astype(jnp.int32)
    n_used = (pad_end[-1:] // MOE_BLOCK).astype(jnp.int32)
    xs = moe_dispatch(xn, dest, counts, pad_start.astype(jnp.int32), nblk * MOE_BLOCK)
    ys = moe_experts(xs, blk_expert, n_used, w_gate, w_up, w_down)
    return moe_combine(x2, ys, dest, wts.T, g_final)


def kernel(x, mem, g_mix, w_in, b_gate, gdn_conv, gdn_a_log, gdn_dt_bias, gdn_norm_g, g_mem, w_mem_kv, w_o_swa, w_o_gdn, w_o_mem, w_out, g_ffn, w_route_group, b_route_group, w_route_expert, w_expert_gate, w_expert_up, w_expert_down, g_final):
    batch, seq, d = x.shape
    depth = w_in.shape[0]
    assert depth == 1, "the final RMSNorm is fused into the layer's MoE combine"
    t = batch * seq
    mem_tokens = mem.shape[1]
    h = x.reshape(t, d)
    o0 = 3 * SWA_WIDTH
    o1 = o0 + 3 * GDN_WIDTH
    o2 = o1 + GDN_WIDTH
    o3 = o2 + 4 * GDN_HEADS
    o4 = o3 + MEM_WIDTH
    tables = rope_tables(seq)
    for l in range(depth):
        w = w_in[l].astype(BF16)
        a = rmsnorm_rows(h, g_mix[l], BF16)
        qkv_a = matmul(a, w, F32, n=o0, rope=(tables, seq))
        o_a = dilated_swa(qkv_a, batch, seq)
        qkv_b = matmul_conv_norm(a, w, o0, gdn_conv[l], seq)
        z_b = matmul(a, w, BF16, n=GDN_WIDTH, col_off=o1)
        w_ba = jnp.zeros((d, LANES), BF16).at[:, :o3 - o2].set(w[:, o2:o3])
        beta, gc = gdn_gates(matmul(a, w_ba, F32), gdn_a_log[l], gdn_dt_bias[l])
        o_f, o_r = gdn_scan(qkv_b, beta, gc, batch, seq)
        o_b = gdn_post(o_f, o_r, z_b, gdn_norm_g[l])
        kv = matmul(rmsnorm_rows(mem.reshape(batch * mem_tokens, d), g_mem[l], BF16), w_mem_kv[l].astype(BF16), BF16)
        o_m = mem_attention(matmul(a, w[:, o3:o4], BF16), kv, batch, seq, mem_tokens)
        gates = matmul(a, w[:, o4:], BF16, bias=b_gate[l].reshape(-1))
        mix = gated_merge(o_a, o_b, o_m, w_o_swa[l].astype(BF16), w_o_gdn[l].astype(BF16),
                          w_o_mem[l].astype(BF16), gates)
        x2 = matmul(mix, w_out[l].astype(BF16), F32, residual=h)
        h = hier_moe_final(x2, g_ffn[l], w_route_group[l], b_route_group[l], w_route_expert[l],
                           w_expert_gate[l], w_expert_up[l], w_expert_down[l], g_final)
    return h.reshape(batch, seq, d)
```

```python
import functools

import jax
import jax.numpy as jnp
from jax import lax
from jax.experimental import pallas as pl
from jax.experimental.pallas import tpu as pltpu

F32 = jnp.float32
BF16 = jnp.bfloat16

HEAD_DIM = 128
SWA_GROUPS = ((128, 1), (512, 4), (2048, 16))
SWA_HEADS_PER_GROUP = 4
SWA_HEADS = SWA_HEADS_PER_GROUP * len(SWA_GROUPS)
SWA_WIDTH = SWA_HEADS * HEAD_DIM
SWA_BLOCK = 64
SWA_UNROLL = 4
ROPE_THETA = 500000.0
ROPE_DIMS = HEAD_DIM // 4
GDN_HEADS = 12
GDN_WIDTH = GDN_HEADS * HEAD_DIM
GDN_CONV = 5
GDN_CHUNK = 64
MEM_HEADS = 4
MEM_HEAD_DIM = 256
MEM_WIDTH = MEM_HEADS * MEM_HEAD_DIM
N_BRANCH = 3
N_GROUPS = 4
EXPERTS_PER_GROUP = 8
N_EXPERTS = N_GROUPS * EXPERTS_PER_GROUP
EXPERT_TOPK = 2
MOE_BLOCK = 512
EPS = 1e-6
NEG_INF = -1e30

LANES = 128
SUBLANES = 8
VMEM_LIMIT = 48 * 1024 * 1024


def _cp(*sem, vmem=VMEM_LIMIT):
    return pltpu.CompilerParams(dimension_semantics=sem, vmem_limit_bytes=vmem)


def _tile(n, pref):
    t = min(n, pref)
    while n % t:
        t //= 2
    return t


def _rmsnorm_kernel(x_ref, g_ref, o_ref):
    x = x_ref[...].astype(F32)
    r = lax.rsqrt(jnp.mean(x * x, axis=-1, keepdims=True) + EPS)
    o_ref[...] = (x * r * g_ref[...]).astype(o_ref.dtype)


def rmsnorm_rows(x, g, out_dtype):
    m, d = x.shape
    tm = _tile(m, 512)
    return pl.pallas_call(
        _rmsnorm_kernel,
        out_shape=jax.ShapeDtypeStruct((m, d), out_dtype),
        grid=(m // tm,),
        in_specs=[pl.BlockSpec((tm, d), lambda i: (i, 0)), pl.BlockSpec((1, d), lambda i: (0, 0))],
        out_specs=pl.BlockSpec((tm, d), lambda i: (i, 0)),
        compiler_params=_cp("parallel"),
        name="rmsnorm_rows",
    )(x, g.reshape(1, d))


def _mm_kernel(a_ref, w_ref, o_ref):
    o_ref[...] = jnp.dot(a_ref[...], w_ref[...], preferred_element_type=F32).astype(o_ref.dtype)


def _mm_sigmoid_kernel(a_ref, w_ref, b_ref, o_ref):
    z = jnp.dot(a_ref[...], w_ref[...], preferred_element_type=F32) + b_ref[...]
    o_ref[...] = jax.nn.sigmoid(z).astype(o_ref.dtype)


def _mm_residual_kernel(a_ref, w_ref, r_ref, o_ref):
    o_ref[...] = r_ref[...] + jnp.dot(a_ref[...], w_ref[...], preferred_element_type=F32)


EPI_COLS = 2 * HEAD_DIM


def _mm_rope_kernel(a_ref, w_ref, c_ref, s1_ref, s2_ref, o_ref):
    c, s1, s2 = c_ref[...], s1_ref[...], s2_ref[...]
    half = ROPE_DIMS // 2
    for c0 in range(0, o_ref.shape[1], EPI_COLS):
        acc = jnp.dot(a_ref[...], w_ref[:, c0:c0 + EPI_COLS], preferred_element_type=F32)
        for h0 in range(0, EPI_COLS, HEAD_DIM):
            xh = acc[:, h0:h0 + HEAD_DIM]
            o_ref[:, c0 + h0:c0 + h0 + HEAD_DIM] = (
                xh * c + pltpu.roll(xh, HEAD_DIM - half, 1) * s1 + pltpu.roll(xh, half, 1) * s2)


MM_TM = 1024
MM_TN = (1536, 1024, 512)


def _col_tile(n):
    return next((t for t in MM_TN if n % t == 0), n)


def matmul(a, w, out_dtype, *, n=None, col_off=0, bias=None, residual=None, rope=None):
    m, k = a.shape
    n = w.shape[1] if n is None else n
    tm, tn = _tile(m, MM_TM), _col_tile(n)
    assert col_off % tn == 0
    joff = col_off // tn
    a_spec = pl.BlockSpec((tm, k), lambda i, j: (i, 0))
    w_spec = pl.BlockSpec((k, tn), lambda i, j: (0, joff + j))
    o_spec = pl.BlockSpec((tm, tn), lambda i, j: (i, j))
    if bias is not None:
        kern, extra, extra_specs = _mm_sigmoid_kernel, (bias.reshape(1, n),), [pl.BlockSpec((1, tn), lambda i, j: (0, j))]
    elif residual is not None:
        kern, extra, extra_specs = _mm_residual_kernel, (residual,), [o_spec]
    elif rope is not None:
        tables, seq = rope
        assert tn == SWA_WIDTH and n == 3 * SWA_WIDTH
        per = seq // tm
        t_spec = pl.BlockSpec((None, tm, HEAD_DIM), lambda i, j: (j, i % per, 0))
        kern = _mm_rope_kernel
        extra, extra_specs = tuple(tables), [t_spec] * 3
    else:
        kern, extra, extra_specs = _mm_kernel, (), []
    return pl.pallas_call(
        kern,
        out_shape=jax.ShapeDtypeStruct((m, n), out_dtype),
        grid=(m // tm, n // tn),
        in_specs=[a_spec, w_spec] + extra_specs,
        out_specs=o_spec,
        compiler_params=_cp("parallel", "arbitrary"),
        name="matmul",
    )(a, w, *extra)


def rope_tables(seq):
    half = ROPE_DIMS // 2
    inv = ROPE_THETA ** (-jnp.arange(half, dtype=F32) / half)
    ang = jnp.arange(seq, dtype=F32)[:, None] * inv[None, :]
    cos, sin = jnp.cos(ang), jnp.sin(ang)
    zeros = jnp.zeros((seq, HEAD_DIM - ROPE_DIMS), F32)
    zh = jnp.zeros((seq, half), F32)
    c = jnp.concatenate([cos, cos, zeros + 1.0], axis=1)
    s1 = jnp.concatenate([-sin, zh, zeros], axis=1)
    s2 = jnp.concatenate([zh, sin, zeros], axis=1)
    ident = (jnp.ones_like(c), jnp.zeros_like(c), jnp.zeros_like(c))
    return tuple(jnp.stack([tab * (HEAD_DIM ** -0.5), tab, idt]) for tab, idt in zip((c, s1, s2), ident))


CONV_HALO = 16


def _mm_conv_kernel(a_ref, ap_ref, an_ref, w_ref, cw_ref, o_ref, buf_ref, *, tm, tiles_per_seq):
    i, kind = pl.program_id(0), pl.program_id(1)
    halo = CONV_HALO
    rows = tm + 2 * halo
    keep_prev = jnp.where((i % tiles_per_seq) == 0, 0.0, 1.0)
    keep_next = jnp.where((i % tiles_per_seq) == tiles_per_seq - 1, 0.0, 1.0)
    qk_mul = jnp.where(kind == 0, HEAD_DIM ** -0.5, 1.0)
    w = w_ref[...]
    buf_ref[0:halo, :] = jnp.dot(ap_ref[...], w, preferred_element_type=F32) * keep_prev
    buf_ref[halo:halo + tm, :] = jnp.dot(a_ref[...], w, preferred_element_type=F32)
    buf_ref[halo + tm:, :] = jnp.dot(an_ref[...], w, preferred_element_type=F32) * keep_next
    for h in range(GDN_HEADS):
        cols = slice(h * HEAD_DIM, (h + 1) * HEAD_DIM)
        xh = buf_ref[:, cols]
        acc = xh * cw_ref[GDN_CONV // 2:GDN_CONV // 2 + 1, cols]
        for t in range(GDN_CONV):
            if t != GDN_CONV // 2:
                acc = acc + pltpu.roll(xh, (GDN_CONV // 2 - t) % rows, 0) * cw_ref[t:t + 1, cols]
        acc = acc[halo:halo + tm, :]
        y = acc * jax.nn.sigmoid(acc)
        inv = lax.rsqrt(jnp.sum(y * y, axis=-1, keepdims=True) + EPS) * qk_mul
        o_ref[h] = (y * jnp.where(kind < 2, inv, 1.0)).astype(o_ref.dtype)


def matmul_conv_norm(a, w, col_off, conv_w, seq):
    m, k = a.shape
    tn = GDN_WIDTH
    tm = _tile(seq, MM_TM)
    assert col_off % tn == 0 and tm % CONV_HALO == 0
    joff = col_off // tn
    per_h = tm // CONV_HALO
    nblk_h = m // CONV_HALO
    return pl.pallas_call(
        functools.partial(_mm_conv_kernel, tm=tm, tiles_per_seq=seq // tm),
        out_shape=jax.ShapeDtypeStruct((3 * GDN_HEADS, m, HEAD_DIM), BF16),
        grid=(m // tm, 3),
        in_specs=[
            pl.BlockSpec((tm, k), lambda i, j: (i, 0)),
            pl.BlockSpec((CONV_HALO, k), lambda i, j: (jnp.maximum(i * per_h - 1, 0), 0)),
            pl.BlockSpec((CONV_HALO, k), lambda i, j: (jnp.minimum((i + 1) * per_h, nblk_h - 1), 0)),
            pl.BlockSpec((k, tn), lambda i, j: (0, joff + j)),
            pl.BlockSpec((GDN_CONV, tn), lambda i, j: (0, j)),
        ],
        out_specs=pl.BlockSpec((GDN_HEADS, tm, HEAD_DIM), lambda i, j: (j, i, 0)),
        scratch_shapes=[pltpu.VMEM((tm + 2 * CONV_HALO, tn), F32)],
        compiler_params=_cp("parallel", "arbitrary"),
        name="matmul_conv_norm",
    )(a, a, a, w, conv_w)


def _swa_kernel(q_ref, k_ref, v_ref, o_ref, acc_ref, m_ref, l_ref, *, seq):
    g = pl.program_id(2)

    @pl.when(g == 0)
    def _():
        acc_ref[...] = jnp.zeros_like(acc_ref)
        m_ref[...] = jnp.full_like(m_ref, NEG_INF)
        l_ref[...] = jnp.zeros_like(l_ref)

    for gi, (window, dil) in enumerate(SWA_GROUPS):
        radius = window // (2 * dil)
        assert radius <= SWA_BLOCK
        sub = seq // dil
        qb = min(128, sub)
        win = min(sub, qb + 2 * SWA_BLOCK)
        nqb = sub // qb

        @pl.when(g == gi)
        def _(dil=dil, radius=radius, sub=sub, qb=qb, win=win, nqb=nqb):
            def rows(start, size):
                return pl.ds(start, size) if dil == 1 else pl.ds(start, size, stride=dil)

            total = dil * nqb
            unroll = SWA_UNROLL if total % SWA_UNROLL == 0 else 1
            idx = range(unroll)

            def body(step, carry):
                its = [step * unroll + u for u in idx]
                qss = [(it % nqb) * qb for it in its]
                wss = [jnp.clip(qs - SWA_BLOCK, 0, sub - win) for qs in qss]
                q_rows = [rows(it // nqb + dil * qs, qb) for it, qs in zip(its, qss)]
                k_rows = [rows(it // nqb + dil * ws, win) for it, ws in zip(its, wss)]
                q = [q_ref[r, :].astype(BF16) for r in q_rows]
                k = [k_ref[r, :].astype(BF16) for r in k_rows]
                v = [v_ref[r, :].astype(BF16) for r in k_rows]
                s = [_nt(q[u], k[u]) for u in idx]
                off = lax.broadcasted_iota(jnp.int32, (qb, win), 0) - lax.broadcasted_iota(jnp.int32, (qb, win), 1)
                s = [jnp.where(jnp.abs(off + (qss[u] - wss[u])) <= radius, s[u], NEG_INF) for u in idx]
                m_old = [m_ref[r, :] for r in q_rows]
                m_new = [jnp.maximum(m_old[u], jnp.max(s[u], axis=1, keepdims=True)) for u in idx]
                p = [jnp.exp(s[u] - m_new[u]) for u in idx]
                corr = [jnp.exp(m_old[u] - m_new[u]) for u in idx]
                pv = [jnp.dot(p[u].astype(BF16), v[u], preferred_element_type=F32) for u in idx]
                for u in idx:
                    l_ref[q_rows[u], :] = l_ref[q_rows[u], :] * corr[u] + jnp.sum(p[u], axis=1, keepdims=True)
                    acc_ref[q_rows[u], :] = acc_ref[q_rows[u], :] * corr[u] + pv[u]
                    m_ref[q_rows[u], :] = m_new[u]
                return carry

            lax.fori_loop(0, total // unroll, body, 0)

    @pl.when(g == len(SWA_GROUPS) - 1)
    def _():
        o_ref[...] = (acc_ref[...] / l_ref[...]).astype(o_ref.dtype)


def dilated_swa(qkv, batch, seq):
    nh = SWA_HEADS_PER_GROUP
    qkv3 = qkv.reshape(batch, seq, 3 * SWA_WIDTH)

    def spec(off):
        return pl.BlockSpec((None, seq, HEAD_DIM), lambda b, h, g: (b, 0, off + g * nh + h))

    out = pl.pallas_call(
        functools.partial(_swa_kernel, seq=seq),
        out_shape=jax.ShapeDtypeStruct((batch, seq, nh * HEAD_DIM), BF16),
        grid=(batch, nh, len(SWA_GROUPS)),
        in_specs=[spec(0), spec(SWA_HEADS), spec(2 * SWA_HEADS)],
        out_specs=pl.BlockSpec((None, seq, HEAD_DIM), lambda b, h, g: (b, 0, h)),
        scratch_shapes=[pltpu.VMEM((seq, HEAD_DIM), F32), pltpu.VMEM((seq, 1), F32), pltpu.VMEM((seq, 1), F32)],
        compiler_params=_cp("parallel", "parallel", "arbitrary"),
        name="dilated_swa",
    )(qkv3, qkv3, qkv3)
    return out.reshape(batch * seq, nh * HEAD_DIM)


def _gdn_gate_kernel(ba_ref, alog_ref, dtb_ref, beta_ref, gc_ref, *, ts):
    x = ba_ref[...]
    nhd = 2 * GDN_HEADS
    lane = lax.broadcasted_iota(jnp.int32, (ts, LANES), 1)
    row = lax.broadcasted_iota(jnp.int32, (ts, LANES), 0) % GDN_CHUNK
    beta_ref[...] = jax.nn.sigmoid(x)
    z = x + dtb_ref[...]
    softplus = jnp.maximum(z, 0.0) + jnp.log(1.0 + jnp.exp(-jnp.abs(z)))
    g = jnp.where(jnp.logical_and(lane >= nhd, lane < 2 * nhd), -jnp.exp(alog_ref[...]) * softplus, 0.0)
    pre, suf = g, g
    s = 1
    while s < GDN_CHUNK:
        pre = pre + jnp.where(row >= s, pltpu.roll(pre, s, 0), 0.0)
        suf = suf + jnp.where(row < GDN_CHUNK - s, pltpu.roll(suf, ts - s, 0), 0.0)
        s *= 2
    gc_ref[...] = jnp.where(lane < nhd + GDN_HEADS, pre, suf)


def gdn_gates(ba, a_log, dt_bias):
    t = ba.shape[0]
    ts = _tile(t, 1024)
    nhd = 2 * GDN_HEADS
    pad = lambda v: jnp.zeros((1, LANES), F32).at[0, nhd:2 * nhd].set(v.reshape(-1).astype(F32))
    shp = jax.ShapeDtypeStruct((t, LANES), F32)
    vec = pl.BlockSpec((1, LANES), lambda i: (0, 0))
    blk = pl.BlockSpec((ts, LANES), lambda i: (i, 0))
    return pl.pallas_call(
        functools.partial(_gdn_gate_kernel, ts=ts),
        out_shape=(shp, shp),
        grid=(t // ts,),
        in_specs=[blk, vec, vec],
        out_specs=(blk, blk),
        compiler_params=_cp("parallel"),
        name="gdn_gates",
    )(ba, pad(a_log), pad(dt_bias))


def _nt(a, b):
    return lax.dot_general(a, b, (((1,), (1,)), ((), ())), preferred_element_type=F32)


def _tn(a, b):
    return lax.dot_general(a, b, (((0,), (0,)), ((), ())), preferred_element_type=F32)


def _mmb(a, b):
    return jnp.dot(a.astype(BF16), b.astype(BF16), preferred_element_type=F32)


def _gdn_chunks(qs, ks, vs, bcols, gcols, states, revs):
    c = GDN_CHUNK
    idx = range(len(qs))
    ri = lax.broadcasted_iota(jnp.int32, (c, c), 0)
    ci = lax.broadcasted_iota(jnp.int32, (c, c), 1)
    strict = {False: ri > ci, True: ri < ci}
    incl = {False: ri >= ci, True: ri <= ci}
    eye = jnp.where(ri == ci, 1.0, 0.0)
    lane = lax.broadcasted_iota(jnp.int32, (c, LANES), 1)
    ones3 = jnp.where(lane < 3, 1.0, 0.0)
    ones3_hi = jnp.where(jnp.logical_and(lane >= 3, lane < 6), 1.0, 0.0)
    g_last = [gcols[i][0:1, :] if revs[i] else gcols[i][c - 1:c, :] for i in idx]
    egc = [jnp.exp(gcols[i]) for i in idx]
    q_state = [_mmb(qs[i] * egc[i], states[i]) for i in idx]
    diff = []
    for i in idx:
        g1 = gcols[i].astype(BF16).astype(F32)
        g2 = (gcols[i] - g1).astype(BF16).astype(F32)
        g3 = (gcols[i] - g1 - g2).astype(BF16).astype(F32)
        pieces = jnp.where(lane == 0, g1, jnp.where(lane == 1, g2, jnp.where(lane == 2, g3, 0.0)))
        xm = pieces + ones3_hi
        ym = ones3 - pltpu.roll(pieces, 3, 1)
        diff.append(_nt(xm.astype(BF16), ym.astype(BF16)))
    decay = [jnp.where(incl[revs[i]], jnp.exp(jnp.where(incl[revs[i]], diff[i], 0.0)), 0.0) for i in idx]
    kb = [ks[i] * bcols[i] for i in idx]
    k16 = [ks[i].astype(BF16) for i in idx]
    lm = [jnp.where(strict[revs[i]], _nt(kb[i].astype(BF16), k16[i]) * decay[i], 0.0) for i in idx]
    attn = [_nt(qs[i].astype(BF16), k16[i]) * decay[i] for i in idx]
    tinv = [eye - lm[i] for i in idx]
    pw = [_mmb(lm[i], lm[i]) for i in idx]
    n = 4
    while n < c:
        both = [_mmb(jnp.concatenate([tinv[i], pw[i]], axis=0), pw[i]) for i in idx]
        tinv = [tinv[i] + both[i][:c] for i in idx]
        pw = [both[i][c:] for i in idx]
        n *= 2
    tinv = [tinv[i] + _mmb(tinv[i], pw[i]) for i in idx]
    u = [_mmb(tinv[i], jnp.concatenate([vs[i] * bcols[i], kb[i] * egc[i]], axis=1)) for i in idx]
    v_new = [u[i][:, :HEAD_DIM] - _mmb(u[i][:, HEAD_DIM:], states[i]) for i in idx]
    outs = [q_state[i] + _mmb(attn[i], v_new[i]) for i in idx]
    k_e = [ks[i] * jnp.exp(g_last[i] - gcols[i]) for i in idx]
    new_states = [states[i] * jnp.exp(g_last[i]) + _tn(k_e[i].astype(BF16), v_new[i].astype(BF16)) for i in idx]
    return outs, new_states


def _gdn_kernel(qf_ref, kf_ref, vf_ref, bf_ref, gf_ref, qb_ref, kb_ref, vb_ref, bb_ref, gb_ref,
                of_ref, ob_ref, state_ref, *, chunks):
    @pl.when(pl.program_id(1) == 0)
    def _():
        state_ref[...] = jnp.zeros_like(state_ref)

    nh = GDN_HEADS

    def body(it, carry):
        rows_f = pl.ds(pl.multiple_of(it * GDN_CHUNK, GDN_CHUNK), GDN_CHUNK)
        rows_b = pl.ds(pl.multiple_of((chunks - 1 - it) * GDN_CHUNK, GDN_CHUNK), GDN_CHUNK)
        tabs = ((qf_ref, kf_ref, vf_ref, bf_ref[rows_f, :], gf_ref[rows_f, :], rows_f, 0),
                (qb_ref, kb_ref, vb_ref, bb_ref[rows_b, :], gb_ref[rows_b, :], rows_b, nh))
        qs, ks, vs, bcols, gcols, states, revs = [], [], [], [], [], [], []
        for d, (q_ref, k_ref, v_ref, bt, gt, rows, lane0) in enumerate(tabs):
            for h in range(nh):
                qs.append(q_ref[h, rows, :].astype(F32))
                ks.append(k_ref[h, rows, :].astype(F32))
                vs.append(v_ref[h, rows, :].astype(F32))
                bcols.append(bt[:, lane0 + h:lane0 + h + 1])
                gcols.append(gt[:, 2 * nh + lane0 + h:2 * nh + lane0 + h + 1])
                states.append(state_ref[d * nh + h])
                revs.append(d == 1)
        outs, new_states = _gdn_chunks(qs, ks, vs, bcols, gcols, states, revs)
        for d, o_ref in enumerate((of_ref, ob_ref)):
            for h in range(nh):
                state_ref[d * nh + h] = new_states[d * nh + h]
                o_ref[h, tabs[d][5], :] = outs[d * nh + h].astype(o_ref.dtype)
        return carry

    lax.fori_loop(0, chunks, body, 0)


def gdn_scan(qkv_h, beta, gc, batch, seq):
    t = batch * seq
    rows = _tile(seq, 512)
    nblk = seq // rows
    fwd = lambda b, ib: b * nblk + ib
    bwd = lambda b, ib: b * nblk + nblk - 1 - ib

    def specs(blk):
        qspec = lambda part: pl.BlockSpec((GDN_HEADS, rows, HEAD_DIM), lambda b, ib: (part, blk(b, ib), 0))
        gspec = pl.BlockSpec((rows, LANES), lambda b, ib: (blk(b, ib), 0))
        return [qspec(0), qspec(1), qspec(2), gspec, gspec]

    ospec = lambda blk: pl.BlockSpec((GDN_HEADS, rows, HEAD_DIM), lambda b, ib: (0, blk(b, ib), 0))
    out = jax.ShapeDtypeStruct((GDN_HEADS, t, HEAD_DIM), BF16)
    return pl.pallas_call(
        functools.partial(_gdn_kernel, chunks=rows // GDN_CHUNK),
        out_shape=(out, out),
        grid=(batch, nblk),
        in_specs=specs(fwd) + specs(bwd),
        out_specs=(ospec(fwd), ospec(bwd)),
        scratch_shapes=[pltpu.VMEM((2 * GDN_HEADS, HEAD_DIM, HEAD_DIM), F32)],
        compiler_params=_cp("parallel", "arbitrary"),
        name="gdn_scan",
    )(qkv_h, qkv_h, qkv_h, beta, gc, qkv_h, qkv_h, qkv_h, beta, gc)


def _gdn_post_kernel(of_ref, ob_ref, z_ref, g_ref, o_ref):
    for h in range(of_ref.shape[0]):
        cols = slice(h * HEAD_DIM, (h + 1) * HEAD_DIM)
        o = of_ref[h].astype(F32) + ob_ref[h].astype(F32)
        r = lax.rsqrt(jnp.mean(o * o, axis=-1, keepdims=True) + EPS)
        z = z_ref[:, cols].astype(F32)
        o_ref[:, cols] = (o * r * g_ref[...] * (z * jax.nn.sigmoid(z))).astype(o_ref.dtype)


def gdn_post(o_f, o_b, z, norm_g):
    nhd, t, _ = o_f.shape
    ts = _tile(t, 512)
    hspec = pl.BlockSpec((nhd, ts, HEAD_DIM), lambda i: (0, i, 0))
    tspec = pl.BlockSpec((ts, nhd * HEAD_DIM), lambda i: (i, 0))
    return pl.pallas_call(
        _gdn_post_kernel,
        out_shape=jax.ShapeDtypeStruct((t, nhd * HEAD_DIM), BF16),
        grid=(t // ts,),
        in_specs=[hspec, hspec, tspec, pl.BlockSpec((1, HEAD_DIM), lambda i: (0, 0))],
        out_specs=tspec,
        compiler_params=_cp("parallel"),
        name="gdn_post",
    )(o_f, o_b, z, norm_g.reshape(1, HEAD_DIM).astype(F32))


def _mem_attn_kernel(q_ref, k_ref, v_ref, o_ref):
    scale = MEM_HEAD_DIM ** -0.5
    for h in range(MEM_HEADS):
        cols = slice(h * MEM_HEAD_DIM, (h + 1) * MEM_HEAD_DIM)
        s = _nt(q_ref[:, cols], k_ref[:, cols]) * scale
        p = jnp.exp(s - jnp.max(s, axis=-1, keepdims=True))
        p = p / jnp.sum(p, axis=-1, keepdims=True)
        o_ref[:, cols] = jnp.dot(p.astype(BF16), v_ref[:, cols], preferred_element_type=F32).astype(o_ref.dtype)


def mem_attention(mq, kv, batch, seq, mem_tokens):
    tq = _tile(seq, 512)
    per = seq // tq
    return pl.pallas_call(
        _mem_attn_kernel,
        out_shape=jax.ShapeDtypeStruct(mq.shape, BF16),
        grid=(batch * per,),
        in_specs=[
            pl.BlockSpec((tq, MEM_WIDTH), lambda i: (i, 0)),
            pl.BlockSpec((mem_tokens, MEM_WIDTH), lambda i: (i // per, 0)),
            pl.BlockSpec((mem_tokens, MEM_WIDTH), lambda i: (i // per, 1)),
        ],
        out_specs=pl.BlockSpec((tq, MEM_WIDTH), lambda i: (i, 0)),
        compiler_params=_cp("parallel"),
        name="mem_attention",
    )(mq, kv, kv)


def _merge_kernel(oa_ref, ob_ref, om_ref, wa_ref, wb_ref, wm_ref, g0_ref, g1_ref, g2_ref, o_ref):
    ya = jnp.dot(oa_ref[...], wa_ref[...], preferred_element_type=F32)
    yb = jnp.dot(ob_ref[...], wb_ref[...], preferred_element_type=F32)
    ym = jnp.dot(om_ref[...], wm_ref[...], preferred_element_type=F32)
    mix = g0_ref[...].astype(F32) * ya + g1_ref[...].astype(F32) * yb + g2_ref[...].astype(F32) * ym
    o_ref[...] = mix.astype(o_ref.dtype)


def gated_merge(o_a, o_b, o_m, w_a, w_b, w_m, gates):
    t = o_a.shape[0]
    d = w_a.shape[1]
    tm, tn = _tile(t, 1024), _tile(d, 512)
    nj = d // tn
    act = lambda o: pl.BlockSpec((tm, o.shape[1]), lambda i, j: (i, 0))
    wsp = lambda w: pl.BlockSpec((w.shape[0], tn), lambda i, j: (0, j))
    gsp = lambda br: pl.BlockSpec((tm, tn), lambda i, j: (i, br * nj + j))
    return pl.pallas_call(
        _merge_kernel,
        out_shape=jax.ShapeDtypeStruct((t, d), BF16),
        grid=(t // tm, nj),
        in_specs=[act(o_a), act(o_b), act(o_m), wsp(w_a), wsp(w_b), wsp(w_m), gsp(0), gsp(1), gsp(2)],
        out_specs=pl.BlockSpec((tm, tn), lambda i, j: (i, j)),
        compiler_params=_cp("parallel", "arbitrary"),
        name="gated_merge",
    )(o_a, o_b, o_m, w_a, w_b, w_m, gates, gates, gates)


def _pack_bf16_pairs(x):
    w = x.shape[1] // 2
    lo = lax.bitcast_convert_type(x[:, :w].astype(BF16).astype(F32), jnp.uint32)
    hi = lax.bitcast_convert_type(x[:, w:].astype(BF16).astype(F32), jnp.uint32)
    return (hi & jnp.uint32(0xFFFF0000)) | (lo >> 16)


def _unpack_bf16_pairs(p):
    lo = lax.bitcast_convert_type(p << 16, F32)
    hi = lax.bitcast_convert_type(p & jnp.uint32(0xFFFF0000), F32)
    return lo, hi


MOE_ROWS_PER_STEP = 256
ROUTE_ROWS = SUBLANES + N_EXPERTS


def _route_kernel(x_ref, g_ref, w_ref, b_ref, xn_ref, eid_ref, wt_ref):
    x = x_ref[...]
    r = lax.rsqrt(jnp.mean(x * x, axis=-1, keepdims=True) + EPS)
    xn = x * r * g_ref[...]
    xn_ref[...] = _pack_bf16_pairs(xn)
    tm = x.shape[0]
    x_hi = xn.astype(BF16)
    x_lo = (xn - x_hi.astype(F32)).astype(BF16)
    w_hi, w_lo = w_ref[0], w_ref[1]
    lg = (jnp.dot(x_hi, w_hi, preferred_element_type=F32) + jnp.dot(x_lo, w_hi, preferred_element_type=F32)
          + jnp.dot(x_hi, w_lo, preferred_element_type=F32))
    lg = lg.T[:ROUTE_ROWS, :] + b_ref[...]
    gl = [lg[i:i + 1, :] for i in range(N_GROUPS)]
    gmax = functools.reduce(jnp.maximum, gl)
    grp = jnp.full((1, tm), N_GROUPS - 1, jnp.int32)
    for i in range(N_GROUPS - 2, -1, -1):
        grp = jnp.where(gl[i] == gmax, i, grp)
    p_grp = 1.0 / functools.reduce(jnp.add, [jnp.exp(v - gmax) for v in gl])
    sel = lg[SUBLANES + (N_GROUPS - 1) * EXPERTS_PER_GROUP:SUBLANES + N_GROUPS * EXPERTS_PER_GROUP, :]
    for i in range(N_GROUPS - 2, -1, -1):
        lo = SUBLANES + i * EXPERTS_PER_GROUP
        sel = jnp.where(grp == i, lg[lo:lo + EXPERTS_PER_GROUP, :], sel)
    rowi = lax.broadcasted_iota(jnp.int32, (EXPERTS_PER_GROUP, tm), 0)
    v1 = jnp.max(sel, axis=0, keepdims=True)
    i1 = jnp.min(jnp.where(sel == v1, rowi, EXPERTS_PER_GROUP), axis=0, keepdims=True)
    sel2 = jnp.where(rowi == i1, -jnp.inf, sel)
    v2 = jnp.max(sel2, axis=0, keepdims=True)
    i2 = jnp.min(jnp.where(sel2 == v2, rowi, EXPERTS_PER_GROUP), axis=0, keepdims=True)
    e2 = jnp.exp(v2 - v1)
    w1 = p_grp / (1.0 + e2)
    eid_ref[0:1, :] = grp * EXPERTS_PER_GROUP + i1
    eid_ref[1:2, :] = grp * EXPERTS_PER_GROUP + i2
    wt_ref[0:1, :] = w1
    wt_ref[1:2, :] = w1 * e2


def moe_route(x, g, w_grp, b_grp, w_exp):
    t, d = x.shape
    tm = _tile(t, 512)
    w = jnp.zeros((d, LANES), F32).at[:, :N_GROUPS].set(w_grp).at[:, SUBLANES:ROUTE_ROWS].set(w_exp)
    w_hi = w.astype(BF16)
    w = jnp.stack([w_hi, (w - w_hi.astype(F32)).astype(BF16)])
    b = jnp.zeros((ROUTE_ROWS, 1), F32).at[:N_GROUPS, 0].set(b_grp.astype(F32))
    return pl.pallas_call(
        _route_kernel,
        out_shape=(jax.ShapeDtypeStruct((t, d // 2), jnp.uint32), jax.ShapeDtypeStruct((EXPERT_TOPK, t), jnp.int32),
                   jax.ShapeDtypeStruct((EXPERT_TOPK, t), F32)),
        grid=(t // tm,),
        in_specs=[pl.BlockSpec((tm, d), lambda i: (i, 0)), pl.BlockSpec((1, d), lambda i: (0, 0)),
                  pl.BlockSpec((2, d, LANES), lambda i: (0, 0, 0)), pl.BlockSpec((ROUTE_ROWS, 1), lambda i: (0, 0))],
        out_specs=(pl.BlockSpec((tm, d // 2), lambda i: (i, 0)), pl.BlockSpec((EXPERT_TOPK, tm), lambda i: (0, i)),
                   pl.BlockSpec((EXPERT_TOPK, tm), lambda i: (0, i))),
        compiler_params=_cp("parallel"),
        name="moe_route",
    )(x, g.reshape(1, d), w, b)


def _rank_kernel(e_ref, rank_ref, cnt_ref, *, tb):
    @pl.when(pl.program_id(0) == 0)
    def _():
        cnt_ref[...] = jnp.zeros_like(cnt_ref)

    e = e_ref[...]
    onehot = jnp.where(lax.broadcasted_iota(jnp.int32, (N_EXPERTS, tb), 0) == e, 1.0, 0.0)
    earlier = jnp.where(lax.broadcasted_iota(jnp.int32, (tb, tb), 0) < lax.broadcasted_iota(jnp.int32, (tb, tb), 1),
                        1.0, 0.0)
    before = _mmb(onehot, earlier) + cnt_ref[...]
    rank_ref[...] = jnp.sum(onehot * before, axis=0, keepdims=True).astype(jnp.int32)
    cnt_ref[...] = cnt_ref[...] + jnp.sum(onehot, axis=1, keepdims=True)


def moe_rank(e_flat):
    n = e_flat.shape[1]
    tb = _tile(n, 512)
    return pl.pallas_call(
        functools.partial(_rank_kernel, tb=tb),
        out_shape=(jax.ShapeDtypeStruct((1, n), jnp.int32), jax.ShapeDtypeStruct((N_EXPERTS, 1), F32)),
        grid=(n // tb,),
        in_specs=[pl.BlockSpec((1, tb), lambda i: (0, i))],
        out_specs=(pl.BlockSpec((1, tb), lambda i: (0, i)), pl.BlockSpec((N_EXPERTS, 1), lambda i: (0, 0))),
        compiler_params=_cp("arbitrary"),
        name="moe_rank",
    )(e_flat)


def _dispatch_kernel(dest_ref, cnt_ref, pstart_ref, xn_ref, xs_ref, xbuf_ref, zero_ref, load_sem, scat_sem, fill_sem,
                     *, tb, tokens, nblk):
    i, nsteps = pl.program_id(0), pl.num_programs(0)
    slot = i % 2

    def load(step, sl):
        return pltpu.make_async_copy(xn_ref.at[pl.ds(step * tb, tb)], xbuf_ref.at[sl], load_sem.at[sl])

    def scatters(step, sl):
        return [pltpu.make_async_copy(xbuf_ref.at[sl, pl.ds(j, 1)],
                                      xs_ref.at[pl.ds(dest_ref[kk * tokens + step * tb + j], 1)], scat_sem.at[sl])
                for j in range(tb) for kk in range(EXPERT_TOPK)]

    def zero_rows(dst_row, nrows):
        return pltpu.make_async_copy(zero_ref.at[pl.ds(0, nrows)], xs_ref.at[pl.ds(dst_row, nrows)], fill_sem)

    @pl.when(i == 0)
    def _():
        load(0, 0).start()
        zero_ref[...] = jnp.zeros_like(zero_ref)
        last = N_EXPERTS - 1
        n_used = (pstart_ref[last] + cnt_ref[last] + MOE_BLOCK - 1) // MOE_BLOCK

        def fill_block(blk, carry):
            zero_rows(blk * MOE_BLOCK, MOE_BLOCK).start()
            zero_rows(blk * MOE_BLOCK, MOE_BLOCK).wait()
            return carry

        lax.fori_loop(n_used, nblk, fill_block, 0)

        def per_expert(e, carry):
            @pl.when(cnt_ref[e] % MOE_BLOCK != 0)
            def _():
                blk = (pstart_ref[e] + cnt_ref[e]) // MOE_BLOCK
                zero_rows(blk * MOE_BLOCK, MOE_BLOCK).start()
                zero_rows(blk * MOE_BLOCK, MOE_BLOCK).wait()

            return carry

        lax.fori_loop(0, N_EXPERTS, per_expert, 0)

    @pl.when(i >= 1)
    def _():
        for cp in scatters(i - 1, 1 - slot):
            cp.wait()

    @pl.when(i + 1 < nsteps)
    def _():
        load(i + 1, 1 - slot).start()

    load(i, slot).wait()
    for cp in scatters(i, slot):
        cp.start()

    @pl.when(i == nsteps - 1)
    def _():
        for cp in scatters(i, slot):
            cp.wait()


def moe_dispatch(xn, dest_flat, counts, pad_start, n_slots):
    t, d = xn.shape
    tb = _tile(t, MOE_ROWS_PER_STEP)
    return pl.pallas_call(
        functools.partial(_dispatch_kernel, tb=tb, tokens=t, nblk=n_slots // MOE_BLOCK),
        out_shape=jax.ShapeDtypeStruct((n_slots, d), xn.dtype),
        grid_spec=pltpu.PrefetchScalarGridSpec(
            num_scalar_prefetch=3,
            grid=(t // tb,),
            in_specs=[pl.BlockSpec(memory_space=pl.ANY)],
            out_specs=pl.BlockSpec(memory_space=pl.ANY),
            scratch_shapes=[pltpu.VMEM((2, tb, d), xn.dtype), pltpu.VMEM((MOE_BLOCK, d), xn.dtype),
                            pltpu.SemaphoreType.DMA((2,)), pltpu.SemaphoreType.DMA((2,)), pltpu.SemaphoreType.DMA],
        ),
        compiler_params=pltpu.CompilerParams(dimension_semantics=("arbitrary",), has_side_effects=True,
                                             vmem_limit_bytes=VMEM_LIMIT),
        name="moe_dispatch",
    )(dest_flat, counts, pad_start, xn)


def _expert_kernel(be_ref, nused_ref, x_ref, wg_ref, wu_ref, wd_ref, y_ref, wg16_ref, wu16_ref, wd16_ref):
    i = pl.program_id(0)
    used = i < nused_ref[0]

    @pl.when(jnp.logical_and(used, jnp.logical_or(i == 0, be_ref[i] != be_ref[jnp.maximum(i - 1, 0)])))
    def _():
        wg16_ref[...] = wg_ref[...].astype(BF16)
        wu16_ref[...] = wu_ref[...].astype(BF16)
        wd16_ref[...] = wd_ref[...].astype(BF16)

    @pl.when(used)
    def _():
        x_lo, x_hi = (v.astype(BF16) for v in _unpack_bf16_pairs(x_ref[...]))
        half = x_lo.shape[1]

        def proj(w_ref):
            return (jnp.dot(x_lo, w_ref[:half, :], preferred_element_type=F32)
                    + jnp.dot(x_hi, w_ref[half:, :], preferred_element_type=F32))

        hg, hu = proj(wg16_ref), proj(wu16_ref)
        hmid = (hg * jax.nn.sigmoid(hg) * hu).astype(BF16)
        y_ref[...] = _pack_bf16_pairs(jnp.dot(hmid, wd16_ref[...], preferred_element_type=F32))

    @pl.when(pl.program_id(0) >= nused_ref[0])
    def _():
        y_ref[...] = jnp.zeros_like(y_ref)


def moe_experts(xs, blk_expert, n_used, w_gate, w_up, w_down):
    p, dp = xs.shape
    d, de = w_gate.shape[1:]
    nblk = p // MOE_BLOCK
    row = lambda i, be, nu: (jnp.minimum(i, nu[0] - 1), 0)
    wsel = lambda i, be, nu: (be[jnp.minimum(i, nu[0] - 1)], 0, 0)
    return pl.pallas_call(
        _expert_kernel,
        out_shape=jax.ShapeDtypeStruct((p, dp), jnp.uint32),
        grid_spec=pltpu.PrefetchScalarGridSpec(
            num_scalar_prefetch=2,
            grid=(nblk,),
            in_specs=[pl.BlockSpec((MOE_BLOCK, dp), row), pl.BlockSpec((None, d, de), wsel),
                      pl.BlockSpec((None, d, de), wsel), pl.BlockSpec((None, de, d), wsel)],
            out_specs=pl.BlockSpec((MOE_BLOCK, dp), lambda i, be, nu: (i, 0)),
            scratch_shapes=[pltpu.VMEM((d, de), BF16), pltpu.VMEM((d, de), BF16), pltpu.VMEM((de, d), BF16)],
        ),
        compiler_params=_cp("arbitrary"),
        name="moe_experts",
    )(blk_expert, n_used, xs, w_gate, w_up, w_down)


def _combine_kernel(dest_ref, x_ref, wt_ref, g_ref, ys_ref, o_ref, y_ref, sems, *, tb, tokens):
    i, nsteps = pl.program_id(0), pl.num_programs(0)
    slot = i % 2

    def gathers(step, sl):
        return [pltpu.make_async_copy(ys_ref.at[pl.ds(dest_ref[kk * tokens + step * tb + j], 1)],
                                      y_ref.at[sl, kk, pl.ds(j, 1)], sems.at[sl])
                for j in range(tb) for kk in range(EXPERT_TOPK)]

    @pl.when(i == 0)
    def _():
        for cp in gathers(0, 0):
            cp.start()

    @pl.when(i + 1 < nsteps)
    def _():
        for cp in gathers(i + 1, 1 - slot):
            cp.start()

    for cp in gathers(i, slot):
        cp.wait()
    y0_lo, y0_hi = _unpack_bf16_pairs(y_ref[slot, 0])
    y1_lo, y1_hi = _unpack_bf16_pairs(y_ref[slot, 1])
    w0, w1 = wt_ref[:, 0:1], wt_ref[:, 1:2]
    h = x_ref[...] + jnp.concatenate([w0 * y0_lo + w1 * y1_lo, w0 * y0_hi + w1 * y1_hi], axis=1)
    r = lax.rsqrt(jnp.mean(h * h, axis=-1, keepdims=True) + EPS)
    o_ref[...] = h * r * g_ref[...]


def moe_combine(x, ys, dest_flat, wts_tok, g_final):
    t, d = x.shape
    tb = _tile(t, MOE_ROWS_PER_STEP)
    return pl.pallas_call(
        functools.partial(_combine_kernel, tb=tb, tokens=t),
        out_shape=jax.ShapeDtypeStruct((t, d), F32),
        grid_spec=pltpu.PrefetchScalarGridSpec(
            num_scalar_prefetch=1,
            grid=(t // tb,),
            in_specs=[pl.BlockSpec((tb, d), lambda i, dr: (i, 0)),
                      pl.BlockSpec((tb, EXPERT_TOPK), lambda i, dr: (i, 0)),
                      pl.BlockSpec((1, d), lambda i, dr: (0, 0)),
                      pl.BlockSpec(memory_space=pl.ANY)],
            out_specs=pl.BlockSpec((tb, d), lambda i, dr: (i, 0)),
            scratch_shapes=[pltpu.VMEM((2, EXPERT_TOPK, tb, d // 2), jnp.uint32), pltpu.SemaphoreType.DMA((2,))],
        ),
        compiler_params=_cp("arbitrary"),
        name="moe_combine",
    )(dest_flat, x, wts_tok, g_final.reshape(1, d), ys)


def hier_moe_final(x2, g_ffn, w_rg, b_rg, w_re, w_gate, w_up, w_down, g_final):
    t, d = x2.shape
    xn, eid, wts = moe_route(x2, g_ffn, w_rg, b_rg, w_re)
    n = EXPERT_TOPK * t
    e_flat = eid.reshape(1, n)
    rank, counts_f = moe_rank(e_flat)
    counts = counts_f.reshape(N_EXPERTS).astype(jnp.int32)
    padded = (counts + MOE_BLOCK - 1) // MOE_BLOCK * MOE_BLOCK
    pad_end = jnp.cumsum(padded)
    pad_start = pad_end - padded
    dest = (jnp.take(pad_start, e_flat[0]) + rank[0]).astype(jnp.int32)
    nblk = -(-n // MOE_BLOCK) + N_EXPERTS
    blk_start = jnp.arange(nblk, dtype=jnp.int32) * MOE_BLOCK
    blk_expert = jnp.minimum(jnp.sum(pad_end[None, :] <= blk_start[:, None], axis=1), N_EXPERTS - 1).astype(jnp.int32)
    n_used = (pad_end[-1:] // MOE_BLOCK).astype(jnp.int32)
    xs = moe_dispatch(xn, dest, counts, pad_start.astype(jnp.int32), nblk * MOE_BLOCK)
    ys = moe_experts(xs, blk_expert, n_used, w_gate, w_up, w_down)
    return moe_combine(x2, ys, dest, wts.T, g_final)


def kernel(x, mem, g_mix, w_in, b_gate, gdn_conv, gdn_a_log, gdn_dt_bias, gdn_norm_g, g_mem, w_mem_kv, w_o_swa, w_o_gdn, w_o_mem, w_out, g_ffn, w_route_group, b_route_group, w_route_expert, w_expert_gate, w_expert_up, w_expert_down, g_final):
    batch, seq, d = x.shape
    depth = w_in.shape[0]
    assert depth == 1, "the final RMSNorm is fused into the layer's MoE combine"
    t = batch * seq
    mem_tokens = mem.shape[1]
    h = x.reshape(t, d)
    o0 = 3 * SWA_WIDTH
    o1 = o0 + 3 * GDN_WIDTH
    o2 = o1 + GDN_WIDTH
    o3 = o2 + 4 * GDN_HEADS
    o4 = o3 + MEM_WIDTH
    tables = rope_tables(seq)
    for l in range(depth):
        w = w_in[l].astype(BF16)
        a = rmsnorm_rows(h, g_mix[l], BF16)
        qkv_a = matmul(a, w, F32, n=o0, rope=(tables, seq))
        o_a = dilated_swa(qkv_a, batch, seq)
        qkv_b = matmul_conv_norm(a, w, o0, gdn_conv[l], seq)
        z_b = matmul(a, w, BF16, n=GDN_WIDTH, col_off=o1)
        w_ba = jnp.zeros((d, LANES), BF16).at[:, :o3 - o2].set(w[:, o2:o3])
        beta, gc = gdn_gates(matmul(a, w_ba, F32), gdn_a_log[l], gdn_dt_bias[l])
        o_f, o_r = gdn_scan(qkv_b, beta, gc, batch, seq)
        o_b = gdn_post(o_f, o_r, z_b, gdn_norm_g[l])
        kv = matmul(rmsnorm_rows(mem.reshape(batch * mem_tokens, d), g_mem[l], BF16), w_mem_kv[l].astype(BF16), BF16)
        o_m = mem_attention(matmul(a, w[:, o3:o4], BF16), kv, batch, seq, mem_tokens)
        gates = matmul(a, w[:, o4:], BF16, bias=b_gate[l].reshape(-1))
        mix = gated_merge(o_a, o_b, o_m, w_o_swa[l].astype(BF16), w_o_gdn[l].astype(BF16),
                          w_o_mem[l].astype(BF16), gates)
        x2 = matmul(mix, w_out[l].astype(BF16), F32, residual=h)
        h = hier_moe_final(x2, g_ffn[l], w_route_group[l], b_route_group[l], w_route_expert[l],
                           w_expert_gate[l], w_expert_up[l], w_expert_down[l], g_final)
    return h.reshape(batch, seq, d)
```
